```python
import math
import jax, jax.numpy as jnp
from jax import lax
import numpy as np

D_MODEL = 1024
BATCH = 4
SEQ = 4096
DEPTH = 2

N_META = 16
BLOCK = 128
A_HEADS = 8
A_HEAD_DIM = 64
IDX_HEADS = 8
IDX_DIM = 64
TOPK_MAX = 256
N_BUCKETS = 32
MAX_DISTANCE = 128
B_HEADS = 4
B_QK_DIM = 128
B_V_DIM = 256
ROPE_BASE = 10000.0
C_HEADS = 16
C_HEAD_DIM = 64
D_FF = 2816
CONV_WIDTH = 3
EPS = 1e-6

A_Q = A_HEADS * A_HEAD_DIM
A_IQ = IDX_HEADS * IDX_DIM
B_QK = B_HEADS * B_QK_DIM
B_V = B_HEADS * B_V_DIM
EVEN_SPLITS = [A_Q, A_Q, A_Q, A_IQ, IDX_DIM, IDX_HEADS, B_QK, B_QK, B_V, B_V]
EVEN_IN = sum(EVEN_SPLITS)
EVEN_OUT = A_Q + B_V
C_W = C_HEADS * C_HEAD_DIM
ODD_IN = 3 * C_W

kernel_name = "hybrid_dsa_retention_stickbreaking_convffn"


def rmsnorm(x, g):
    xf = x.astype(jnp.float32)
    y = xf * lax.rsqrt(jnp.mean(jnp.square(xf), axis=-1, keepdims=True) + EPS)
    return (y * g.astype(jnp.float32)).astype(x.dtype)


def t5_bucket(rel):
    n = jnp.maximum(rel, 0)
    max_exact = N_BUCKETS // 2
    large = max_exact + (jnp.log(jnp.maximum(n, 1).astype(jnp.float32) / max_exact)
                         / math.log(MAX_DISTANCE / max_exact)
                         * (N_BUCKETS - max_exact)).astype(jnp.int32)
    large = jnp.minimum(large, N_BUCKETS - 1)
    return jnp.where(n < max_exact, n, large)


def rotary(x, pos):
    half = x.shape[-1] // 2
    inv = 1.0 / (ROPE_BASE ** (jnp.arange(half, dtype=jnp.float32) / half))
    ang = pos.astype(jnp.float32)[:, None] * inv[None, :]
    cos = jnp.cos(ang)[:, None, :]
    sin = jnp.sin(ang)[:, None, :]
    xf = x.astype(jnp.float32)
    x1, x2 = xf[..., :half], xf[..., half:]
    return jnp.concatenate([x1 * cos - x2 * sin, x2 * cos + x1 * sin], axis=-1).astype(x.dtype)


def dsa_attention(q, k, v, qi, ki, wi, rel_bias, n_keep):
    Bsz, L = q.shape[:2]
    nblk = L // BLOCK
    kpos = jnp.arange(L)
    is_meta = kpos < N_META

    def block(args):
        qb, qib, wib, qpos = args
        s = jnp.einsum('bqhd,bkd->bqhk', qib, ki).astype(jnp.float32) * (IDX_DIM ** -0.5)
        score = jnp.einsum('bqhk,bqh->bqk', jax.nn.relu(s), wib.astype(jnp.float32))
        causal = kpos[None, :] <= qpos[:, None]
        score = jnp.where(causal[None], jnp.where(is_meta[None, None, :], jnp.inf, score), -jnp.inf)
        _, idx = lax.top_k(score, n_keep)
        gather = jax.vmap(lambda a, i: a[i])
        ksel = gather(k, idx)
        vsel = gather(v, idx)
        rel = qpos[None, :, None] - idx
        bias = jnp.moveaxis(rel_bias[t5_bucket(rel)], -1, 2)
        logits = jnp.einsum('bqhd,bqkhd->bqhk', qb, ksel).astype(jnp.float32) * (A_HEAD_DIM ** -0.5)
        logits = logits + bias.astype(jnp.float32)
        logits = jnp.where((rel >= 0)[:, :, None, :], logits, -jnp.inf)
        p = jax.nn.softmax(logits, axis=-1).astype(v.dtype)
        return jnp.einsum('bqhk,bqkhd->bqhd', p, vsel)

    to_blocks = lambda a: jnp.moveaxis(a.reshape(Bsz, nblk, BLOCK, *a.shape[2:]), 1, 0)
    out = lax.map(block, (to_blocks(q), to_blocks(qi), to_blocks(wi), kpos.reshape(nblk, BLOCK)))
    return jnp.moveaxis(out, 0, 1).reshape(Bsz, L, *q.shape[2:])


def retention(q, k, v):
    Bsz, L, H, dk = q.shape
    dv = v.shape[-1]
    C = BLOCK
    N = L // C
    lg = jnp.log(1.0 - 2.0 ** (-5.0 - jnp.arange(H, dtype=jnp.float32)))
    n = jnp.arange(C, dtype=jnp.float32)
    diff = n[:, None] - n[None, :]
    dmat = jnp.where(diff[None] >= 0, jnp.exp(jnp.maximum(diff, 0.0)[None] * lg[:, None, None]), 0.0)
    xi = jnp.exp((n[None, :] + 1.0) * lg[:, None])[..., None]
    zeta = jnp.exp((C - 1.0 - n[None, :]) * lg[:, None])[..., None]
    g_chunk = jnp.exp(C * lg)[:, None, None]
    chunks = lambda a: a.astype(jnp.float32).reshape(Bsz, N, C, H, a.shape[-1]).transpose(1, 0, 3, 2, 4)
    qc, kc, vc = chunks(q), chunks(k * (dk ** -0.5)), chunks(v)

    def step(R, inp):
        qi, ki, vi = inp
        inner = jnp.einsum('bhcd,bhed->bhce', qi, ki) * dmat
        o = jnp.einsum('bhce,bhef->bhcf', inner, vi) + jnp.einsum('bhcd,bhdf->bhcf', qi, R) * xi
        R = R * g_chunk + jnp.einsum('bhcd,bhcf->bhdf', ki * zeta, vi)
        return R, o

    R0 = jnp.zeros((Bsz, H, dk, dv), jnp.float32)
    _, o = lax.scan(step, R0, (qc, kc, vc))
    return o.transpose(1, 0, 3, 2, 4).reshape(Bsz, L, H, dv)


def stick_breaking(q, k, v):
    L = q.shape[1]
    scale = q.shape[-1] ** -0.5
    outs = []
    for i in range(L // BLOCK):
        s, e = i * BLOCK, (i + 1) * BLOCK
        z = jnp.einsum('bqhd,bkhd->bhqk', q[:, s:e], k[:, :e]).astype(jnp.float32) * scale
        mask = (jnp.arange(e)[None, :] < jnp.arange(s, e)[:, None])[None, None]
        log_1m = jnp.where(mask, jax.nn.log_sigmoid(-z), 0.0)
        rest = lax.cumsum(log_1m, axis=3, reverse=True) - log_1m
        a = jnp.where(mask, jnp.exp(jax.nn.log_sigmoid(z) + rest), 0.0)
        outs.append(jnp.einsum('bhqk,bkhd->bqhd', a.astype(v.dtype), v[:, :e]))
    return jnp.concatenate(outs, axis=1)


def even_mixer(h, w_in, gn_gain, w_out, rel_bias, n_keep):
    Bsz, L, _ = h.shape
    p = h @ w_in
    splits = [int(s) for s in np.cumsum(EVEN_SPLITS)[:-1]]
    aq, ak, av, iq, ik, iw, bq, bk, bv, bg = jnp.split(p, splits, axis=-1)
    heads = lambda t, nh: t.reshape(Bsz, L, nh, -1)
    a_out = dsa_attention(heads(aq, A_HEADS), heads(ak, A_HEADS), heads(av, A_HEADS),
                          heads(iq, IDX_HEADS), ik, iw * (IDX_HEADS ** -0.5), rel_bias, n_keep)
    a_out = a_out.reshape(Bsz, L, A_Q)
    pos = jnp.arange(L)
    bq = rotary(heads(bq, B_HEADS), pos)
    bk = rotary(heads(bk, B_HEADS), pos)
    lead = (-N_META) % BLOCK
    frame = lambda t: jnp.pad(t[:, :L - lead], ((0, 0), (lead, 0), (0, 0), (0, 0)))
    r = retention(frame(bq), frame(bk), frame(heads(bv, B_HEADS)))
    r = jnp.pad(r[:, lead:], ((0, 0), (0, lead), (0, 0), (0, 0)))
    mu = jnp.mean(r, axis=-1, keepdims=True)
    var = jnp.mean(jnp.square(r - mu), axis=-1, keepdims=True)
    r = ((r - mu) * lax.rsqrt(var + EPS)).reshape(Bsz, L, B_V) * gn_gain.astype(jnp.float32)
    r_out = (r * jax.nn.silu(bg.astype(jnp.float32))).astype(h.dtype)
    return jnp.concatenate([a_out, r_out], axis=-1) @ w_out


def odd_mixer(h, w_in, w_out):
    Bsz, L, _ = h.shape
    q, k, v = jnp.split(h @ w_in, 3, axis=-1)
    heads = lambda t: t.reshape(Bsz, L, C_HEADS, C_HEAD_DIM)
    return stick_breaking(heads(q), heads(k), heads(v)).reshape(Bsz, L, C_W) @ w_out


def conv_ffn(h, w_up, w_gate, conv_w, conv_b, w_down):
    u = h @ w_up
    g = h @ w_gate
    g = lax.conv_general_dilated(g, conv_w[:, None, :].astype(g.dtype), window_strides=(1,),
                                 padding=[(CONV_WIDTH - 1, 0)],
                                 dimension_numbers=('NWC', 'WIO', 'NWC'),
                                 feature_group_count=D_FF) + conv_b
    return (jax.nn.silu(g) * u) @ w_down


def setup_inputs(seed: int = 0) -> dict:
    key = jax.random.key(seed)
    ks = jax.random.split(key, 20)
    f32 = jnp.float32
    ne, no = (DEPTH + 1) // 2, DEPTH // 2
    nrm = lambda k, shape, fan_in: jax.random.normal(k, shape, f32) * (fan_in ** -0.5)
    gain = lambda k, shape: 1.0 + 0.05 * jax.random.normal(k, shape, f32)
    return {
        "x": jax.random.normal(ks[0], (BATCH, SEQ, D_MODEL), f32),
        "meta_tokens": jax.random.normal(ks[1], (N_META, D_MODEL), f32),
        "rel_bias": 0.5 * jax.random.normal(ks[2], (N_BUCKETS, A_HEADS), f32),
        "norm_mix": gain(ks[3], (DEPTH, D_MODEL)),
        "norm_ffn": gain(ks[4], (DEPTH, D_MODEL)),
        "norm_final": gain(ks[5], (D_MODEL,)),
        "even_w_in": nrm(ks[6], (ne, D_MODEL, EVEN_IN), D_MODEL),
        "even_gn_gain": gain(ks[7], (ne, B_V)),
        "even_w_out": nrm(ks[8], (ne, EVEN_OUT, D_MODEL), EVEN_OUT),
        "odd_w_in": nrm(ks[9], (no, D_MODEL, ODD_IN), D_MODEL),
        "odd_w_out": nrm(ks[10], (no, C_W, D_MODEL), C_W),
        "ffn_w_up": nrm(ks[11], (DEPTH, D_MODEL, D_FF), D_MODEL),
        "ffn_w_gate": nrm(ks[12], (DEPTH, D_MODEL, D_FF), D_MODEL),
        "ffn_conv_w": nrm(ks[13], (DEPTH, CONV_WIDTH, D_FF), CONV_WIDTH),
        "ffn_conv_b": 0.01 * jax.random.normal(ks[14], (DEPTH, D_FF), f32),
        "ffn_w_down": nrm(ks[15], (DEPTH, D_FF, D_MODEL), D_FF),
    }


def reference(x, meta_tokens, rel_bias, norm_mix, norm_ffn, norm_final,
              even_w_in, even_gn_gain, even_w_out, odd_w_in, odd_w_out,
              ffn_w_up, ffn_w_gate, ffn_conv_w, ffn_conv_b, ffn_w_down):
    Bsz, S, D = x.shape
    lead = (-N_META) % BLOCK
    n_keep = min(TOPK_MAX, S // 4)
    meta = jnp.broadcast_to(meta_tokens[None].astype(x.dtype), (Bsz, N_META, D))
    h = jnp.concatenate([meta, x, jnp.zeros((Bsz, lead, D), x.dtype)], axis=1)
    for l in range(DEPTH):
        hn = rmsnorm(h, norm_mix[l])
        if l % 2 == 0:
            j = l // 2
            h = h + even_mixer(hn, even_w_in[j], even_gn_gain[j], even_w_out[j], rel_bias, n_keep)
        else:
            j = l // 2
            h = h + odd_mixer(hn, odd_w_in[j], odd_w_out[j])
        h = h + conv_ffn(rmsnorm(h, norm_ffn[l]), ffn_w_up[l], ffn_w_gate[l],
                         ffn_conv_w[l], ffn_conv_b[l], ffn_w_down[l])
    return rmsnorm(h, norm_final)[:, N_META:N_META + S]
```

```python
import functools
import math

import jax
import jax.numpy as jnp
import numpy as np
from jax import lax
from jax.experimental import pallas as pl
from jax.experimental.pallas import tpu as pltpu

N_META = 16
BLK = 256
LEAD = BLK - N_META
A_HEADS, A_HEAD_DIM = 8, 64
IDX_HEADS, IDX_DIM = 8, 64
TOPK_MAX = 256
N_BUCKETS, MAX_DISTANCE = 32, 128
B_HEADS, B_QK_DIM, B_V_DIM = 4, 128, 256
ROPE_BASE = 10000.0
C_HEADS, C_HEAD_DIM = 16, 64
CONV_WIDTH = 3
EPS = 1e-6
BIG = 1e30
BISECT_ITERS = 36
VMEM_LIMIT_BYTES = 56 * 1024 * 1024

F32 = jnp.float32
BF16 = jnp.bfloat16
NT_DIMS = (((1,), (1,)), ((), ()))
TN_DIMS = (((0,), (0,)), ((), ()))


def _params(*sem):
    return pltpu.CompilerParams(dimension_semantics=sem, vmem_limit_bytes=VMEM_LIMIT_BYTES)


def _norm_matmul_kernel(x_ref, g_ref, w_ref, o_ref, xn_ref):
    @pl.when(pl.program_id(1) == 0)
    def _():
        x = x_ref[...]
        ms = jnp.mean(x * x, axis=-1, keepdims=True)
        xn_ref[...] = (x * lax.rsqrt(ms + EPS) * g_ref[...]).astype(BF16)

    o_ref[...] = jnp.dot(xn_ref[...], w_ref[...], preferred_element_type=F32).astype(o_ref.dtype)


def _norm_matmul(x, gain, w, tm, tn):
    rows, d = x.shape
    n = w.shape[1]
    return pl.pallas_call(
        _norm_matmul_kernel,
        out_shape=jax.ShapeDtypeStruct((rows, n), BF16),
        grid=(rows // tm, n // tn),
        in_specs=[pl.BlockSpec((tm, d), lambda i, j: (i, 0)),
                  pl.BlockSpec((1, d), lambda i, j: (0, 0)),
                  pl.BlockSpec((d, tn), lambda i, j: (0, j))],
        out_specs=pl.BlockSpec((tm, tn), lambda i, j: (i, j)),
        scratch_shapes=[pltpu.VMEM((tm, d), BF16)],
        compiler_params=_params("parallel", "arbitrary"),
        name="norm_matmul",
    )(x, gain.reshape(1, d), w)


def _transpose_kernel(x_ref, o_ref):
    o_ref[...] = x_ref[...].astype(F32).T.astype(o_ref.dtype)


def _block_transpose(p, nbatch, nb, col_block, width):
    return pl.pallas_call(
        _transpose_kernel,
        out_shape=jax.ShapeDtypeStruct((nbatch, nb, width, BLK), p.dtype),
        grid=(nbatch, nb),
        in_specs=[pl.BlockSpec((BLK, width), lambda b, i: (b * nb + i, col_block))],
        out_specs=pl.BlockSpec((None, None, width, BLK), lambda b, i: (b, i, 0, 0)),
        compiler_params=_params("parallel", "parallel"),
        name="block_transpose",
    )(p)


def _bucket_tiles():
    k = np.arange(BLK)[:, None]
    q = np.arange(BLK)[None, :]
    tiles = []
    for t in range(2):
        n = np.maximum(q - k + t * BLK, 0)
        max_exact = N_BUCKETS // 2
        large = max_exact + (np.log(np.maximum(n, 1).astype(np.float32) / max_exact)
                             / math.log(MAX_DISTANCE / max_exact)
                             * (N_BUCKETS - max_exact)).astype(np.int32)
        large = np.minimum(large, N_BUCKETS - 1)
        tiles.append(np.where(n < max_exact, n, large).astype(np.int32))
    return np.stack(tiles)


def _bias_tiles_kernel(rb_ref, idx_ref, o_ref):
    h = pl.program_id(0)
    for t in range(2):
        idx = idx_ref[t]
        acc = jnp.zeros((BLK, BLK), F32)
        for b in range(N_BUCKETS):
            acc = jnp.where(idx == b, rb_ref[b, h], acc)
        o_ref[t] = acc
    o_ref[2] = jnp.full((BLK, BLK), rb_ref[N_BUCKETS - 1, h], F32)


def _bias_tiles(rel_bias):
    return pl.pallas_call(
        _bias_tiles_kernel,
        out_shape=jax.ShapeDtypeStruct((A_HEADS, 3, BLK, BLK), F32),
        grid=(A_HEADS,),
        in_specs=[pl.BlockSpec(memory_space=pltpu.SMEM),
                  pl.BlockSpec((2, BLK, BLK), lambda h: (0, 0, 0))],
        out_specs=pl.BlockSpec((None, 3, BLK, BLK), lambda h: (h, 0, 0, 0)),
        compiler_params=_params("parallel"),
        name="t5_bias_tiles",
    )(rel_bias.astype(F32), jnp.asarray(_bucket_tiles()))


def _dsa_kernel(aq_ref, iq_ref, ikwq_ref, ak_ref, avt_ref, ikw_ref, bias_ref, o_ref,
                s_ref, acc_ref, m_ref, l_ref, *, n_keep):
    i = pl.program_id(1)
    row = lax.broadcasted_iota(jnp.int32, (BLK, BLK), 0)
    col = lax.broadcasted_iota(jnp.int32, (BLK, BLK), 1)
    qf = col + i * BLK

    s_ref[0] = jnp.where((row >= LEAD) & (row <= qf), BIG, -BIG)
    eye = (lax.broadcasted_iota(jnp.int32, (128, 128), 0)
           == lax.broadcasted_iota(jnp.int32, (128, 128), 1)).astype(BF16)
    ikwq_t = lax.dot_general(eye, ikwq_ref[...], NT_DIMS, preferred_element_type=F32)
    w_scale = (IDX_DIM ** -0.5) * (IDX_HEADS ** -0.5)
    w_rows = [ikwq_t[IDX_DIM + h:IDX_DIM + h + 1, :] * w_scale for h in range(IDX_HEADS)]
    iq_heads = [iq_ref[:, h * IDX_DIM:(h + 1) * IDX_DIM] for h in range(IDX_HEADS)]

    def score_block(j, mabs):
        ik_j = ikw_ref[j, :, :IDX_DIM]
        acc = jnp.zeros((BLK, BLK), F32)
        for h in range(IDX_HEADS):
            st = lax.dot_general(ik_j, iq_heads[h], NT_DIMS, preferred_element_type=F32)
            acc = acc + jnp.maximum(st, 0.0) * w_rows[h]
        causal = (row + j * BLK) <= qf
        s_ref[j] = jnp.where(causal, acc, -BIG)
        return jnp.maximum(mabs, jnp.max(jnp.where(causal, jnp.abs(acc), 0.0), axis=0, keepdims=True))

    mabs = lax.fori_loop(1, i + 1, score_block, jnp.zeros((1, BLK), F32))

    k_eff = float(n_keep - N_META)
    bound = mabs * 1.000001 + 1e-30

    def count_ge(thr):
        def body(j, c):
            return c + jnp.sum((s_ref[j] >= thr).astype(F32), axis=0, keepdims=True)
        return lax.fori_loop(1, i + 1, body, jnp.zeros((1, BLK), F32))

    def bisect(_, lohi):
        lo, hi = lohi
        mid = lo + (hi - lo) * 0.5
        ge = count_ge(mid) >= k_eff
        return jnp.where(ge, mid, lo), jnp.where(ge, hi, mid)

    lo, _ = lax.fori_loop(0, BISECT_ITERS, bisect, (-bound, bound))

    def vmin_body(j, v):
        s = s_ref[j]
        return jnp.minimum(v, jnp.min(jnp.where(s >= lo, s, BIG), axis=0, keepdims=True))

    v = lax.fori_loop(1, i + 1, vmin_body, jnp.full((1, BLK), BIG, F32))
    v = jnp.where(v >= BIG, 0.0, v)

    def count_gt(j, c):
        return c + jnp.sum((s_ref[j] > v).astype(F32), axis=0, keepdims=True)

    ties_wanted = k_eff - lax.fori_loop(1, i + 1, count_gt, jnp.zeros((1, BLK), F32))

    acc_ref[...] = jnp.zeros_like(acc_ref)
    m_ref[...] = jnp.full_like(m_ref, -BIG)
    l_ref[...] = jnp.zeros_like(l_ref)
    tri_incl = (col <= row).astype(BF16)
    aq_heads = [aq_ref[:, h * A_HEAD_DIM:(h + 1) * A_HEAD_DIM] * (A_HEAD_DIM ** -0.5) for h in range(A_HEADS)]

    def attend(j, ties_seen):
        s = s_ref[j]
        eq = s == v
        rank = ties_seen + jnp.dot(tri_incl, eq.astype(BF16), preferred_element_type=F32)
        sel = (s > v) | (eq & (rank <= ties_wanted))
        d = jnp.minimum(i - j, 2)
        for h in range(A_HEADS):
            z = lax.dot_general(ak_ref[j, :, h * A_HEAD_DIM:(h + 1) * A_HEAD_DIM], aq_heads[h], NT_DIMS,
                                preferred_element_type=F32) + bias_ref[h, d]
            m_old = m_ref[h:h + 1, :]
            m_new = jnp.maximum(m_old, jnp.max(jnp.where(sel, z, -BIG), axis=0, keepdims=True))
            alpha = jnp.exp(m_old - m_new)
            p = jnp.where(sel, jnp.exp(z - m_new), 0.0)
            l_ref[h:h + 1, :] = alpha * l_ref[h:h + 1, :] + jnp.sum(p, axis=0, keepdims=True)
            pv = jnp.dot(avt_ref[j, h * A_HEAD_DIM:(h + 1) * A_HEAD_DIM, :], p.astype(BF16),
                         preferred_element_type=F32)
            hs = slice(h * A_HEAD_DIM, (h + 1) * A_HEAD_DIM)
            acc_ref[hs, :] = alpha * acc_ref[hs, :] + pv
            m_ref[h:h + 1, :] = m_new
        return rank[BLK - 1:BLK, :]

    lax.fori_loop(0, i + 1, attend, jnp.zeros((1, BLK), F32))

    outs = []
    for h in range(A_HEADS):
        l = l_ref[h:h + 1, :]
        hs = slice(h * A_HEAD_DIM, (h + 1) * A_HEAD_DIM)
        outs.append(jnp.where(l > 0.0, acc_ref[hs, :] / jnp.where(l > 0.0, l, 1.0), 0.0))
    o_ref[...] = jnp.concatenate(outs, axis=0).T.astype(o_ref.dtype)


def _dsa_attention(p0, ikw, avt, bias, nbatch, nb, n_keep):
    rows = p0.shape[0]
    aw = A_HEADS * A_HEAD_DIM
    p0b = p0.reshape(nbatch, nb, BLK, p0.shape[1])
    ikwb = ikw.reshape(nbatch, nb, BLK, ikw.shape[1])
    return pl.pallas_call(
        functools.partial(_dsa_kernel, n_keep=n_keep),
        out_shape=jax.ShapeDtypeStruct((rows, aw), BF16),
        grid=(nbatch, nb),
        in_specs=[pl.BlockSpec((BLK, aw), lambda b, i: (b * nb + i, 0)),
                  pl.BlockSpec((BLK, aw), lambda b, i: (b * nb + i, 3)),
                  pl.BlockSpec((BLK, 128), lambda b, i: (b * nb + i, 0)),
                  pl.BlockSpec((None, nb, BLK, aw), lambda b, i: (b, 0, 0, 1)),
                  pl.BlockSpec((None, nb, aw, BLK), lambda b, i: (b, 0, 0, 0)),
                  pl.BlockSpec((None, nb, BLK, 128), lambda b, i: (b, 0, 0, 0)),
                  pl.BlockSpec((A_HEADS, 3, BLK, BLK), lambda b, i: (0, 0, 0, 0))],
        out_specs=pl.BlockSpec((BLK, aw), lambda b, i: (b * nb + i, 0)),
        scratch_shapes=[pltpu.VMEM((nb, BLK, BLK), F32),
                        pltpu.VMEM((aw, BLK), F32),
                        pltpu.VMEM((A_HEADS, BLK), F32),
                        pltpu.VMEM((A_HEADS, BLK), F32)],
        compiler_params=_params("parallel", "arbitrary"),
        name="dsa_attention",
    )(p0, p0, ikw, p0b, avt, ikwb, bias)


def _retention_kernel(bq_ref, bk_ref, bv_ref, bg_ref, cos_ref, sin_ref, dmat_ref, xi_ref, zeta_ref,
                      gain_ref, o_ref, r_ref, *, g_chunk):
    @pl.when(pl.program_id(1) == 0)
    def _():
        r_ref[...] = jnp.zeros_like(r_ref)

    cosf = cos_ref[...]
    sinf = sin_ref[...]

    def rot(x):
        return x * cosf + pltpu.roll(x, B_QK_DIM // 2, 1) * sinf

    for h in range(B_HEADS):
        ks = slice(h * B_QK_DIM, (h + 1) * B_QK_DIM)
        vs = slice(h * B_V_DIM, (h + 1) * B_V_DIM)
        q = rot(bq_ref[:, ks].astype(F32))
        k = rot(bk_ref[:, ks].astype(F32)) * (B_QK_DIM ** -0.5)
        qb = q.astype(BF16)
        kb = k.astype(BF16)
        v = bv_ref[:, vs]
        inner = lax.dot_general(qb, kb, NT_DIMS, preferred_element_type=F32) * dmat_ref[h]
        r_old = r_ref[h]
        o = (jnp.dot(inner.astype(BF16), v, preferred_element_type=F32)
             + jnp.dot(qb, r_old.astype(BF16), preferred_element_type=F32) * xi_ref[h])
        kz = (k * zeta_ref[h]).astype(BF16)
        r_ref[h] = r_old * g_chunk[h] + lax.dot_general(kz, v, TN_DIMS, preferred_element_type=F32)
        mu = jnp.mean(o, axis=-1, keepdims=True)
        oc = o - mu
        var = jnp.mean(oc * oc, axis=-1, keepdims=True)
        rn = oc * lax.rsqrt(var + EPS) * gain_ref[:, vs]
        gate = bg_ref[:, vs].astype(F32)
        o_ref[:, vs] = (rn * (gate / (1.0 + jnp.exp(-gate)))).astype(o_ref.dtype)


def _retention(p0, gn_gain, nbatch, nb):
    rows = p0.shape[0]
    qkw, vw = B_HEADS * B_QK_DIM, B_HEADS * B_V_DIM
    frame = nb * BLK
    half = B_QK_DIM // 2
    pos = (jnp.arange(frame) - LEAD).astype(F32)
    inv = 1.0 / (ROPE_BASE ** (jnp.arange(half, dtype=F32) / half))
    ang = pos[:, None] * inv[None, :]
    cosf = jnp.concatenate([jnp.cos(ang), jnp.cos(ang)], axis=-1)
    sinf = jnp.concatenate([-jnp.sin(ang), jnp.sin(ang)], axis=-1)
    lg = jnp.log(1.0 - 2.0 ** (-5.0 - jnp.arange(B_HEADS, dtype=F32)))
    n = jnp.arange(BLK, dtype=F32)
    diff = n[:, None] - n[None, :]
    dmat = jnp.where(diff[None] >= 0, jnp.exp(jnp.maximum(diff, 0.0)[None] * lg[:, None, None]), 0.0)
    xi = jnp.broadcast_to(jnp.exp((n[None, :] + 1.0) * lg[:, None])[..., None], (B_HEADS, BLK, B_V_DIM))
    zeta = jnp.broadcast_to(jnp.exp((BLK - 1.0 - n[None, :]) * lg[:, None])[..., None],
                            (B_HEADS, BLK, B_QK_DIM))
    g_chunk = tuple(float(math.exp(BLK * math.log(1.0 - 2.0 ** (-5.0 - h)))) for h in range(B_HEADS))
    const = lambda shape: pl.BlockSpec(shape, lambda b, i: (0,) * len(shape))
    return pl.pallas_call(
        functools.partial(_retention_kernel, g_chunk=g_chunk),
        out_shape=jax.ShapeDtypeStruct((rows, vw), BF16),
        grid=(nbatch, nb),
        in_specs=[pl.BlockSpec((BLK, qkw), lambda b, i: (b * nb + i, 4)),
                  pl.BlockSpec((BLK, qkw), lambda b, i: (b * nb + i, 5)),
                  pl.BlockSpec((BLK, vw), lambda b, i: (b * nb + i, 3)),
                  pl.BlockSpec((BLK, vw), lambda b, i: (b * nb + i, 4)),
                  pl.BlockSpec((BLK, B_QK_DIM), lambda b, i: (i, 0)),
                  pl.BlockSpec((BLK, B_QK_DIM), lambda b, i: (i, 0)),
                  const((B_HEADS, BLK, BLK)),
                  const((B_HEADS, BLK, B_V_DIM)),
                  const((B_HEADS, BLK, B_QK_DIM)),
                  const((1, vw))],
        out_specs=pl.BlockSpec((BLK, vw), lambda b, i: (b * nb + i, 0)),
        scratch_shapes=[pltpu.VMEM((B_HEADS, B_QK_DIM, B_V_DIM), F32)],
        compiler_params=_params("parallel", "arbitrary"),
        name="retention",
    )(p0, p0, p0, p0, cosf, sinf, dmat, xi, zeta, gn_gain.reshape(1, vw).astype(F32))


def _proj_residual_kernel(*refs, n_pairs):
    h_ref = refs[2 * n_pairs]
    o_ref = refs[2 * n_pairs + 1]
    acc = h_ref[...]
    for t in range(n_pairs):
        acc = acc + jnp.dot(refs[2 * t][...], refs[2 * t + 1][...], preferred_element_type=F32)
    o_ref[...] = acc


def _proj_residual(pairs, h, tm):
    rows, d = h.shape
    in_specs, args = [], []
    for a, w in pairs:
        in_specs += [pl.BlockSpec((tm, a.shape[1]), lambda i: (i, 0)),
                     pl.BlockSpec(w.shape, lambda i: (0, 0))]
        args += [a, w]
    in_specs.append(pl.BlockSpec((tm, d), lambda i: (i, 0)))
    return pl.pallas_call(
        functools.partial(_proj_residual_kernel, n_pairs=len(pairs)),
        out_shape=jax.ShapeDtypeStruct((rows, d), F32),
        grid=(rows // tm,),
        in_specs=in_specs,
        out_specs=pl.BlockSpec((tm, d), lambda i: (i, 0)),
        compiler_params=_params("parallel"),
        name="proj_residual",
    )(*args, h)


def _ffn_tail_kernel(u_ref, g_ref, halo_ref, cw_ref, cb_ref, wd_ref, h_ref, o_ref):
    g = g_ref[...].astype(F32)
    halo = halo_ref[...].astype(F32)
    tm = g.shape[0]
    rowi = lax.broadcasted_iota(jnp.int32, g.shape, 0)
    g1 = jnp.where(rowi >= 1, pltpu.roll(g, 1, 0), halo[15:16, :])
    g2 = jnp.where(rowi >= 2, pltpu.roll(g, 2, 0),
                   jnp.where(rowi == 1, halo[15:16, :], halo[14:15, :]))
    gc = g2 * cw_ref[0:1, :] + g1 * cw_ref[1:2, :] + g * cw_ref[2:3, :] + cb_ref[...]
    act = (gc / (1.0 + jnp.exp(-gc))) * u_ref[...].astype(F32)
    o_ref[...] = h_ref[...] + jnp.dot(act.astype(BF16), wd_ref[...], preferred_element_type=F32)


def _ffn_tail(ug, conv_w, conv_b, w_down, h, tm):
    rows, d = h.shape
    dff = w_down.shape[0]
    nblk = dff // 128
    assert dff % 128 == 0 and tm % 16 == 0
    return pl.pallas_call(
        _ffn_tail_kernel,
        out_shape=jax.ShapeDtypeStruct((rows, d), F32),
        grid=(rows // tm,),
        in_specs=[pl.BlockSpec((tm, dff), lambda i: (i, 0)),
                  pl.BlockSpec((tm, dff), lambda i: (i, 1)),
                  pl.BlockSpec((16, dff), lambda i: (jnp.maximum(i * (tm // 16) - 1, 0), 1)),
                  pl.BlockSpec((CONV_WIDTH, dff), lambda i: (0, 0)),
                  pl.BlockSpec((1, dff), lambda i: (0, 0)),
                  pl.BlockSpec((dff, d), lambda i: (0, 0)),
                  pl.BlockSpec((tm, d), lambda i: (i, 0))],
        out_specs=pl.BlockSpec((tm, d), lambda i: (i, 0)),
        compiler_params=_params("parallel"),
        name="ffn_tail",
    )(ug, ug, ug, conv_w.astype(F32), conv_b.reshape(1, dff).astype(F32), w_down, h)


SB_GROUP = 4


def _stick_breaking_kernel(q_ref, k_ref, vt_ref, o_ref, acc_ref):
    i = pl.program_id(2)
    row = lax.broadcasted_iota(jnp.int32, (BLK, BLK), 0)
    col = lax.broadcasted_iota(jnp.int32, (BLK, BLK), 1)
    qf = col + i * BLK
    tri_after = (col > row).astype(BF16)
    scale = C_HEAD_DIM ** -0.5

    for h in range(SB_GROUP):
        hs = slice(h * C_HEAD_DIM, (h + 1) * C_HEAD_DIM)
        q_h = q_ref[:, hs] * scale
        acc_ref[...] = jnp.zeros_like(acc_ref)

        def step(t, carry):
            j = i - t
            kf = row + j * BLK
            mask = (kf < qf) & (kf >= LEAD)
            z = lax.dot_general(k_ref[j][:, hs], q_h, NT_DIMS, preferred_element_type=F32)
            lm = -(jnp.maximum(z, 0.0) + jnp.log(1.0 + jnp.exp(-jnp.abs(z))))
            lm = jnp.where(mask, lm, 0.0)
            lm_hi = lm.astype(BF16)
            lm_lo = (lm - lm_hi.astype(F32)).astype(BF16)
            rest = (jnp.dot(tri_after, lm_hi, preferred_element_type=F32)
                    + jnp.dot(tri_after, lm_lo, preferred_element_type=F32))
            a = jnp.where(mask, jnp.exp(z + lm + rest + carry), 0.0)
            acc_ref[...] += jnp.dot(vt_ref[j][hs, :], a.astype(BF16), preferred_element_type=F32)
            return carry + jnp.sum(lm, axis=0, keepdims=True)

        lax.fori_loop(0, i + 1, step, jnp.zeros((1, BLK), F32))
        o_ref[:, hs] = acc_ref[...].T.astype(o_ref.dtype)


def _stick_breaking(p1, vt, nbatch, nb):
    rows = p1.shape[0]
    cw = C_HEADS * C_HEAD_DIM
    gw = SB_GROUP * C_HEAD_DIM
    ngroups = C_HEADS // SB_GROUP
    p1b = p1.reshape(nbatch, nb, BLK, p1.shape[1])
    return pl.pallas_call(
        _stick_breaking_kernel,
        out_shape=jax.ShapeDtypeStruct((rows, cw), BF16),
        grid=(nbatch, ngroups, nb),
        in_specs=[pl.BlockSpec((BLK, gw), lambda b, g, i: (b * nb + i, g)),
                  pl.BlockSpec((None, nb, BLK, gw), lambda b, g, i: (b, 0, 0, ngroups + g)),
                  pl.BlockSpec((None, nb, gw, BLK), lambda b, g, i: (b, 0, g, 0))],
        out_specs=pl.BlockSpec((BLK, gw), lambda b, g, i: (b * nb + i, g)),
        scratch_shapes=[pltpu.VMEM((C_HEAD_DIM, BLK), F32)],
        compiler_params=_params("parallel", "parallel", "arbitrary"),
        name="stick_breaking",
    )(p1, p1b, vt)


def _final_norm_kernel(x_ref, g_ref, o_ref):
    x = x_ref[...]
    ms = jnp.mean(x * x, axis=-1, keepdims=True)
    o_ref[...] = x * lax.rsqrt(ms + EPS) * g_ref[...]


def _final_norm(h, gain, nbatch, nb):
    d = h.shape[1]
    return pl.pallas_call(
        _final_norm_kernel,
        out_shape=jax.ShapeDtypeStruct((nbatch, (nb - 1) * BLK, d), F32),
        grid=(nbatch, nb - 1),
        in_specs=[pl.BlockSpec((BLK, d), lambda b, i: (b * nb + i + 1, 0)),
                  pl.BlockSpec((1, d), lambda b, i: (0, 0))],
        out_specs=pl.BlockSpec((None, BLK, d), lambda b, i: (b, i, 0)),
        compiler_params=_params("parallel", "parallel"),
        name="final_norm",
    )(h, gain.reshape(1, d).astype(F32))


def _pick_tile(rows, pref):
    t = pref
    while rows % t:
        t //= 2
    return t


def kernel(x, meta_tokens, rel_bias, norm_mix, norm_ffn, norm_final, even_w_in, even_gn_gain, even_w_out, odd_w_in, odd_w_out, ffn_w_up, ffn_w_gate, ffn_conv_w, ffn_conv_b, ffn_w_down):
    nbatch, seq, d = x.shape
    assert seq % BLK == 0
    nb = seq // BLK + 1
    rows = nbatch * nb * BLK
    n_keep = min(TOPK_MAX, seq // 4)
    assert n_keep >= N_META
    depth = norm_mix.shape[0]
    tm = _pick_tile(rows, 1024)

    meta = jnp.broadcast_to(meta_tokens[None].astype(x.dtype), (nbatch, N_META, d))
    h = jnp.concatenate([jnp.zeros((nbatch, LEAD, d), x.dtype), meta, x], axis=1).reshape(rows, d)

    aw = A_HEADS * A_HEAD_DIM
    qkw, vw = B_HEADS * B_QK_DIM, B_HEADS * B_V_DIM
    cw = C_HEADS * C_HEAD_DIM
    bias = _bias_tiles(rel_bias)

    for l in range(depth):
        j = l // 2
        if l % 2 == 0:
            w = even_w_in[j]
            o_iq, o_ik, o_iw, o_bq = 3 * aw, 4 * aw, 4 * aw + IDX_DIM, 4 * aw + IDX_DIM + IDX_HEADS
            w_main = jnp.concatenate([w[:, :4 * aw], w[:, o_bq:]], axis=1).astype(BF16)
            w_idx = jnp.concatenate([w[:, o_ik:o_bq], jnp.zeros((d, 128 - IDX_DIM - IDX_HEADS), w.dtype)],
                                    axis=1).astype(BF16)
            p0 = _norm_matmul(h, norm_mix[l], w_main, tm, 512)
            ikw = _norm_matmul(h, norm_mix[l], w_idx, tm, 128)
            avt = _block_transpose(p0, nbatch, nb, 2, aw)
            a_out = _dsa_attention(p0, ikw, avt, bias, nbatch, nb, n_keep)
            r_out = _retention(p0, even_gn_gain[j], nbatch, nb)
            w_out = even_w_out[j].astype(BF16)
            h = _proj_residual([(a_out, w_out[:aw]), (r_out, w_out[aw:])], h, _pick_tile(rows, 512))
        else:
            p1 = _norm_matmul(h, norm_mix[l], odd_w_in[j].astype(BF16), tm, 512)
            vt = _block_transpose(p1, nbatch, nb, 2, cw)
            s_out = _stick_breaking(p1, vt, nbatch, nb)
            h = _proj_residual([(s_out, odd_w_out[j].astype(BF16))], h, _pick_tile(rows, 512))
        w_ug = jnp.concatenate([ffn_w_up[l], ffn_w_gate[l]], axis=1).astype(BF16)
        ug = _norm_matmul(h, norm_ffn[l], w_ug, tm, 512)
        h = _ffn_tail(ug, ffn_conv_w[l], ffn_conv_b[l], ffn_w_down[l].astype(BF16), h, _pick_tile(rows, 512))

    return _final_norm(h, norm_final, nbatch, nb)
```

```python
import functools
import math

import jax
import jax.numpy as jnp
import numpy as np
from jax import lax
from jax.experimental import pallas as pl
from jax.experimental.pallas import tpu as pltpu

N_META = 16
BLK = 256
LEAD = BLK - N_META
A_HEADS, A_HEAD_DIM = 8, 64
IDX_HEADS, IDX_DIM = 8, 64
TOPK_MAX = 256
N_BUCKETS, MAX_DISTANCE = 32, 128
B_HEADS, B_QK_DIM, B_V_DIM = 4, 128, 256
ROPE_BASE = 10000.0
C_HEADS, C_HEAD_DIM = 16, 64
CONV_WIDTH = 3
EPS = 1e-6
BIG = 1e30
BISECT_ITERS = 36
DSA_HEAD_GROUP = 8
VMEM_LIMIT_BYTES = 56 * 1024 * 1024

F32 = jnp.float32
BF16 = jnp.bfloat16
NT_DIMS = (((1,), (1,)), ((), ()))
TN_DIMS = (((0,), (0,)), ((), ()))


def _params(*sem):
    return pltpu.CompilerParams(dimension_semantics=sem, vmem_limit_bytes=VMEM_LIMIT_BYTES)


def _norm_matmul_kernel(x_ref, g_ref, w_ref, o_ref, xn_ref):
    @pl.when(pl.program_id(1) == 0)
    def _():
        x = x_ref[...]
        ms = jnp.mean(x * x, axis=-1, keepdims=True)
        xn_ref[...] = (x * lax.rsqrt(ms + EPS) * g_ref[...]).astype(BF16)

    o_ref[...] = jnp.dot(xn_ref[...], w_ref[...], preferred_element_type=F32).astype(o_ref.dtype)


def _norm_matmul(x, gain, w, tm, tn):
    rows, d = x.shape
    n = w.shape[1]
    return pl.pallas_call(
        _norm_matmul_kernel,
        out_shape=jax.ShapeDtypeStruct((rows, n), BF16),
        grid=(rows // tm, n // tn),
        in_specs=[pl.BlockSpec((tm, d), lambda i, j: (i, 0)),
                  pl.BlockSpec((1, d), lambda i, j: (0, 0)),
                  pl.BlockSpec((d, tn), lambda i, j: (0, j))],
        out_specs=pl.BlockSpec((tm, tn), lambda i, j: (i, j)),
        scratch_shapes=[pltpu.VMEM((tm, d), BF16)],
        compiler_params=_params("parallel", "arbitrary"),
        name="norm_matmul",
    )(x, gain.reshape(1, d), w)


def _bucket_tiles():
    q = np.arange(BLK)[:, None]
    k = np.arange(BLK)[None, :]
    tiles = []
    for t in range(2):
        n = np.maximum(q - k + t * BLK, 0)
        max_exact = N_BUCKETS // 2
        large = max_exact + (np.log(np.maximum(n, 1).astype(np.float32) / max_exact)
                             / math.log(MAX_DISTANCE / max_exact)
                             * (N_BUCKETS - max_exact)).astype(np.int32)
        large = np.minimum(large, N_BUCKETS - 1)
        tiles.append(np.where(n < max_exact, n, large).astype(np.int32))
    return np.stack(tiles)


def _bias_tiles_kernel(rb_ref, idx_ref, o_ref):
    h = pl.program_id(0)
    for t in range(2):
        idx = idx_ref[t]
        acc = jnp.zeros((BLK, BLK), F32)
        for b in range(N_BUCKETS):
            acc = jnp.where(idx == b, rb_ref[b, h], acc)
        o_ref[t] = acc
    o_ref[2] = jnp.full((BLK, BLK), rb_ref[N_BUCKETS - 1, h], F32)


def _bias_tiles(rel_bias):
    return pl.pallas_call(
        _bias_tiles_kernel,
        out_shape=jax.ShapeDtypeStruct((A_HEADS, 3, BLK, BLK), F32),
        grid=(A_HEADS,),
        in_specs=[pl.BlockSpec(memory_space=pltpu.SMEM),
                  pl.BlockSpec((2, BLK, BLK), lambda h: (0, 0, 0))],
        out_specs=pl.BlockSpec((None, 3, BLK, BLK), lambda h: (h, 0, 0, 0)),
        compiler_params=_params("parallel"),
        name="t5_bias_tiles",
    )(rel_bias.astype(F32), jnp.asarray(_bucket_tiles()))


def _dsa_kernel(aq_ref, iq_ref, ikwq_ref, ak_ref, av_ref, ikw_ref, bias_ref, o_ref,
                s_ref, acc_ref, *, n_keep):
    i = pl.program_id(1)
    row = lax.broadcasted_iota(jnp.int32, (BLK, BLK), 0)
    col = lax.broadcasted_iota(jnp.int32, (BLK, BLK), 1)
    qf = col + i * BLK

    s_ref[0] = jnp.where((row >= LEAD) & (row <= qf), BIG, -BIG)
    eye = (lax.broadcasted_iota(jnp.int32, (128, 128), 0)
           == lax.broadcasted_iota(jnp.int32, (128, 128), 1)).astype(BF16)
    ikwq_t = lax.dot_general(eye, ikwq_ref[...], NT_DIMS, preferred_element_type=F32)
    w_scale = (IDX_DIM ** -0.5) * (IDX_HEADS ** -0.5)
    w_rows = [ikwq_t[IDX_DIM + h:IDX_DIM + h + 1, :] * w_scale for h in range(IDX_HEADS)]
    iq_heads = [iq_ref[:, h * IDX_DIM:(h + 1) * IDX_DIM] for h in range(IDX_HEADS)]

    def score_block(j, mabs):
        ik_j = ikw_ref[j, :, :IDX_DIM]
        acc = jnp.zeros((BLK, BLK), F32)
        for h in range(IDX_HEADS):
            st = lax.dot_general(ik_j, iq_heads[h], NT_DIMS, preferred_element_type=F32)
            acc = acc + jnp.maximum(st, 0.0) * w_rows[h]
        causal = (row + j * BLK) <= qf
        s_ref[j] = jnp.where(causal, acc, -BIG)
        return jnp.maximum(mabs, jnp.max(jnp.where(causal, jnp.abs(acc), 0.0), axis=0, keepdims=True))

    mabs = lax.fori_loop(1, i + 1, score_block, jnp.zeros((1, BLK), F32))

    k_eff = float(n_keep - N_META)
    bound = mabs * 1.000001 + 1e-30

    def count_ge(thr):
        def body(j, c):
            return c + jnp.sum((s_ref[j] >= thr).astype(F32), axis=0, keepdims=True)
        return lax.fori_loop(1, i + 1, body, jnp.zeros((1, BLK), F32))

    def unsettled(state):
        it, _, _, cnt_lo = state
        return (it < BISECT_ITERS) & (jnp.max(cnt_lo) > k_eff)

    def bisect(state):
        it, lo, hi, cnt_lo = state
        mid = lo + (hi - lo) * 0.5
        cnt = count_ge(mid)
        ge = cnt >= k_eff
        return it + 1, jnp.where(ge, mid, lo), jnp.where(ge, hi, mid), jnp.where(ge, cnt, cnt_lo)

    _, lo, _, _ = lax.while_loop(unsettled, bisect, (jnp.int32(0), -bound, bound, count_ge(-bound)))

    def vmin_body(j, v):
        s = s_ref[j]
        return jnp.minimum(v, jnp.min(jnp.where(s >= lo, s, BIG), axis=0, keepdims=True))

    v = lax.fori_loop(1, i + 1, vmin_body, jnp.full((1, BLK), BIG, F32))
    v = jnp.where(v >= BIG, 0.0, v)

    def count_gt(j, c):
        return c + jnp.sum((s_ref[j] > v).astype(F32), axis=0, keepdims=True)

    ties_wanted = k_eff - lax.fori_loop(1, i + 1, count_gt, jnp.zeros((1, BLK), F32))

    acc_ref[...] = jnp.zeros_like(acc_ref)
    tri_incl = (col <= row).astype(BF16)
    heads = [slice(h * A_HEAD_DIM, (h + 1) * A_HEAD_DIM) for h in range(A_HEADS)]
    aq_heads = [aq_ref[:, hs] * (A_HEAD_DIM ** -0.5) for hs in heads]

    def attend(j, carry):
        ties_seen, ms, ls = carry
        s = s_ref[j]
        eq = s == v
        rank = ties_seen + jnp.dot(tri_incl, eq.astype(BF16), preferred_element_type=F32)
        sel = (s > v) | (eq & (rank <= ties_wanted))
        neg = jnp.where(sel, 0.0, -BIG).T
        d = jnp.minimum(i - j, 2)
        ms_new, ls_new = [], []
        for g in range(0, A_HEADS, DSA_HEAD_GROUP):
            group = range(g, g + DSA_HEAD_GROUP)
            zs = {h: lax.dot_general(aq_heads[h], ak_ref[j, :, heads[h]], NT_DIMS, preferred_element_type=F32)
                  + bias_ref[h, d] + neg for h in group}
            alphas, ps = {}, {}
            for h in group:
                m_new = jnp.maximum(ms[h], jnp.max(zs[h], axis=1, keepdims=True))
                alphas[h] = jnp.exp(ms[h] - m_new)
                p = jnp.exp(zs[h] - m_new)
                ls_new.append(alphas[h] * ls[h] + jnp.sum(p, axis=1, keepdims=True))
                ms_new.append(m_new)
                ps[h] = p.astype(BF16)
            for h in group:
                acc_ref[h] = alphas[h] * acc_ref[h] + jnp.dot(ps[h], av_ref[j, :, heads[h]],
                                                              preferred_element_type=F32)
        return rank[BLK - 1:BLK, :], tuple(ms_new), tuple(ls_new)

    init = (jnp.zeros((1, BLK), F32),
            tuple(jnp.full((BLK, 1), -BIG, F32) for _ in heads),
            tuple(jnp.zeros((BLK, 1), F32) for _ in heads))
    _, _, ls = lax.fori_loop(0, i + 1, attend, init)

    valid_q = (lax.broadcasted_iota(jnp.int32, (BLK, 1), 0) + i * BLK) >= LEAD
    out = jnp.concatenate([acc_ref[h] / ls[h] for h in range(A_HEADS)], axis=1)
    o_ref[...] = jnp.where(valid_q, out, 0.0).astype(o_ref.dtype)


def _dsa_attention(p0, ikw, bias, nbatch, nb, n_keep):
    rows = p0.shape[0]
    aw = A_HEADS * A_HEAD_DIM
    p0b = p0.reshape(nbatch, nb, BLK, p0.shape[1])
    ikwb = ikw.reshape(nbatch, nb, BLK, ikw.shape[1])
    return pl.pallas_call(
        functools.partial(_dsa_kernel, n_keep=n_keep),
        out_shape=jax.ShapeDtypeStruct((rows, aw), BF16),
        grid=(nbatch, nb),
        in_specs=[pl.BlockSpec((BLK, aw), lambda b, i: (b * nb + i, 0)),
                  pl.BlockSpec((BLK, aw), lambda b, i: (b * nb + i, 3)),
                  pl.BlockSpec((BLK, 128), lambda b, i: (b * nb + i, 0)),
                  pl.BlockSpec((None, nb, BLK, aw), lambda b, i: (b, 0, 0, 1)),
                  pl.BlockSpec((None, nb, BLK, aw), lambda b, i: (b, 0, 0, 2)),
                  pl.BlockSpec((None, nb, BLK, 128), lambda b, i: (b, 0, 0, 0)),
                  pl.BlockSpec((A_HEADS, 3, BLK, BLK), lambda b, i: (0, 0, 0, 0))],
        out_specs=pl.BlockSpec((BLK, aw), lambda b, i: (b * nb + i, 0)),
        scratch_shapes=[pltpu.VMEM((nb, BLK, BLK), F32),
                        pltpu.VMEM((A_HEADS, BLK, A_HEAD_DIM), F32)],
        compiler_params=_params("parallel", "arbitrary"),
        name="dsa_attention",
    )(p0, p0, ikw, p0b, p0b, ikwb, bias)


def _retention_kernel(bq_ref, bk_ref, bv_ref, bg_ref, cos_ref, sin_ref, dmat_ref, xi_ref, zeta_ref,
                      gain_ref, o_ref, r_ref, *, g_chunk):
    @pl.when(pl.program_id(1) == 0)
    def _():
        r_ref[...] = jnp.zeros_like(r_ref)

    cosf = cos_ref[...]
    sinf = sin_ref[...]

    def rot(x):
        return x * cosf + pltpu.roll(x, B_QK_DIM // 2, 1) * sinf

    for h in range(B_HEADS):
        ks = slice(h * B_QK_DIM, (h + 1) * B_QK_DIM)
        vs = slice(h * B_V_DIM, (h + 1) * B_V_DIM)
        q = rot(bq_ref[:, ks].astype(F32))
        k = rot(bk_ref[:, ks].astype(F32)) * (B_QK_DIM ** -0.5)
        qb = q.astype(BF16)
        kb = k.astype(BF16)
        v = bv_ref[:, vs]
        inner = lax.dot_general(qb, kb, NT_DIMS, preferred_element_type=F32) * dmat_ref[h]
        r_old = r_ref[h]
        o = (jnp.dot(inner.astype(BF16), v, preferred_element_type=F32)
             + jnp.dot(qb, r_old.astype(BF16), preferred_element_type=F32) * xi_ref[h])
        kz = (k * zeta_ref[h]).astype(BF16)
        r_ref[h] = r_old * g_chunk[h] + lax.dot_general(kz, v, TN_DIMS, preferred_element_type=F32)
        mu = jnp.mean(o, axis=-1, keepdims=True)
        oc = o - mu
        var = jnp.mean(oc * oc, axis=-1, keepdims=True)
        rn = oc * lax.rsqrt(var + EPS) * gain_ref[:, vs]
        gate = bg_ref[:, vs].astype(F32)
        o_ref[:, vs] = (rn * (gate / (1.0 + jnp.exp(-gate)))).astype(o_ref.dtype)


def _retention(p0, gn_gain, nbatch, nb):
    rows = p0.shape[0]
    qkw, vw = B_HEADS * B_QK_DIM, B_HEADS * B_V_DIM
    frame = nb * BLK
    half = B_QK_DIM // 2
    pos = (jnp.arange(frame) - LEAD).astype(F32)
    inv = 1.0 / (ROPE_BASE ** (jnp.arange(half, dtype=F32) / half))
    ang = pos[:, None] * inv[None, :]
    cosf = jnp.concatenate([jnp.cos(ang), jnp.cos(ang)], axis=-1)
    sinf = jnp.concatenate([-jnp.sin(ang), jnp.sin(ang)], axis=-1)
    lg = jnp.log(1.0 - 2.0 ** (-5.0 - jnp.arange(B_HEADS, dtype=F32)))
    n = jnp.arange(BLK, dtype=F32)
    diff = n[:, None] - n[None, :]
    dmat = jnp.where(diff[None] >= 0, jnp.exp(jnp.maximum(diff, 0.0)[None] * lg[:, None, None]), 0.0)
    xi = jnp.broadcast_to(jnp.exp((n[None, :] + 1.0) * lg[:, None])[..., None], (B_HEADS, BLK, B_V_DIM))
    zeta = jnp.broadcast_to(jnp.exp((BLK - 1.0 - n[None, :]) * lg[:, None])[..., None],
                            (B_HEADS, BLK, B_QK_DIM))
    g_chunk = tuple(float(math.exp(BLK * math.log(1.0 - 2.0 ** (-5.0 - h)))) for h in range(B_HEADS))
    const = lambda shape: pl.BlockSpec(shape, lambda b, i: (0,) * len(shape))
    return pl.pallas_call(
        functools.partial(_retention_kernel, g_chunk=g_chunk),
        out_shape=jax.ShapeDtypeStruct((rows, vw), BF16),
        grid=(nbatch, nb),
        in_specs=[pl.BlockSpec((BLK, qkw), lambda b, i: (b * nb + i, 4)),
                  pl.BlockSpec((BLK, qkw), lambda b, i: (b * nb + i, 5)),
                  pl.BlockSpec((BLK, vw), lambda b, i: (b * nb + i, 3)),
                  pl.BlockSpec((BLK, vw), lambda b, i: (b * nb + i, 4)),
                  pl.BlockSpec((BLK, B_QK_DIM), lambda b, i: (i, 0)),
                  pl.BlockSpec((BLK, B_QK_DIM), lambda b, i: (i, 0)),
                  const((B_HEADS, BLK, BLK)),
                  const((B_HEADS, BLK, B_V_DIM)),
                  const((B_HEADS, BLK, B_QK_DIM)),
                  const((1, vw))],
        out_specs=pl.BlockSpec((BLK, vw), lambda b, i: (b * nb + i, 0)),
        scratch_shapes=[pltpu.VMEM((B_HEADS, B_QK_DIM, B_V_DIM), F32)],
        compiler_params=_params("parallel", "arbitrary"),
        name="retention",
    )(p0, p0, p0, p0, cosf, sinf, dmat, xi, zeta, gn_gain.reshape(1, vw).astype(F32))


def _proj_residual_kernel(*refs, n_pairs):
    h_ref = refs[2 * n_pairs]
    o_ref = refs[2 * n_pairs + 1]
    acc = h_ref[...]
    for t in range(n_pairs):
        acc = acc + jnp.dot(refs[2 * t][...], refs[2 * t + 1][...], preferred_element_type=F32)
    o_ref[...] = acc


def _proj_residual(pairs, h, tm):
    rows, d = h.shape
    in_specs, args = [], []
    for a, w in pairs:
        in_specs += [pl.BlockSpec((tm, a.shape[1]), lambda i: (i, 0)),
                     pl.BlockSpec(w.shape, lambda i: (0, 0))]
        args += [a, w]
    in_specs.append(pl.BlockSpec((tm, d), lambda i: (i, 0)))
    return pl.pallas_call(
        functools.partial(_proj_residual_kernel, n_pairs=len(pairs)),
        out_shape=jax.ShapeDtypeStruct((rows, d), F32),
        grid=(rows // tm,),
        in_specs=in_specs,
        out_specs=pl.BlockSpec((tm, d), lambda i: (i, 0)),
        compiler_params=_params("parallel"),
        name="proj_residual",
    )(*args, h)


def _ffn_tail_kernel(u_ref, g_ref, halo_ref, cw_ref, cb_ref, wd_ref, h_ref, o_ref):
    g = g_ref[...].astype(F32)
    halo = halo_ref[...].astype(F32)
    tm = g.shape[0]
    rowi = lax.broadcasted_iota(jnp.int32, g.shape, 0)
    g1 = jnp.where(rowi >= 1, pltpu.roll(g, 1, 0), halo[15:16, :])
    g2 = jnp.where(rowi >= 2, pltpu.roll(g, 2, 0),
                   jnp.where(rowi == 1, halo[15:16, :], halo[14:15, :]))
    gc = g2 * cw_ref[0:1, :] + g1 * cw_ref[1:2, :] + g * cw_ref[2:3, :] + cb_ref[...]
    act = (gc / (1.0 + jnp.exp(-gc))) * u_ref[...].astype(F32)
    o_ref[...] = h_ref[...] + jnp.dot(act.astype(BF16), wd_ref[...], preferred_element_type=F32)


def _ffn_tail(ug, conv_w, conv_b, w_down, h, tm):
    rows, d = h.shape
    dff = w_down.shape[0]
    nblk = dff // 128
    assert dff % 128 == 0 and tm % 16 == 0
    return pl.pallas_call(
        _ffn_tail_kernel,
        out_shape=jax.ShapeDtypeStruct((rows, d), F32),
        grid=(rows // tm,),
        in_specs=[pl.BlockSpec((tm, dff), lambda i: (i, 0)),
                  pl.BlockSpec((tm, dff), lambda i: (i, 1)),
                  pl.BlockSpec((16, dff), lambda i: (jnp.maximum(i * (tm // 16) - 1, 0), 1)),
                  pl.BlockSpec((CONV_WIDTH, dff), lambda i: (0, 0)),
                  pl.BlockSpec((1, dff), lambda i: (0, 0)),
                  pl.BlockSpec((dff, d), lambda i: (0, 0)),
                  pl.BlockSpec((tm, d), lambda i: (i, 0))],
        out_specs=pl.BlockSpec((tm, d), lambda i: (i, 0)),
        compiler_params=_params("parallel"),
        name="ffn_tail",
    )(ug, ug, ug, conv_w.astype(F32), conv_b.reshape(1, dff).astype(F32), w_down, h)


SB_GROUP = 4


def _stick_breaking_kernel(q_ref, k_ref, v_ref, o_ref, acc_ref):
    i = pl.program_id(2)
    row = lax.broadcasted_iota(jnp.int32, (BLK, BLK), 0)
    col = lax.broadcasted_iota(jnp.int32, (BLK, BLK), 1)
    tri_after = (row > col).astype(BF16)
    scale = C_HEAD_DIM ** -0.5

    heads = [slice(h * C_HEAD_DIM, (h + 1) * C_HEAD_DIM) for h in range(SB_GROUP)]
    q_heads = [q_ref[:, hs] * scale for hs in heads]
    acc_ref[...] = jnp.zeros_like(acc_ref)

    def tile(j, carries, mask):
        zs = [lax.dot_general(q_heads[h], k_ref[j, :, hs], NT_DIMS, preferred_element_type=F32)
              for h, hs in enumerate(heads)]
        lms = []
        for z in zs:
            lm = -(jnp.maximum(z, 0.0) + jnp.log(1.0 + jnp.exp(-jnp.abs(z))))
            lms.append(lm if mask is None else jnp.where(mask, lm, 0.0))
        rests = []
        for lm in lms:
            lm_hi = lm.astype(BF16)
            lm_lo = (lm - lm_hi.astype(F32)).astype(BF16)
            rests.append(jnp.dot(lm_hi, tri_after, preferred_element_type=F32)
                         + jnp.dot(lm_lo, tri_after, preferred_element_type=F32))
        out = []
        for h, hs in enumerate(heads):
            a = jnp.exp(zs[h] + lms[h] + rests[h] + carries[h])
            if mask is not None:
                a = jnp.where(mask, a, 0.0)
            acc_ref[h] += jnp.dot(a.astype(BF16), v_ref[j, :, hs], preferred_element_type=F32)
            out.append(carries[h] + jnp.sum(lms[h], axis=1, keepdims=True))
        return tuple(out)

    qf = row + i * BLK
    kf = col + i * BLK
    carries = tile(i, tuple(jnp.zeros((BLK, 1), F32) for _ in heads), (kf < qf) & (kf >= LEAD))
    carries = lax.fori_loop(1, i, lambda t, c: tile(i - t, c, None), carries)

    @pl.when(i > 0)
    def _():
        tile(0, carries, col >= LEAD)

    o_ref[...] = jnp.concatenate([acc_ref[h] for h in range(SB_GROUP)], axis=1).astype(o_ref.dtype)


def _stick_breaking(p1, nbatch, nb):
    rows = p1.shape[0]
    cw = C_HEADS * C_HEAD_DIM
    gw = SB_GROUP * C_HEAD_DIM
    ngroups = C_HEADS // SB_GROUP
    p1b = p1.reshape(nbatch, nb, BLK, p1.shape[1])
    return pl.pallas_call(
        _stick_breaking_kernel,
        out_shape=jax.ShapeDtypeStruct((rows, cw), BF16),
        grid=(nbatch, ngroups, nb),
        in_specs=[pl.BlockSpec((BLK, gw), lambda b, g, i: (b * nb + i, g)),
                  pl.BlockSpec((None, nb, BLK, gw), lambda b, g, i: (b, 0, 0, ngroups + g)),
                  pl.BlockSpec((None, nb, BLK, gw), lambda b, g, i: (b, 0, 0, 2 * ngroups + g))],
        out_specs=pl.BlockSpec((BLK, gw), lambda b, g, i: (b * nb + i, g)),
        scratch_shapes=[pltpu.VMEM((SB_GROUP, BLK, C_HEAD_DIM), F32)],
        compiler_params=_params("parallel", "parallel", "arbitrary"),
        name="stick_breaking",
    )(p1, p1b, p1b)


def _final_norm_kernel(x_ref, g_ref, o_ref):
    x = x_ref[...]
    ms = jnp.mean(x * x, axis=-1, keepdims=True)
    o_ref[...] = x * lax.rsqrt(ms + EPS) * g_ref[...]


def _final_norm(h, gain, nbatch, nb):
    d = h.shape[1]
    return pl.pallas_call(
        _final_norm_kernel,
        out_shape=jax.ShapeDtypeStruct((nbatch, (nb - 1) * BLK, d), F32),
        grid=(nbatch, nb - 1),
        in_specs=[pl.BlockSpec((BLK, d), lambda b, i: (b * nb + i + 1, 0)),
                  pl.BlockSpec((1, d), lambda b, i: (0, 0))],
        out_specs=pl.BlockSpec((None, BLK, d), lambda b, i: (b, i, 0)),
        compiler_params=_params("parallel", "parallel"),
        name="final_norm",
    )(h, gain.reshape(1, d).astype(F32))


def _pick_tile(rows, pref):
    t = pref
    while rows % t:
        t //= 2
    return t


def kernel(x, meta_tokens, rel_bias, norm_mix, norm_ffn, norm_final, even_w_in, even_gn_gain, even_w_out, odd_w_in, odd_w_out, ffn_w_up, ffn_w_gate, ffn_conv_w, ffn_conv_b, ffn_w_down):
    nbatch, seq, d = x.shape
    assert seq % BLK == 0
    nb = seq // BLK + 1
    rows = nbatch * nb * BLK
    n_keep = min(TOPK_MAX, seq // 4)
    assert n_keep >= N_META
    depth = norm_mix.shape[0]
    tm = _pick_tile(rows, 1024)

    meta = jnp.broadcast_to(meta_tokens[None].astype(x.dtype), (nbatch, N_META, d))
    h = jnp.concatenate([jnp.zeros((nbatch, LEAD, d), x.dtype), meta, x], axis=1).reshape(rows, d)

    aw = A_HEADS * A_HEAD_DIM
    qkw, vw = B_HEADS * B_QK_DIM, B_HEADS * B_V_DIM
    cw = C_HEADS * C_HEAD_DIM
    bias = _bias_tiles(rel_bias)

    for l in range(depth):
        j = l // 2
        if l % 2 == 0:
            w = even_w_in[j]
            o_iq, o_ik, o_iw, o_bq = 3 * aw, 4 * aw, 4 * aw + IDX_DIM, 4 * aw + IDX_DIM + IDX_HEADS
            w_main = jnp.concatenate([w[:, :4 * aw], w[:, o_bq:]], axis=1).astype(BF16)
            w_idx = jnp.concatenate([w[:, o_ik:o_bq], jnp.zeros((d, 128 - IDX_DIM - IDX_HEADS), w.dtype)],
                                    axis=1).astype(BF16)
            p0 = _norm_matmul(h, norm_mix[l], w_main, tm, 512)
            ikw = _norm_matmul(h, norm_mix[l], w_idx, tm, 128)
            a_out = _dsa_attention(p0, ikw, bias, nbatch, nb, n_keep)
            r_out = _retention(p0, even_gn_gain[j], nbatch, nb)
            w_out = even_w_out[j].astype(BF16)
            h = _proj_residual([(a_out, w_out[:aw]), (r_out, w_out[aw:])], h, _pick_tile(rows, 512))
        else:
            p1 = _norm_matmul(h, norm_mix[l], odd_w_in[j].astype(BF16), tm, 512)
            s_out = _stick_breaking(p1, nbatch, nb)
            h = _proj_residual([(s_out, odd_w_out[j].astype(BF16))], h, _pick_tile(rows, 512))
        w_ug = jnp.concatenate([ffn_w_up[l], ffn_w_gate[l]], axis=1).astype(BF16)
        ug = _norm_matmul(h, norm_ffn[l], w_ug, tm, 512)
        h = _ffn_tail(ug, ffn_conv_w[l], ffn_conv_b[l], ffn_w_down[l].astype(BF16), h, _pick_tile(rows, 512))

    return _final_norm(h, norm_final, nbatch, nb)
```

```python
import functools
import math

import jax
import jax.numpy as jnp
import numpy as np
from jax import lax
from jax.experimental import pallas as pl
from jax.experimental.pallas import tpu as pltpu

N_META = 16
BLK = 256
LEAD = BLK - N_META
A_HEADS, A_HEAD_DIM = 8, 64
IDX_HEADS, IDX_DIM = 8, 64
TOPK_MAX = 256
N_BUCKETS, MAX_DISTANCE = 32, 128
B_HEADS, B_QK_DIM, B_V_DIM = 4, 128, 256
ROPE_BASE = 10000.0
C_HEADS, C_HEAD_DIM = 16, 64
CONV_WIDTH = 3
EPS = 1e-6
BIG = 1e30
LOG2E = 1.4426950408889634
BISECT_ITERS = 64
DSA_HEAD_GROUP = 8
VMEM_LIMIT_BYTES = 56 * 1024 * 1024

F32 = jnp.float32
BF16 = jnp.bfloat16
NT_DIMS = (((1,), (1,)), ((), ()))
TN_DIMS = (((0,), (0,)), ((), ()))


def _params(*sem):
    return pltpu.CompilerParams(dimension_semantics=sem, vmem_limit_bytes=VMEM_LIMIT_BYTES)


def _norm_matmul_kernel(x_ref, g_ref, w_ref, o_ref, xn_ref):
    @pl.when(pl.program_id(1) == 0)
    def _():
        x = x_ref[...]
        ms = jnp.mean(x * x, axis=-1, keepdims=True)
        xn_ref[...] = (x * lax.rsqrt(ms + EPS) * g_ref[...]).astype(BF16)

    o_ref[...] = jnp.dot(xn_ref[...], w_ref[...], preferred_element_type=F32).astype(o_ref.dtype)


def _norm_matmul(x, gain, w, tm, tn):
    rows, d = x.shape
    n = w.shape[1]
    return pl.pallas_call(
        _norm_matmul_kernel,
        out_shape=jax.ShapeDtypeStruct((rows, n), BF16),
        grid=(rows // tm, n // tn),
        in_specs=[pl.BlockSpec((tm, d), lambda i, j: (i, 0)),
                  pl.BlockSpec((1, d), lambda i, j: (0, 0)),
                  pl.BlockSpec((d, tn), lambda i, j: (0, j))],
        out_specs=pl.BlockSpec((tm, tn), lambda i, j: (i, j)),
        scratch_shapes=[pltpu.VMEM((tm, d), BF16)],
        compiler_params=_params("parallel", "arbitrary"),
        name="norm_matmul",
    )(x, gain.reshape(1, d), w)


def _bucket_tiles():
    q = np.arange(BLK)[:, None]
    k = np.arange(BLK)[None, :]
    tiles = []
    for t in range(2):
        n = np.maximum(q - k + t * BLK, 0)
        max_exact = N_BUCKETS // 2
        large = max_exact + (np.log(np.maximum(n, 1).astype(np.float32) / max_exact)
                             / math.log(MAX_DISTANCE / max_exact)
                             * (N_BUCKETS - max_exact)).astype(np.int32)
        large = np.minimum(large, N_BUCKETS - 1)
        tiles.append(np.where(n < max_exact, n, large).astype(np.int32))
    return np.stack(tiles)


def _bias_tiles_kernel(rb_ref, idx_ref, o_ref):
    h = pl.program_id(0)
    for t in range(2):
        idx = idx_ref[t]
        acc = jnp.zeros((BLK, BLK), F32)
        for b in range(N_BUCKETS):
            acc = jnp.where(idx == b, rb_ref[b, h], acc)
        o_ref[t] = acc
    o_ref[2] = jnp.full((BLK, BLK), rb_ref[N_BUCKETS - 1, h], F32)


def _bias_tiles(rel_bias):
    return pl.pallas_call(
        _bias_tiles_kernel,
        out_shape=jax.ShapeDtypeStruct((A_HEADS, 3, BLK, BLK), F32),
        grid=(A_HEADS,),
        in_specs=[pl.BlockSpec(memory_space=pltpu.SMEM),
                  pl.BlockSpec((2, BLK, BLK), lambda h: (0, 0, 0))],
        out_specs=pl.BlockSpec((None, 3, BLK, BLK), lambda h: (h, 0, 0, 0)),
        compiler_params=_params("parallel"),
        name="t5_bias_tiles",
    )(rel_bias.astype(F32), jnp.asarray(_bucket_tiles()))


def _dsa_kernel(aq_ref, iq_ref, ikwq_ref, ak_ref, av_ref, ikw_ref, bias_ref, o_ref,
                s_ref, acc_ref, *, n_keep):
    i = pl.program_id(1)
    row = lax.broadcasted_iota(jnp.int32, (BLK, BLK), 0)
    col = lax.broadcasted_iota(jnp.int32, (BLK, BLK), 1)
    qf = col + i * BLK

    s_ref[0] = jnp.where((row >= LEAD) & (row <= qf), BIG, -BIG)
    eye = (lax.broadcasted_iota(jnp.int32, (128, 128), 0)
           == lax.broadcasted_iota(jnp.int32, (128, 128), 1)).astype(BF16)
    ikwq_t = lax.dot_general(eye, ikwq_ref[...], NT_DIMS, preferred_element_type=F32)
    w_scale = (IDX_DIM ** -0.5) * (IDX_HEADS ** -0.5)
    w_rows = [ikwq_t[IDX_DIM + h:IDX_DIM + h + 1, :] * w_scale for h in range(IDX_HEADS)]
    iq_heads = [iq_ref[:, h * IDX_DIM:(h + 1) * IDX_DIM] for h in range(IDX_HEADS)]

    def score_block(j, mabs):
        ik_j = ikw_ref[j, :, :IDX_DIM]
        acc = jnp.zeros((BLK, BLK), F32)
        for h in range(IDX_HEADS):
            st = lax.dot_general(ik_j, iq_heads[h], NT_DIMS, preferred_element_type=F32)
            acc = acc + jnp.maximum(st, 0.0) * w_rows[h]
        causal = (row + j * BLK) <= qf
        s_ref[j] = jnp.where(causal, acc, -BIG)
        return jnp.maximum(mabs, jnp.max(jnp.where(causal, jnp.abs(acc), 0.0), axis=0, keepdims=True))

    mabs = lax.fori_loop(1, i + 1, score_block, jnp.zeros((1, BLK), F32))

    k_eff = float(n_keep - N_META)
    bound = mabs * 1.000001 + 1e-30

    def count_ge(thr):
        def body(j, c):
            return c + jnp.sum((s_ref[j] >= thr).astype(F32), axis=0, keepdims=True)
        return lax.fori_loop(1, i + 1, body, jnp.zeros((1, BLK), F32))

    def unsettled(state):
        it, _, _, cnt_lo, _ = state
        return (it < BISECT_ITERS) & (jnp.max(cnt_lo) > k_eff)

    def bisect(state):
        it, lo, hi, cnt_lo, cnt_hi = state
        secant = (cnt_lo - (k_eff - 0.5)) / jnp.maximum(cnt_lo - cnt_hi, 1.0)
        frac = jnp.where(it % 2 == 0, 0.5, jnp.clip(secant, 0.03125, 0.96875))
        mid = lo + (hi - lo) * frac
        cnt = count_ge(mid)
        ge = cnt >= k_eff
        return (it + 1, jnp.where(ge, mid, lo), jnp.where(ge, hi, mid),
                jnp.where(ge, cnt, cnt_lo), jnp.where(ge, cnt_hi, cnt))

    _, lo, _, _, _ = lax.while_loop(
        unsettled, bisect, (jnp.int32(0), -bound, bound, count_ge(-bound), jnp.zeros((1, BLK), F32)))

    def vmin_body(j, v):
        s = s_ref[j]
        return jnp.minimum(v, jnp.min(jnp.where(s >= lo, s, BIG), axis=0, keepdims=True))

    v = lax.fori_loop(1, i + 1, vmin_body, jnp.full((1, BLK), BIG, F32))
    v = jnp.where(v >= BIG, 0.0, v)

    def count_gt(j, c):
        return c + jnp.sum((s_ref[j] > v).astype(F32), axis=0, keepdims=True)

    ties_wanted = k_eff - lax.fori_loop(1, i + 1, count_gt, jnp.zeros((1, BLK), F32))

    acc_ref[...] = jnp.zeros_like(acc_ref)
    tri_incl = (col <= row).astype(BF16)
    heads = [slice(h * A_HEAD_DIM, (h + 1) * A_HEAD_DIM) for h in range(A_HEADS)]
    aq_heads = [aq_ref[:, hs] * (A_HEAD_DIM ** -0.5) for hs in heads]

    def attend(j, carry):
        ties_seen, ms, ls = carry
        s = s_ref[j]
        eq = s == v
        rank = ties_seen + jnp.dot(tri_incl, eq.astype(BF16), preferred_element_type=F32)
        sel = (s > v) | (eq & (rank <= ties_wanted))
        neg = jnp.where(sel, 0.0, -BIG).T
        d = jnp.minimum(i - j, 2)
        ms_new, ls_new = [], []
        for g in range(0, A_HEADS, DSA_HEAD_GROUP):
            group = range(g, g + DSA_HEAD_GROUP)
            zs = {h: lax.dot_general(aq_heads[h], ak_ref[j, :, heads[h]], NT_DIMS, preferred_element_type=F32)
                  + bias_ref[h, d] + neg for h in group}
            alphas, ps = {}, {}
            for h in group:
                m_new = jnp.maximum(ms[h], jnp.max(zs[h], axis=1, keepdims=True))
                alphas[h] = jnp.exp(ms[h] - m_new)
                p = jnp.exp(zs[h] - m_new)
                ls_new.append(alphas[h] * ls[h] + jnp.sum(p, axis=1, keepdims=True))
                ms_new.append(m_new)
                ps[h] = p.astype(BF16)
            for h in group:
                acc_ref[h] = alphas[h] * acc_ref[h] + jnp.dot(ps[h], av_ref[j, :, heads[h]],
                                                              preferred_element_type=F32)
        return rank[BLK - 1:BLK, :], tuple(ms_new), tuple(ls_new)

    init = (jnp.zeros((1, BLK), F32),
            tuple(jnp.full((BLK, 1), -BIG, F32) for _ in heads),
            tuple(jnp.zeros((BLK, 1), F32) for _ in heads))
    _, _, ls = lax.fori_loop(0, i + 1, attend, init)

    valid_q = (lax.broadcasted_iota(jnp.int32, (BLK, 1), 0) + i * BLK) >= LEAD
    out = jnp.concatenate([acc_ref[h] / ls[h] for h in range(A_HEADS)], axis=1)
    o_ref[...] = jnp.where(valid_q, out, 0.0).astype(o_ref.dtype)


def _dsa_attention(p0, ikw, bias, nbatch, nb, n_keep):
    rows = p0.shape[0]
    aw = A_HEADS * A_HEAD_DIM
    p0b = p0.reshape(nbatch, nb, BLK, p0.shape[1])
    ikwb = ikw.reshape(nbatch, nb, BLK, ikw.shape[1])
    return pl.pallas_call(
        functools.partial(_dsa_kernel, n_keep=n_keep),
        out_shape=jax.ShapeDtypeStruct((rows, aw), BF16),
        grid=(nbatch, nb),
        in_specs=[pl.BlockSpec((BLK, aw), lambda b, i: (b * nb + i, 0)),
                  pl.BlockSpec((BLK, aw), lambda b, i: (b * nb + i, 3)),
                  pl.BlockSpec((BLK, 128), lambda b, i: (b * nb + i, 0)),
                  pl.BlockSpec((None, nb, BLK, aw), lambda b, i: (b, 0, 0, 1)),
                  pl.BlockSpec((None, nb, BLK, aw), lambda b, i: (b, 0, 0, 2)),
                  pl.BlockSpec((None, nb, BLK, 128), lambda b, i: (b, 0, 0, 0)),
                  pl.BlockSpec((A_HEADS, 3, BLK, BLK), lambda b, i: (0, 0, 0, 0))],
        out_specs=pl.BlockSpec((BLK, aw), lambda b, i: (b * nb + i, 0)),
        scratch_shapes=[pltpu.VMEM((nb, BLK, BLK), F32),
                        pltpu.VMEM((A_HEADS, BLK, A_HEAD_DIM), F32)],
        compiler_params=_params("parallel", "arbitrary"),
        name="dsa_attention",
    )(p0, p0, ikw, p0b, p0b, ikwb, bias)


def _retention_kernel(bq_ref, bk_ref, bv_ref, bg_ref, cos_ref, sin_ref, dmat_ref, xi_ref, zeta_ref,
                      gain_ref, o_ref, r_ref, *, g_chunk):
    @pl.when(pl.program_id(1) == 0)
    def _():
        r_ref[...] = jnp.zeros_like(r_ref)

    cosf = cos_ref[...]
    sinf = sin_ref[...]

    def rot(x):
        return x * cosf + pltpu.roll(x, B_QK_DIM // 2, 1) * sinf

    for h in range(B_HEADS):
        ks = slice(h * B_QK_DIM, (h + 1) * B_QK_DIM)
        vs = slice(h * B_V_DIM, (h + 1) * B_V_DIM)
        q = rot(bq_ref[:, ks].astype(F32))
        k = rot(bk_ref[:, ks].astype(F32)) * (B_QK_DIM ** -0.5)
        qb = q.astype(BF16)
        kb = k.astype(BF16)
        v = bv_ref[:, vs]
        inner = lax.dot_general(qb, kb, NT_DIMS, preferred_element_type=F32) * dmat_ref[h]
        r_old = r_ref[h]
        o = (jnp.dot(inner.astype(BF16), v, preferred_element_type=F32)
             + jnp.dot(qb, r_old.astype(BF16), preferred_element_type=F32) * xi_ref[h])
        kz = (k * zeta_ref[h]).astype(BF16)
        r_ref[h] = r_old * g_chunk[h] + lax.dot_general(kz, v, TN_DIMS, preferred_element_type=F32)
        mu = jnp.mean(o, axis=-1, keepdims=True)
        oc = o - mu
        var = jnp.mean(oc * oc, axis=-1, keepdims=True)
        rn = oc * lax.rsqrt(var + EPS) * gain_ref[:, vs]
        gate = bg_ref[:, vs].astype(F32)
        o_ref[:, vs] = (rn * (gate / (1.0 + jnp.exp(-gate)))).astype(o_ref.dtype)


def _retention(p0, gn_gain, nbatch, nb):
    rows = p0.shape[0]
    qkw, vw = B_HEADS * B_QK_DIM, B_HEADS * B_V_DIM
    frame = nb * BLK
    half = B_QK_DIM // 2
    pos = (jnp.arange(frame) - LEAD).astype(F32)
    inv = 1.0 / (ROPE_BASE ** (jnp.arange(half, dtype=F32) / half))
    ang = pos[:, None] * inv[None, :]
    cosf = jnp.concatenate([jnp.cos(ang), jnp.cos(ang)], axis=-1)
    sinf = jnp.concatenate([-jnp.sin(ang), jnp.sin(ang)], axis=-1)
    lg = jnp.log(1.0 - 2.0 ** (-5.0 - jnp.arange(B_HEADS, dtype=F32)))
    n = jnp.arange(BLK, dtype=F32)
    diff = n[:, None] - n[None, :]
    dmat = jnp.where(diff[None] >= 0, jnp.exp(jnp.maximum(diff, 0.0)[None] * lg[:, None, None]), 0.0)
    xi = jnp.broadcast_to(jnp.exp((n[None, :] + 1.0) * lg[:, None])[..., None], (B_HEADS, BLK, B_V_DIM))
    zeta = jnp.broadcast_to(jnp.exp((BLK - 1.0 - n[None, :]) * lg[:, None])[..., None],
                            (B_HEADS, BLK, B_QK_DIM))
    g_chunk = tuple(float(math.exp(BLK * math.log(1.0 - 2.0 ** (-5.0 - h)))) for h in range(B_HEADS))
    const = lambda shape: pl.BlockSpec(shape, lambda b, i: (0,) * len(shape))
    return pl.pallas_call(
        functools.partial(_retention_kernel, g_chunk=g_chunk),
        out_shape=jax.ShapeDtypeStruct((rows, vw), BF16),
        grid=(nbatch, nb),
        in_specs=[pl.BlockSpec((BLK, qkw), lambda b, i: (b * nb + i, 4)),
                  pl.BlockSpec((BLK, qkw), lambda b, i: (b * nb + i, 5)),
                  pl.BlockSpec((BLK, vw), lambda b, i: (b * nb + i, 3)),
                  pl.BlockSpec((BLK, vw), lambda b, i: (b * nb + i, 4)),
                  pl.BlockSpec((BLK, B_QK_DIM), lambda b, i: (i, 0)),
                  pl.BlockSpec((BLK, B_QK_DIM), lambda b, i: (i, 0)),
                  const((B_HEADS, BLK, BLK)),
                  const((B_HEADS, BLK, B_V_DIM)),
                  const((B_HEADS, BLK, B_QK_DIM)),
                  const((1, vw))],
        out_specs=pl.BlockSpec((BLK, vw), lambda b, i: (b * nb + i, 0)),
        scratch_shapes=[pltpu.VMEM((B_HEADS, B_QK_DIM, B_V_DIM), F32)],
        compiler_params=_params("parallel", "arbitrary"),
        name="retention",
    )(p0, p0, p0, p0, cosf, sinf, dmat, xi, zeta, gn_gain.reshape(1, vw).astype(F32))


def _proj_residual_kernel(*refs, n_pairs):
    h_ref = refs[2 * n_pairs]
    o_ref = refs[2 * n_pairs + 1]
    acc = h_ref[...]
    for t in range(n_pairs):
        acc = acc + jnp.dot(refs[2 * t][...], refs[2 * t + 1][...], preferred_element_type=F32)
    o_ref[...] = acc


def _proj_residual(pairs, h, tm):
    rows, d = h.shape
    in_specs, args = [], []
    for a, w in pairs:
        in_specs += [pl.BlockSpec((tm, a.shape[1]), lambda i: (i, 0)),
                     pl.BlockSpec(w.shape, lambda i: (0, 0))]
        args += [a, w]
    in_specs.append(pl.BlockSpec((tm, d), lambda i: (i, 0)))
    return pl.pallas_call(
        functools.partial(_proj_residual_kernel, n_pairs=len(pairs)),
        out_shape=jax.ShapeDtypeStruct((rows, d), F32),
        grid=(rows // tm,),
        in_specs=in_specs,
        out_specs=pl.BlockSpec((tm, d), lambda i: (i, 0)),
        compiler_params=_params("parallel"),
        name="proj_residual",
    )(*args, h)


def _ffn_tail_kernel(u_ref, g_ref, halo_ref, cw_ref, cb_ref, wd_ref, h_ref, o_ref):
    g = g_ref[...].astype(F32)
    halo = halo_ref[...].astype(F32)
    tm = g.shape[0]
    rowi = lax.broadcasted_iota(jnp.int32, g.shape, 0)
    g1 = jnp.where(rowi >= 1, pltpu.roll(g, 1, 0), halo[15:16, :])
    g2 = jnp.where(rowi >= 2, pltpu.roll(g, 2, 0),
                   jnp.where(rowi == 1, halo[15:16, :], halo[14:15, :]))
    gc = g2 * cw_ref[0:1, :] + g1 * cw_ref[1:2, :] + g * cw_ref[2:3, :] + cb_ref[...]
    act = (gc / (1.0 + jnp.exp(-gc))) * u_ref[...].astype(F32)
    o_ref[...] = h_ref[...] + jnp.dot(act.astype(BF16), wd_ref[...], preferred_element_type=F32)


def _ffn_tail(ug, conv_w, conv_b, w_down, h, tm):
    rows, d = h.shape
    dff = w_down.shape[0]
    nblk = dff // 128
    assert dff % 128 == 0 and tm % 16 == 0
    return pl.pallas_call(
        _ffn_tail_kernel,
        out_shape=jax.ShapeDtypeStruct((rows, d), F32),
        grid=(rows // tm,),
        in_specs=[pl.BlockSpec((tm, dff), lambda i: (i, 0)),
                  pl.BlockSpec((tm, dff), lambda i: (i, 1)),
                  pl.BlockSpec((16, dff), lambda i: (jnp.maximum(i * (tm // 16) - 1, 0), 1)),
                  pl.BlockSpec((CONV_WIDTH, dff), lambda i: (0, 0)),
                  pl.BlockSpec((1, dff), lambda i: (0, 0)),
                  pl.BlockSpec((dff, d), lambda i: (0, 0)),
                  pl.BlockSpec((tm, d), lambda i: (i, 0))],
        out_specs=pl.BlockSpec((tm, d), lambda i: (i, 0)),
        compiler_params=_params("parallel"),
        name="ffn_tail",
    )(ug, ug, ug, conv_w.astype(F32), conv_b.reshape(1, dff).astype(F32), w_down, h)


SB_GROUP = 4


def _stick_breaking_kernel(q_ref, k_ref, v_ref, o_ref, acc_ref):
    i = pl.program_id(2)
    row = lax.broadcasted_iota(jnp.int32, (BLK, BLK), 0)
    col = lax.broadcasted_iota(jnp.int32, (BLK, BLK), 1)
    tri_after = (row > col).astype(BF16)
    scale = C_HEAD_DIM ** -0.5

    heads = [slice(h * C_HEAD_DIM, (h + 1) * C_HEAD_DIM) for h in range(SB_GROUP)]
    q_heads = [q_ref[:, hs] * scale for hs in heads]
    acc_ref[...] = jnp.zeros_like(acc_ref)

    def tile(j, carries, mask):
        zs = [lax.dot_general(q_heads[h], k_ref[j, :, hs], NT_DIMS, preferred_element_type=F32)
              for h, hs in enumerate(heads)]
        sps = []
        for z in zs:
            sp = jnp.maximum(z, 0.0) + jnp.log(1.0 + jnp.exp2(jnp.abs(z) * -LOG2E))
            sps.append(sp if mask is None else jnp.where(mask, sp, 0.0))
        rests = []
        for sp in sps:
            rests.append(jnp.dot(sp.astype(BF16), tri_after, preferred_element_type=F32))
        out = []
        for h, hs in enumerate(heads):
            a = jnp.exp2((zs[h] - (sps[h] + rests[h] + carries[h])) * LOG2E)
            if mask is not None:
                a = jnp.where(mask, a, 0.0)
            acc_ref[h] += jnp.dot(a.astype(BF16), v_ref[j, :, hs], preferred_element_type=F32)
            out.append(carries[h] + jnp.sum(sps[h], axis=1, keepdims=True))
        return tuple(out)

    qf = row + i * BLK
    kf = col + i * BLK
    carries = tile(i, tuple(jnp.zeros((BLK, 1), F32) for _ in heads), (kf < qf) & (kf >= LEAD))
    carries = lax.fori_loop(1, i, lambda t, c: tile(i - t, c, None), carries)

    @pl.when(i > 0)
    def _():
        tile(0, carries, col >= LEAD)

    o_ref[...] = jnp.concatenate([acc_ref[h] for h in range(SB_GROUP)], axis=1).astype(o_ref.dtype)


def _stick_breaking(p1, nbatch, nb):
    rows = p1.shape[0]
    cw = C_HEADS * C_HEAD_DIM
    gw = SB_GROUP * C_HEAD_DIM
    ngroups = C_HEADS // SB_GROUP
    p1b = p1.reshape(nbatch, nb, BLK, p1.shape[1])
    return pl.pallas_call(
        _stick_breaking_kernel,
        out_shape=jax.ShapeDtypeStruct((rows, cw), BF16),
        grid=(nbatch, ngroups, nb),
        in_specs=[pl.BlockSpec((BLK, gw), lambda b, g, i: (b * nb + i, g)),
                  pl.BlockSpec((None, nb, BLK, gw), lambda b, g, i: (b, 0, 0, ngroups + g)),
                  pl.BlockSpec((None, nb, BLK, gw), lambda b, g, i: (b, 0, 0, 2 * ngroups + g))],
        out_specs=pl.BlockSpec((BLK, gw), lambda b, g, i: (b * nb + i, g)),
        scratch_shapes=[pltpu.VMEM((SB_GROUP, BLK, C_HEAD_DIM), F32)],
        compiler_params=_params("parallel", "parallel", "arbitrary"),
        name="stick_breaking",
    )(p1, p1b, p1b)


def _final_norm_kernel(x_ref, g_ref, o_ref):
    x = x_ref[...]
    ms = jnp.mean(x * x, axis=-1, keepdims=True)
    o_ref[...] = x * lax.rsqrt(ms + EPS) * g_ref[...]


def _final_norm(h, gain, nbatch, nb):
    d = h.shape[1]
    return pl.pallas_call(
        _final_norm_kernel,
        out_shape=jax.ShapeDtypeStruct((nbatch, (nb - 1) * BLK, d), F32),
        grid=(nbatch, nb - 1),
        in_specs=[pl.BlockSpec((BLK, d), lambda b, i: (b * nb + i + 1, 0)),
                  pl.BlockSpec((1, d), lambda b, i: (0, 0))],
        out_specs=pl.BlockSpec((None, BLK, d), lambda b, i: (b, i, 0)),
        compiler_params=_params("parallel", "parallel"),
        name="final_norm",
    )(h, gain.reshape(1, d).astype(F32))


def _pick_tile(rows, pref):
    t = pref
    while rows % t:
        t //= 2
    return t


def kernel(x, meta_tokens, rel_bias, norm_mix, norm_ffn, norm_final, even_w_in, even_gn_gain, even_w_out, odd_w_in, odd_w_out, ffn_w_up, ffn_w_gate, ffn_conv_w, ffn_conv_b, ffn_w_down):
    nbatch, seq, d = x.shape
    assert seq % BLK == 0
    nb = seq // BLK + 1
    rows = nbatch * nb * BLK
    n_keep = min(TOPK_MAX, seq // 4)
    assert n_keep >= N_META
    depth = norm_mix.shape[0]
    tm = _pick_tile(rows, 1024)

    meta = jnp.broadcast_to(meta_tokens[None].astype(x.dtype), (nbatch, N_META, d))
    h = jnp.concatenate([jnp.zeros((nbatch, LEAD, d), x.dtype), meta, x], axis=1).reshape(rows, d)

    aw = A_HEADS * A_HEAD_DIM
    qkw, vw = B_HEADS * B_QK_DIM, B_HEADS * B_V_DIM
    cw = C_HEADS * C_HEAD_DIM
    bias = _bias_tiles(rel_bias)

    for l in range(depth):
        j = l // 2
        if l % 2 == 0:
            w = even_w_in[j]
            o_iq, o_ik, o_iw, o_bq = 3 * aw, 4 * aw, 4 * aw + IDX_DIM, 4 * aw + IDX_DIM + IDX_HEADS
            w_main = jnp.concatenate([w[:, :4 * aw], w[:, o_bq:]], axis=1).astype(BF16)
            w_idx = jnp.concatenate([w[:, o_ik:o_bq], jnp.zeros((d, 128 - IDX_DIM - IDX_HEADS), w.dtype)],
                                    axis=1).astype(BF16)
            p0 = _norm_matmul(h, norm_mix[l], w_main, tm, 512)
            ikw = _norm_matmul(h, norm_mix[l], w_idx, tm, 128)
            a_out = _dsa_attention(p0, ikw, bias, nbatch, nb, n_keep)
            r_out = _retention(p0, even_gn_gain[j], nbatch, nb)
            w_out = even_w_out[j].astype(BF16)
            h = _proj_residual([(a_out, w_out[:aw]), (r_out, w_out[aw:])], h, _pick_tile(rows, 512))
        else:
            p1 = _norm_matmul(h, norm_mix[l], odd_w_in[j].astype(BF16), tm, 512)
            s_out = _stick_breaking(p1, nbatch, nb)
            h = _proj_residual([(s_out, odd_w_out[j].astype(BF16))], h, _pick_tile(rows, 512))
        w_ug = jnp.concatenate([ffn_w_up[l], ffn_w_gate[l]], axis=1).astype(BF16)
        ug = _norm_matmul(h, norm_ffn[l], w_ug, tm, 512)
        h = _ffn_tail(ug, ffn_conv_w[l], ffn_conv_b[l], ffn_w_down[l].astype(BF16), h, _pick_tile(rows, 512))

    return _final_norm(h, norm_final, nbatch, nb)
```

```python
import functools
import math

import jax
import jax.numpy as jnp
import numpy as np
from jax import lax
from jax.experimental import pallas as pl
from jax.experimental.pallas import tpu as pltpu

N_META = 16
BLK = 256
LEAD = BLK - N_META
A_HEADS, A_HEAD_DIM = 8, 64
IDX_HEADS, IDX_DIM = 8, 64
TOPK_MAX = 256
N_BUCKETS, MAX_DISTANCE = 32, 128
B_HEADS, B_QK_DIM, B_V_DIM = 4, 128, 256
ROPE_BASE = 10000.0
C_HEADS, C_HEAD_DIM = 16, 64
CONV_WIDTH = 3
EPS = 1e-6
BIG = 1e30
LOG2E = 1.4426950408889634
BISECT_ITERS = 40
DSA_HEAD_GROUP = 8
VMEM_LIMIT_BYTES = 56 * 1024 * 1024

F32 = jnp.float32
BF16 = jnp.bfloat16
NT_DIMS = (((1,), (1,)), ((), ()))
TN_DIMS = (((0,), (0,)), ((), ()))


def _params(*sem):
    return pltpu.CompilerParams(dimension_semantics=sem, vmem_limit_bytes=VMEM_LIMIT_BYTES)


def _norm_matmul_kernel(x_ref, g_ref, w_ref, o_ref, xn_ref):
    @pl.when(pl.program_id(1) == 0)
    def _():
        x = x_ref[...]
        ms = jnp.mean(x * x, axis=-1, keepdims=True)
        xn_ref[...] = (x * lax.rsqrt(ms + EPS) * g_ref[...]).astype(BF16)

    o_ref[...] = jnp.dot(xn_ref[...], w_ref[...], preferred_element_type=F32).astype(o_ref.dtype)


def _norm_matmul(x, gain, w, tm, tn):
    rows, d = x.shape
    n = w.shape[1]
    return pl.pallas_call(
        _norm_matmul_kernel,
        out_shape=jax.ShapeDtypeStruct((rows, n), BF16),
        grid=(rows // tm, n // tn),
        in_specs=[pl.BlockSpec((tm, d), lambda i, j: (i, 0)),
                  pl.BlockSpec((1, d), lambda i, j: (0, 0)),
                  pl.BlockSpec((d, tn), lambda i, j: (0, j))],
        out_specs=pl.BlockSpec((tm, tn), lambda i, j: (i, j)),
        scratch_shapes=[pltpu.VMEM((tm, d), BF16)],
        compiler_params=_params("parallel", "arbitrary"),
        name="norm_matmul",
    )(x, gain.reshape(1, d), w)


def _bucket_tiles():
    q = np.arange(BLK)[:, None]
    k = np.arange(BLK)[None, :]
    tiles = []
    for t in range(2):
        n = np.maximum(q - k + t * BLK, 0)
        max_exact = N_BUCKETS // 2
        large = max_exact + (np.log(np.maximum(n, 1).astype(np.float32) / max_exact)
                             / math.log(MAX_DISTANCE / max_exact)
                             * (N_BUCKETS - max_exact)).astype(np.int32)
        large = np.minimum(large, N_BUCKETS - 1)
        tiles.append(np.where(n < max_exact, n, large).astype(np.int32))
    return np.stack(tiles)


def _bias_tiles_kernel(rb_ref, idx_ref, o_ref):
    h = pl.program_id(0)
    for t in range(2):
        idx = idx_ref[t]
        acc = jnp.zeros((BLK, BLK), F32)
        for b in range(N_BUCKETS):
            acc = jnp.where(idx == b, rb_ref[b, h], acc)
        o_ref[t] = acc
    o_ref[2] = jnp.full((BLK, BLK), rb_ref[N_BUCKETS - 1, h], F32)


def _bias_tiles(rel_bias):
    return pl.pallas_call(
        _bias_tiles_kernel,
        out_shape=jax.ShapeDtypeStruct((A_HEADS, 3, BLK, BLK), F32),
        grid=(A_HEADS,),
        in_specs=[pl.BlockSpec(memory_space=pltpu.SMEM),
                  pl.BlockSpec((2, BLK, BLK), lambda h: (0, 0, 0))],
        out_specs=pl.BlockSpec((None, 3, BLK, BLK), lambda h: (h, 0, 0, 0)),
        compiler_params=_params("parallel"),
        name="t5_bias_tiles",
    )(rel_bias.astype(F32), jnp.asarray(_bucket_tiles()))


def _dsa_kernel(aq_ref, iq_ref, ikwq_ref, ak_ref, av_ref, ikw_ref, bias_ref, o_ref,
                s_ref, acc_ref, av1_ref, *, n_keep):
    i = pl.program_id(1)
    row = lax.broadcasted_iota(jnp.int32, (BLK, BLK), 0)
    col = lax.broadcasted_iota(jnp.int32, (BLK, BLK), 1)
    qf = col + i * BLK
    heads = [slice(h * A_HEAD_DIM, (h + 1) * A_HEAD_DIM) for h in range(A_HEADS)]
    heads1 = [slice(h * 2 * A_HEAD_DIM, (h + 1) * 2 * A_HEAD_DIM) for h in range(A_HEADS)]

    @pl.when(i == 0)
    def _():
        def fill(j, _):
            ones = jnp.ones((BLK, A_HEAD_DIM), BF16)
            av1_ref[j] = jnp.concatenate([x for hs in heads for x in (av_ref[j, :, hs], ones)], axis=1)
            return 0
        lax.fori_loop(0, av_ref.shape[0], fill, 0)

    s_ref[0] = jnp.where((row >= LEAD) & (row <= qf), BIG, -BIG)
    eye = (lax.broadcasted_iota(jnp.int32, (128, 128), 0)
           == lax.broadcasted_iota(jnp.int32, (128, 128), 1)).astype(BF16)
    ikwq_t = lax.dot_general(eye, ikwq_ref[...], NT_DIMS, preferred_element_type=F32)
    w_scale = (IDX_DIM ** -0.5) * (IDX_HEADS ** -0.5)
    w_rows = [ikwq_t[IDX_DIM + h:IDX_DIM + h + 1, :] * w_scale for h in range(IDX_HEADS)]
    iq_heads = [iq_ref[:, h * IDX_DIM:(h + 1) * IDX_DIM] for h in range(IDX_HEADS)]

    def score_block(j, mabs):
        ik_j = ikw_ref[j, :, :IDX_DIM]
        acc = jnp.zeros((BLK, BLK), F32)
        for h in range(IDX_HEADS):
            st = lax.dot_general(ik_j, iq_heads[h], NT_DIMS, preferred_element_type=F32)
            acc = acc + jnp.maximum(st, 0.0) * w_rows[h]
        causal = (row + j * BLK) <= qf
        s_ref[j] = jnp.where(causal, acc, -BIG)
        return jnp.maximum(mabs, jnp.max(jnp.where(causal, jnp.abs(acc), 0.0), axis=0, keepdims=True))

    mabs = lax.fori_loop(1, i + 1, score_block, jnp.zeros((1, BLK), F32))

    k_eff = float(n_keep - N_META)
    bound = mabs * 1.000001 + 1e-30

    def count_ge(thr):
        def body(j, c):
            return c + jnp.sum((s_ref[j] >= thr).astype(F32), axis=0, keepdims=True)
        return lax.fori_loop(1, i + 1, body, jnp.zeros((1, BLK), F32))

    def unsettled(state):
        it, _, _, cnt_lo = state
        return (it < BISECT_ITERS) & (jnp.max(cnt_lo) > k_eff)

    def halve(lo, hi, cnt_lo):
        mid = lo + (hi - lo) * 0.5
        cnt = count_ge(mid)
        ge = cnt >= k_eff
        return jnp.where(ge, mid, lo), jnp.where(ge, hi, mid), jnp.where(ge, cnt, cnt_lo)

    def bisect(state):
        it, lo, hi, cnt_lo = state
        return (it + 2,) + halve(*halve(lo, hi, cnt_lo))

    _, lo, _, _ = lax.while_loop(unsettled, bisect, (jnp.int32(0), -bound, bound, count_ge(-bound)))

    def vmin_body(j, v):
        s = s_ref[j]
        return jnp.minimum(v, jnp.min(jnp.where(s >= lo, s, BIG), axis=0, keepdims=True))

    v = lax.fori_loop(1, i + 1, vmin_body, jnp.full((1, BLK), BIG, F32))
    v = jnp.where(v >= BIG, 0.0, v)

    def count_gt(j, c):
        return c + jnp.sum((s_ref[j] > v).astype(F32), axis=0, keepdims=True)

    ties_wanted = k_eff - lax.fori_loop(1, i + 1, count_gt, jnp.zeros((1, BLK), F32))

    acc_ref[...] = jnp.zeros_like(acc_ref)
    tri_incl = (col <= row).astype(BF16)
    aq_heads = [aq_ref[:, hs] * (A_HEAD_DIM ** -0.5) for hs in heads]

    def attend(j, carry):
        ties_seen, ms = carry
        s = s_ref[j]
        eq = s == v
        rank = ties_seen + jnp.dot(tri_incl, eq.astype(BF16), preferred_element_type=F32)
        sel = (s > v) | (eq & (rank <= ties_wanted))
        neg = jnp.where(sel, 0.0, -BIG).T
        d = jnp.minimum(i - j, 2)
        ms_new = []
        for g in range(0, A_HEADS, DSA_HEAD_GROUP):
            group = range(g, g + DSA_HEAD_GROUP)
            zs = {h: lax.dot_general(aq_heads[h], ak_ref[j, :, heads[h]], NT_DIMS, preferred_element_type=F32)
                  + bias_ref[h, d] + neg for h in group}
            alphas, ps = {}, {}
            for h in group:
                m_new = jnp.maximum(ms[h], jnp.max(zs[h], axis=1, keepdims=True))
                alphas[h] = jnp.exp(ms[h] - m_new)
                ps[h] = jnp.exp(zs[h] - m_new).astype(BF16)
                ms_new.append(m_new)
            for h in group:
                acc_ref[h] = alphas[h] * acc_ref[h] + jnp.dot(ps[h], av1_ref[j, :, heads1[h]],
                                                              preferred_element_type=F32)
        return rank[BLK - 1:BLK, :], tuple(ms_new)

    init = (jnp.zeros((1, BLK), F32), tuple(jnp.full((BLK, 1), -BIG, F32) for _ in heads))
    lax.fori_loop(0, i + 1, attend, init)

    valid_q = (lax.broadcasted_iota(jnp.int32, (BLK, 1), 0) + i * BLK) >= LEAD
    outs = []
    for h in range(A_HEADS):
        acc = acc_ref[h]
        outs.append((acc / pltpu.roll(acc, A_HEAD_DIM, 1))[:, :A_HEAD_DIM])
    o_ref[...] = jnp.where(valid_q, jnp.concatenate(outs, axis=1), 0.0).astype(o_ref.dtype)


def _dsa_attention(p0, ikw, bias, nbatch, nb, n_keep):
    rows = p0.shape[0]
    aw = A_HEADS * A_HEAD_DIM
    p0b = p0.reshape(nbatch, nb, BLK, p0.shape[1])
    ikwb = ikw.reshape(nbatch, nb, BLK, ikw.shape[1])
    return pl.pallas_call(
        functools.partial(_dsa_kernel, n_keep=n_keep),
        out_shape=jax.ShapeDtypeStruct((rows, aw), BF16),
        grid=(nbatch, nb),
        in_specs=[pl.BlockSpec((BLK, aw), lambda b, i: (b * nb + i, 0)),
                  pl.BlockSpec((BLK, aw), lambda b, i: (b * nb + i, 3)),
                  pl.BlockSpec((BLK, 128), lambda b, i: (b * nb + i, 0)),
                  pl.BlockSpec((None, nb, BLK, aw), lambda b, i: (b, 0, 0, 1)),
                  pl.BlockSpec((None, nb, BLK, aw), lambda b, i: (b, 0, 0, 2)),
                  pl.BlockSpec((None, nb, BLK, 128), lambda b, i: (b, 0, 0, 0)),
                  pl.BlockSpec((A_HEADS, 3, BLK, BLK), lambda b, i: (0, 0, 0, 0))],
        out_specs=pl.BlockSpec((BLK, aw), lambda b, i: (b * nb + i, 0)),
        scratch_shapes=[pltpu.VMEM((nb, BLK, BLK), F32),
                        pltpu.VMEM((A_HEADS, BLK, 2 * A_HEAD_DIM), F32),
                        pltpu.VMEM((nb, BLK, 2 * aw), BF16)],
        compiler_params=_params("parallel", "arbitrary"),
        name="dsa_attention",
    )(p0, p0, ikw, p0b, p0b, ikwb, bias)


def _retention_kernel(bq_ref, bk_ref, bv_ref, bg_ref, cos_ref, sin_ref, dmat_ref, xi_ref, zeta_ref,
                      gain_ref, o_ref, r_ref, *, g_chunk):
    @pl.when(pl.program_id(1) == 0)
    def _():
        r_ref[...] = jnp.zeros_like(r_ref)

    cosf = cos_ref[...]
    sinf = sin_ref[...]

    def rot(x):
        return x * cosf + pltpu.roll(x, B_QK_DIM // 2, 1) * sinf

    for h in range(B_HEADS):
        ks = slice(h * B_QK_DIM, (h + 1) * B_QK_DIM)
        vs = slice(h * B_V_DIM, (h + 1) * B_V_DIM)
        q = rot(bq_ref[:, ks].astype(F32))
        k = rot(bk_ref[:, ks].astype(F32)) * (B_QK_DIM ** -0.5)
        qb = q.astype(BF16)
        kb = k.astype(BF16)
        v = bv_ref[:, vs]
        inner = lax.dot_general(qb, kb, NT_DIMS, preferred_element_type=F32) * dmat_ref[h]
        r_old = r_ref[h]
        o = (jnp.dot(inner.astype(BF16), v, preferred_element_type=F32)
             + jnp.dot(qb, r_old.astype(BF16), preferred_element_type=F32) * xi_ref[h])
        kz = (k * zeta_ref[h]).astype(BF16)
        r_ref[h] = r_old * g_chunk[h] + lax.dot_general(kz, v, TN_DIMS, preferred_element_type=F32)
        mu = jnp.mean(o, axis=-1, keepdims=True)
        oc = o - mu
        var = jnp.mean(oc * oc, axis=-1, keepdims=True)
        rn = oc * lax.rsqrt(var + EPS) * gain_ref[:, vs]
        gate = bg_ref[:, vs].astype(F32)
        o_ref[:, vs] = (rn * (gate / (1.0 + jnp.exp(-gate)))).astype(o_ref.dtype)


def _retention(p0, gn_gain, nbatch, nb):
    rows = p0.shape[0]
    qkw, vw = B_HEADS * B_QK_DIM, B_HEADS * B_V_DIM
    frame = nb * BLK
    half = B_QK_DIM // 2
    pos = (jnp.arange(frame) - LEAD).astype(F32)
    inv = 1.0 / (ROPE_BASE ** (jnp.arange(half, dtype=F32) / half))
    ang = pos[:, None] * inv[None, :]
    cosf = jnp.concatenate([jnp.cos(ang), jnp.cos(ang)], axis=-1)
    sinf = jnp.concatenate([-jnp.sin(ang), jnp.sin(ang)], axis=-1)
    lg = jnp.log(1.0 - 2.0 ** (-5.0 - jnp.arange(B_HEADS, dtype=F32)))
    n = jnp.arange(BLK, dtype=F32)
    diff = n[:, None] - n[None, :]
    dmat = jnp.where(diff[None] >= 0, jnp.exp(jnp.maximum(diff, 0.0)[None] * lg[:, None, None]), 0.0)
    xi = jnp.broadcast_to(jnp.exp((n[None, :] + 1.0) * lg[:, None])[..., None], (B_HEADS, BLK, B_V_DIM))
    zeta = jnp.broadcast_to(jnp.exp((BLK - 1.0 - n[None, :]) * lg[:, None])[..., None],
                            (B_HEADS, BLK, B_QK_DIM))
    g_chunk = tuple(float(math.exp(BLK * math.log(1.0 - 2.0 ** (-5.0 - h)))) for h in range(B_HEADS))
    const = lambda shape: pl.BlockSpec(shape, lambda b, i: (0,) * len(shape))
    return pl.pallas_call(
        functools.partial(_retention_kernel, g_chunk=g_chunk),
        out_shape=jax.ShapeDtypeStruct((rows, vw), BF16),
        grid=(nbatch, nb),
        in_specs=[pl.BlockSpec((BLK, qkw), lambda b, i: (b * nb + i, 4)),
                  pl.BlockSpec((BLK, qkw), lambda b, i: (b * nb + i, 5)),
                  pl.BlockSpec((BLK, vw), lambda b, i: (b * nb + i, 3)),
                  pl.BlockSpec((BLK, vw), lambda b, i: (b * nb + i, 4)),
                  pl.BlockSpec((BLK, B_QK_DIM), lambda b, i: (i, 0)),
                  pl.BlockSpec((BLK, B_QK_DIM), lambda b, i: (i, 0)),
                  const((B_HEADS, BLK, BLK)),
                  const((B_HEADS, BLK, B_V_DIM)),
                  const((B_HEADS, BLK, B_QK_DIM)),
                  const((1, vw))],
        out_specs=pl.BlockSpec((BLK, vw), lambda b, i: (b * nb + i, 0)),
        scratch_shapes=[pltpu.VMEM((B_HEADS, B_QK_DIM, B_V_DIM), F32)],
        compiler_params=_params("parallel", "arbitrary"),
        name="retention",
    )(p0, p0, p0, p0, cosf, sinf, dmat, xi, zeta, gn_gain.reshape(1, vw).astype(F32))


def _proj_residual_kernel(*refs, n_pairs):
    h_ref = refs[2 * n_pairs]
    o_ref = refs[2 * n_pairs + 1]
    acc = h_ref[...]
    for t in range(n_pairs):
        acc = acc + jnp.dot(refs[2 * t][...], refs[2 * t + 1][...], preferred_element_type=F32)
    o_ref[...] = acc


def _proj_residual(pairs, h, tm):
    rows, d = h.shape
    in_specs, args = [], []
    for a, w in pairs:
        in_specs += [pl.BlockSpec((tm, a.shape[1]), lambda i: (i, 0)),
                     pl.BlockSpec(w.shape, lambda i: (0, 0))]
        args += [a, w]
    in_specs.append(pl.BlockSpec((tm, d), lambda i: (i, 0)))
    return pl.pallas_call(
        functools.partial(_proj_residual_kernel, n_pairs=len(pairs)),
        out_shape=jax.ShapeDtypeStruct((rows, d), F32),
        grid=(rows // tm,),
        in_specs=in_specs,
        out_specs=pl.BlockSpec((tm, d), lambda i: (i, 0)),
        compiler_params=_params("parallel"),
        name="proj_residual",
    )(*args, h)


def _ffn_tail_kernel(u_ref, g_ref, halo_ref, cw_ref, cb_ref, wd_ref, h_ref, *rest, final):
    g = g_ref[...].astype(F32)
    halo = halo_ref[...].astype(F32)
    rowi = lax.broadcasted_iota(jnp.int32, g.shape, 0)
    g1 = jnp.where(rowi >= 1, pltpu.roll(g, 1, 0), halo[15:16, :])
    g2 = jnp.where(rowi >= 2, pltpu.roll(g, 2, 0),
                   jnp.where(rowi == 1, halo[15:16, :], halo[14:15, :]))
    gc = g2 * cw_ref[0:1, :] + g1 * cw_ref[1:2, :] + g * cw_ref[2:3, :] + cb_ref[...]
    act = (gc / (1.0 + jnp.exp(-gc))) * u_ref[...].astype(F32)
    x = h_ref[...] + jnp.dot(act.astype(BF16), wd_ref[...], preferred_element_type=F32)
    if final:
        ng_ref, o_ref = rest
        ms = jnp.mean(x * x, axis=-1, keepdims=True)
        o_ref[...] = x * lax.rsqrt(ms + EPS) * ng_ref[...]
    else:
        rest[0][...] = x


def _ffn_tail(ug, conv_w, conv_b, w_down, h, tm, final=None):
    rows, d = h.shape
    dff = w_down.shape[0]
    assert dff % 128 == 0 and tm % 16 == 0
    in_specs = [pl.BlockSpec((tm, dff), lambda i: (i, 0)),
                pl.BlockSpec((tm, dff), lambda i: (i, 1)),
                pl.BlockSpec((16, dff), lambda i: (jnp.maximum(i * (tm // 16) - 1, 0), 1)),
                pl.BlockSpec((CONV_WIDTH, dff), lambda i: (0, 0)),
                pl.BlockSpec((1, dff), lambda i: (0, 0)),
                pl.BlockSpec((dff, d), lambda i: (0, 0)),
                pl.BlockSpec((tm, d), lambda i: (i, 0))]
    args = [ug, ug, ug, conv_w.astype(F32), conv_b.reshape(1, dff).astype(F32), w_down, h]
    if final is None:
        out_shape = jax.ShapeDtypeStruct((rows, d), F32)
        out_spec = pl.BlockSpec((tm, d), lambda i: (i, 0))
        sem = "parallel"
    else:
        gain, nbatch, nb = final
        assert tm == BLK
        in_specs.append(pl.BlockSpec((1, d), lambda i: (0, 0)))
        args.append(gain.reshape(1, d).astype(F32))
        out_shape = jax.ShapeDtypeStruct((nbatch, (nb - 1) * BLK, d), F32)
        out_spec = pl.BlockSpec((None, BLK, d), lambda i: (i // nb, jnp.maximum(i % nb - 1, 0), 0))
        sem = "arbitrary"
    return pl.pallas_call(
        functools.partial(_ffn_tail_kernel, final=final is not None),
        out_shape=out_shape,
        grid=(rows // tm,),
        in_specs=in_specs,
        out_specs=out_spec,
        compiler_params=_params(sem),
        name="ffn_tail",
    )(*args)


SB_GROUP = 4


def _stick_breaking_kernel(q_ref, k_ref, v_ref, o_ref, acc_ref):
    i = pl.program_id(2)
    row = lax.broadcasted_iota(jnp.int32, (BLK, BLK), 0)
    col = lax.broadcasted_iota(jnp.int32, (BLK, BLK), 1)
    tri_after = (row > col).astype(BF16)
    scale = C_HEAD_DIM ** -0.5

    heads = [slice(h * C_HEAD_DIM, (h + 1) * C_HEAD_DIM) for h in range(SB_GROUP)]
    q_heads = [q_ref[:, hs] * scale for hs in heads]
    acc_ref[...] = jnp.zeros_like(acc_ref)

    def tiles(blocks, carries):
        keys = [(b, h) for b in range(len(blocks)) for h in range(SB_GROUP)]
        zs = {(b, h): lax.dot_general(q_heads[h], k_ref[blocks[b][0], :, heads[h]], NT_DIMS,
                                      preferred_element_type=F32) for b, h in keys}
        log_beta, rests, sums = {}, {}, {}
        for b, h in keys:
            z, mask = zs[b, h], blocks[b][1]
            sp = jnp.maximum(z, 0.0) + jnp.log(1.0 + jnp.exp2(jnp.abs(z) * -LOG2E))
            if mask is not None:
                sp = jnp.where(mask, sp, 0.0)
            log_beta[b, h] = z - sp
            sums[b, h] = jnp.sum(sp, axis=1, keepdims=True)
            rests[b, h] = jnp.dot(sp.astype(BF16), tri_after, preferred_element_type=F32)
        run = list(carries)
        for b, h in keys:
            j, mask = blocks[b]
            a = jnp.exp2((log_beta[b, h] - (rests[b, h] + run[h])) * LOG2E)
            if mask is not None:
                a = jnp.where(mask, a, 0.0)
            acc_ref[h] += jnp.dot(a.astype(BF16), v_ref[j, :, heads[h]], preferred_element_type=F32)
            run[h] = run[h] + sums[b, h]
        return tuple(run)

    qf = row + i * BLK
    kf = col + i * BLK
    carries = tiles([(i, (kf < qf) & (kf >= LEAD))], tuple(jnp.zeros((BLK, 1), F32) for _ in heads))
    n_full = jnp.maximum(i - 1, 0)
    odd = n_full & 1
    carries = lax.cond(odd == 1, lambda c: tiles([(i - 1, None)], c), lambda c: c, carries)
    top = i - 1 - odd
    carries = lax.fori_loop(0, jnp.right_shift(n_full, 1),
                            lambda t, c: tiles([(top - 2 * t, None), (top - 2 * t - 1, None)], c), carries)

    @pl.when(i > 0)
    def _():
        tiles([(0, col >= LEAD)], carries)

    o_ref[...] = jnp.concatenate([acc_ref[h] for h in range(SB_GROUP)], axis=1).astype(o_ref.dtype)


def _stick_breaking(p1, nbatch, nb):
    rows = p1.shape[0]
    cw = C_HEADS * C_HEAD_DIM
    gw = SB_GROUP * C_HEAD_DIM
    ngroups = C_HEADS // SB_GROUP
    p1b = p1.reshape(nbatch, nb, BLK, p1.shape[1])
    return pl.pallas_call(
        _stick_breaking_kernel,
        out_shape=jax.ShapeDtypeStruct((rows, cw), BF16),
        grid=(nbatch, ngroups, nb),
        in_specs=[pl.BlockSpec((BLK, gw), lambda b, g, i: (b * nb + i, g)),
                  pl.BlockSpec((None, nb, BLK, gw), lambda b, g, i: (b, 0, 0, ngroups + g)),
                  pl.BlockSpec((None, nb, BLK, gw), lambda b, g, i: (b, 0, 0, 2 * ngroups + g))],
        out_specs=pl.BlockSpec((BLK, gw), lambda b, g, i: (b * nb + i, g)),
        scratch_shapes=[pltpu.VMEM((SB_GROUP, BLK, C_HEAD_DIM), F32)],
        compiler_params=_params("parallel", "parallel", "arbitrary"),
        name="stick_breaking",
    )(p1, p1b, p1b)


def _pick_tile(rows, pref):
    t = pref
    while rows % t:
        t //= 2
    return t


def _pick_cols(n, cap=2816):
    return max(t for t in range(128, min(n, cap) + 1, 128) if n % t == 0)


def kernel(x, meta_tokens, rel_bias, norm_mix, norm_ffn, norm_final, even_w_in, even_gn_gain, even_w_out, odd_w_in, odd_w_out, ffn_w_up, ffn_w_gate, ffn_conv_w, ffn_conv_b, ffn_w_down):
    nbatch, seq, d = x.shape
    assert seq % BLK == 0
    nb = seq // BLK + 1
    rows = nbatch * nb * BLK
    n_keep = min(TOPK_MAX, seq // 4)
    assert n_keep >= N_META
    depth = norm_mix.shape[0]
    tm = _pick_tile(rows, 1024)

    meta = jnp.broadcast_to(meta_tokens[None].astype(x.dtype), (nbatch, N_META, d))
    h = jnp.concatenate([jnp.zeros((nbatch, LEAD, d), x.dtype), meta, x], axis=1).reshape(rows, d)

    aw = A_HEADS * A_HEAD_DIM
    qkw, vw = B_HEADS * B_QK_DIM, B_HEADS * B_V_DIM
    cw = C_HEADS * C_HEAD_DIM
    bias = _bias_tiles(rel_bias)

    for l in range(depth):
        j = l // 2
        if l % 2 == 0:
            w = even_w_in[j]
            o_iq, o_ik, o_iw, o_bq = 3 * aw, 4 * aw, 4 * aw + IDX_DIM, 4 * aw + IDX_DIM + IDX_HEADS
            w_main = jnp.concatenate([w[:, :4 * aw], w[:, o_bq:]], axis=1).astype(BF16)
            w_idx = jnp.concatenate([w[:, o_ik:o_bq], jnp.zeros((d, 128 - IDX_DIM - IDX_HEADS), w.dtype)],
                                    axis=1).astype(BF16)
            p0 = _norm_matmul(h, norm_mix[l], w_main, tm, _pick_cols(w_main.shape[1]))
            ikw = _norm_matmul(h, norm_mix[l], w_idx, tm, 128)
            a_out = _dsa_attention(p0, ikw, bias, nbatch, nb, n_keep)
            r_out = _retention(p0, even_gn_gain[j], nbatch, nb)
            w_out = even_w_out[j].astype(BF16)
            h = _proj_residual([(a_out, w_out[:aw]), (r_out, w_out[aw:])], h, _pick_tile(rows, 512))
        else:
            p1 = _norm_matmul(h, norm_mix[l], odd_w_in[j].astype(BF16), tm, _pick_cols(odd_w_in.shape[2]))
            s_out = _stick_breaking(p1, nbatch, nb)
            h = _proj_residual([(s_out, odd_w_out[j].astype(BF16))], h, _pick_tile(rows, 512))
        w_ug = jnp.concatenate([ffn_w_up[l], ffn_w_gate[l]], axis=1).astype(BF16)
        ug = _norm_matmul(h, norm_ffn[l], w_ug, tm, _pick_cols(w_ug.shape[1]))
        w_down = ffn_w_down[l].astype(BF16)
        if l + 1 < depth:
            h = _ffn_tail(ug, ffn_conv_w[l], ffn_conv_b[l], w_down, h, _pick_tile(rows, 512))
        else:
            h = _ffn_tail(ug, ffn_conv_w[l], ffn_conv_b[l], w_down, h, BLK, final=(norm_final, nbatch, nb))
    return h
```

```python
import functools
import math

import jax
import jax.numpy as jnp
import numpy as np
from jax import lax
from jax.experimental import pallas as pl
from jax.experimental.pallas import tpu as pltpu

N_META = 16
BLK = 256
LEAD = BLK - N_META
A_HEADS, A_HEAD_DIM = 8, 64
IDX_HEADS, IDX_DIM = 8, 64
TOPK_MAX = 256
N_BUCKETS, MAX_DISTANCE = 32, 128
B_HEADS, B_QK_DIM, B_V_DIM = 4, 128, 256
ROPE_BASE = 10000.0
C_HEADS, C_HEAD_DIM = 16, 64
CONV_WIDTH = 3
EPS = 1e-6
BIG = 1e30
LOG2E = 1.4426950408889634
BISECT_ITERS = 40
DSA_HEAD_GROUP = 8
VMEM_LIMIT_BYTES = 56 * 1024 * 1024

F32 = jnp.float32
BF16 = jnp.bfloat16
NT_DIMS = (((1,), (1,)), ((), ()))
TN_DIMS = (((0,), (0,)), ((), ()))


def _params(*sem):
    return pltpu.CompilerParams(dimension_semantics=sem, vmem_limit_bytes=VMEM_LIMIT_BYTES)


def _norm_matmul_kernel(x_ref, g_ref, w_ref, o_ref, xn_ref):
    @pl.when(pl.program_id(1) == 0)
    def _():
        x = x_ref[...]
        ms = jnp.mean(x * x, axis=-1, keepdims=True)
        xn_ref[...] = (x * lax.rsqrt(ms + EPS) * g_ref[...]).astype(BF16)

    o_ref[...] = jnp.dot(xn_ref[...], w_ref[...], preferred_element_type=F32).astype(o_ref.dtype)


def _norm_matmul(x, gain, w, tm, tn):
    rows, d = x.shape
    n = w.shape[1]
    return pl.pallas_call(
        _norm_matmul_kernel,
        out_shape=jax.ShapeDtypeStruct((rows, n), BF16),
        grid=(rows // tm, n // tn),
        in_specs=[pl.BlockSpec((tm, d), lambda i, j: (i, 0)),
                  pl.BlockSpec((1, d), lambda i, j: (0, 0)),
                  pl.BlockSpec((d, tn), lambda i, j: (0, j))],
        out_specs=pl.BlockSpec((tm, tn), lambda i, j: (i, j)),
        scratch_shapes=[pltpu.VMEM((tm, d), BF16)],
        compiler_params=_params("parallel", "arbitrary"),
        name="norm_matmul",
    )(x, gain.reshape(1, d), w)


def _bucket_tiles():
    q = np.arange(BLK)[:, None]
    k = np.arange(BLK)[None, :]
    tiles = []
    for t in range(2):
        n = np.maximum(q - k + t * BLK, 0)
        max_exact = N_BUCKETS // 2
        large = max_exact + (np.log(np.maximum(n, 1).astype(np.float32) / max_exact)
                             / math.log(MAX_DISTANCE / max_exact)
                             * (N_BUCKETS - max_exact)).astype(np.int32)
        large = np.minimum(large, N_BUCKETS - 1)
        tiles.append(np.where(n < max_exact, n, large).astype(np.int32))
    return np.stack(tiles)


def _bias_tiles_kernel(rb_ref, idx_ref, o_ref):
    h = pl.program_id(0)
    far = rb_ref[N_BUCKETS - 1, h]
    for t in range(2):
        idx = idx_ref[t]
        acc = jnp.zeros((BLK, BLK), F32)
        for b in range(N_BUCKETS):
            acc = jnp.where(idx == b, rb_ref[b, h] - far, acc)
        o_ref[t] = acc


def _bias_tiles(rel_bias):
    assert BLK + 1 >= MAX_DISTANCE
    return pl.pallas_call(
        _bias_tiles_kernel,
        out_shape=jax.ShapeDtypeStruct((A_HEADS, 2, BLK, BLK), F32),
        grid=(A_HEADS,),
        in_specs=[pl.BlockSpec(memory_space=pltpu.SMEM),
                  pl.BlockSpec((2, BLK, BLK), lambda h: (0, 0, 0))],
        out_specs=pl.BlockSpec((None, 2, BLK, BLK), lambda h: (h, 0, 0, 0)),
        compiler_params=_params("parallel"),
        name="t5_bias_tiles",
    )(rel_bias.astype(F32), jnp.asarray(_bucket_tiles()))


def _dsa_kernel(aq_ref, iq_ref, ikwq_ref, ak_ref, av_ref, ikw_ref, bias_ref, o_ref,
                s_ref, acc_ref, av1_ref, *, n_keep):
    i = pl.program_id(1)
    row = lax.broadcasted_iota(jnp.int32, (BLK, BLK), 0)
    col = lax.broadcasted_iota(jnp.int32, (BLK, BLK), 1)
    qf = col + i * BLK
    heads = [slice(h * A_HEAD_DIM, (h + 1) * A_HEAD_DIM) for h in range(A_HEADS)]
    heads1 = [slice(h * 2 * A_HEAD_DIM, (h + 1) * 2 * A_HEAD_DIM) for h in range(A_HEADS)]

    @pl.when(i == 0)
    def _():
        def fill(j, _):
            ones = jnp.ones((BLK, A_HEAD_DIM), BF16)
            av1_ref[j] = jnp.concatenate([x for hs in heads for x in (av_ref[j, :, hs], ones)], axis=1)
            return 0
        lax.fori_loop(0, av_ref.shape[0], fill, 0)

    s_ref[0] = jnp.where((row >= LEAD) & (row <= qf), BIG, -BIG)
    eye = (lax.broadcasted_iota(jnp.int32, (128, 128), 0)
           == lax.broadcasted_iota(jnp.int32, (128, 128), 1)).astype(BF16)
    ikwq_t = lax.dot_general(eye, ikwq_ref[...], NT_DIMS, preferred_element_type=F32)
    w_scale = (IDX_DIM ** -0.5) * (IDX_HEADS ** -0.5)
    w_rows = [ikwq_t[IDX_DIM + h:IDX_DIM + h + 1, :] * w_scale for h in range(IDX_HEADS)]
    iq_heads = [iq_ref[:, h * IDX_DIM:(h + 1) * IDX_DIM] for h in range(IDX_HEADS)]

    def score_block(j, mabs):
        ik_j = ikw_ref[j, :, :IDX_DIM]
        acc = jnp.zeros((BLK, BLK), F32)
        for h in range(IDX_HEADS):
            st = lax.dot_general(ik_j, iq_heads[h], NT_DIMS, preferred_element_type=F32)
            acc = acc + jnp.maximum(st, 0.0) * w_rows[h]
        causal = (row + j * BLK) <= qf
        s_ref[j] = jnp.where(causal, acc, -BIG)
        return jnp.maximum(mabs, jnp.max(jnp.where(causal, jnp.abs(acc), 0.0), axis=0, keepdims=True))

    mabs = lax.fori_loop(1, i + 1, score_block, jnp.zeros((1, BLK), F32))

    k_eff = float(n_keep - N_META)
    bound = mabs * 1.000001 + 1e-30

    def count_ge(thr):
        def body(j, c):
            hit = (s_ref[j] >= thr).astype(F32).reshape(4, BLK // 32, 8, BLK)
            return c + jnp.sum(hit, axis=1)
        c = lax.fori_loop(1, i + 1, body, jnp.zeros((4, 8, BLK), F32))
        return jnp.sum(jnp.sum(c, axis=0), axis=0, keepdims=True)

    def unsettled(state):
        it, _, _, cnt_lo = state
        return (it < BISECT_ITERS) & (jnp.max(cnt_lo) > k_eff)

    def halve(lo, hi, cnt_lo):
        mid = lo + (hi - lo) * 0.5
        cnt = count_ge(mid)
        ge = cnt >= k_eff
        return jnp.where(ge, mid, lo), jnp.where(ge, hi, mid), jnp.where(ge, cnt, cnt_lo)

    def bisect(state):
        it, lo, hi, cnt_lo = state
        return (it + 2,) + halve(*halve(lo, hi, cnt_lo))

    _, lo, _, _ = lax.while_loop(unsettled, bisect, (jnp.int32(0), -bound, bound, count_ge(-bound)))

    def vmin_body(j, v):
        s = s_ref[j]
        return jnp.minimum(v, jnp.min(jnp.where(s >= lo, s, BIG), axis=0, keepdims=True))

    v = lax.fori_loop(1, i + 1, vmin_body, jnp.full((1, BLK), BIG, F32))
    v = jnp.where(v >= BIG, 0.0, v)

    def count_gt(j, c):
        return c + jnp.sum((s_ref[j] > v).astype(F32), axis=0, keepdims=True)

    ties_wanted = k_eff - lax.fori_loop(1, i + 1, count_gt, jnp.zeros((1, BLK), F32))

    acc_ref[...] = jnp.zeros_like(acc_ref)
    tri_incl = (col <= row).astype(BF16)
    aq_heads = [aq_ref[:, hs] * (A_HEAD_DIM ** -0.5) for hs in heads]

    def attend(j, carry, near):
        ties_seen, ms = carry
        s = s_ref[j]
        eq = s == v
        rank = ties_seen + jnp.dot(tri_incl, eq.astype(BF16), preferred_element_type=F32)
        sel = (s > v) | (eq & (rank <= ties_wanted))
        neg = jnp.where(sel, 0.0, -BIG).T
        ms_new = []
        for g in range(0, A_HEADS, DSA_HEAD_GROUP):
            group = range(g, g + DSA_HEAD_GROUP)
            zs = {h: lax.dot_general(aq_heads[h], ak_ref[j, :, heads[h]], NT_DIMS, preferred_element_type=F32)
                  + (bias_ref[h, i - j] + neg if near else neg) for h in group}
            alphas, ps = {}, {}
            for h in group:
                m_new = jnp.maximum(ms[h], jnp.max(zs[h], axis=1, keepdims=True))
                alphas[h] = jnp.exp(ms[h] - m_new)
                ps[h] = jnp.exp(zs[h] - m_new).astype(BF16)
                ms_new.append(m_new)
            for h in group:
                acc_ref[h] = alphas[h] * acc_ref[h] + jnp.dot(ps[h], av1_ref[j, :, heads1[h]],
                                                              preferred_element_type=F32)
        return rank[BLK - 1:BLK, :], tuple(ms_new)

    init = (jnp.zeros((1, BLK), F32), tuple(jnp.full((BLK, 1), -BIG, F32) for _ in heads))
    first_near = jnp.maximum(i - 1, 0)
    carry = lax.fori_loop(0, first_near, functools.partial(attend, near=False), init)
    lax.fori_loop(first_near, i + 1, functools.partial(attend, near=True), carry)

    valid_q = (lax.broadcasted_iota(jnp.int32, (BLK, 1), 0) + i * BLK) >= LEAD
    outs = []
    for h in range(A_HEADS):
        acc = acc_ref[h]
        outs.append((acc / pltpu.roll(acc, A_HEAD_DIM, 1))[:, :A_HEAD_DIM])
    o_ref[...] = jnp.where(valid_q, jnp.concatenate(outs, axis=1), 0.0).astype(o_ref.dtype)


def _dsa_attention(p0, ikw, bias, nbatch, nb, n_keep):
    rows = p0.shape[0]
    aw = A_HEADS * A_HEAD_DIM
    p0b = p0.reshape(nbatch, nb, BLK, p0.shape[1])
    ikwb = ikw.reshape(nbatch, nb, BLK, ikw.shape[1])
    return pl.pallas_call(
        functools.partial(_dsa_kernel, n_keep=n_keep),
        out_shape=jax.ShapeDtypeStruct((rows, aw), BF16),
        grid=(nbatch, nb),
        in_specs=[pl.BlockSpec((BLK, aw), lambda b, i: (b * nb + i, 0)),
                  pl.BlockSpec((BLK, aw), lambda b, i: (b * nb + i, 3)),
                  pl.BlockSpec((BLK, 128), lambda b, i: (b * nb + i, 0)),
                  pl.BlockSpec((None, nb, BLK, aw), lambda b, i: (b, 0, 0, 1)),
                  pl.BlockSpec((None, nb, BLK, aw), lambda b, i: (b, 0, 0, 2)),
                  pl.BlockSpec((None, nb, BLK, 128), lambda b, i: (b, 0, 0, 0)),
                  pl.BlockSpec((A_HEADS, 2, BLK, BLK), lambda b, i: (0, 0, 0, 0))],
        out_specs=pl.BlockSpec((BLK, aw), lambda b, i: (b * nb + i, 0)),
        scratch_shapes=[pltpu.VMEM((nb, BLK, BLK), F32),
                        pltpu.VMEM((A_HEADS, BLK, 2 * A_HEAD_DIM), F32),
                        pltpu.VMEM((nb, BLK, 2 * aw), BF16)],
        compiler_params=_params("parallel", "arbitrary"),
        name="dsa_attention",
    )(p0, p0, ikw, p0b, p0b, ikwb, bias)


def _retention_kernel(bq_ref, bk_ref, bv_ref, bg_ref, cos_ref, sin_ref, dmat_ref, xi_ref, zeta_ref,
                      gain_ref, o_ref, r_ref, *, g_chunk):
    @pl.when(pl.program_id(1) == 0)
    def _():
        r_ref[...] = jnp.zeros_like(r_ref)

    cosf = cos_ref[...]
    sinf = sin_ref[...]

    def rot(x):
        return x * cosf + pltpu.roll(x, B_QK_DIM // 2, 1) * sinf

    for h in range(B_HEADS):
        ks = slice(h * B_QK_DIM, (h + 1) * B_QK_DIM)
        vs = slice(h * B_V_DIM, (h + 1) * B_V_DIM)
        q = rot(bq_ref[:, ks].astype(F32))
        k = rot(bk_ref[:, ks].astype(F32)) * (B_QK_DIM ** -0.5)
        qb = q.astype(BF16)
        kb = k.astype(BF16)
        v = bv_ref[:, vs]
        inner = lax.dot_general(qb, kb, NT_DIMS, preferred_element_type=F32) * dmat_ref[h]
        r_old = r_ref[h]
        o = (jnp.dot(inner.astype(BF16), v, preferred_element_type=F32)
             + jnp.dot(qb, r_old.astype(BF16), preferred_element_type=F32) * xi_ref[h])
        kz = (k * zeta_ref[h]).astype(BF16)
        r_ref[h] = r_old * g_chunk[h] + lax.dot_general(kz, v, TN_DIMS, preferred_element_type=F32)
        mu = jnp.mean(o, axis=-1, keepdims=True)
        oc = o - mu
        var = jnp.mean(oc * oc, axis=-1, keepdims=True)
        rn = oc * lax.rsqrt(var + EPS) * gain_ref[:, vs]
        gate = bg_ref[:, vs].astype(F32)
        o_ref[:, vs] = (rn * (gate / (1.0 + jnp.exp(-gate)))).astype(o_ref.dtype)


def _retention(p0, gn_gain, nbatch, nb):
    rows = p0.shape[0]
    qkw, vw = B_HEADS * B_QK_DIM, B_HEADS * B_V_DIM
    frame = nb * BLK
    half = B_QK_DIM // 2
    pos = (jnp.arange(frame) - LEAD).astype(F32)
    inv = 1.0 / (ROPE_BASE ** (jnp.arange(half, dtype=F32) / half))
    ang = pos[:, None] * inv[None, :]
    cosf = jnp.concatenate([jnp.cos(ang), jnp.cos(ang)], axis=-1)
    sinf = jnp.concatenate([-jnp.sin(ang), jnp.sin(ang)], axis=-1)
    lg = jnp.log(1.0 - 2.0 ** (-5.0 - jnp.arange(B_HEADS, dtype=F32)))
    n = jnp.arange(BLK, dtype=F32)
    diff = n[:, None] - n[None, :]
    dmat = jnp.where(diff[None] >= 0, jnp.exp(jnp.maximum(diff, 0.0)[None] * lg[:, None, None]), 0.0)
    xi = jnp.broadcast_to(jnp.exp((n[None, :] + 1.0) * lg[:, None])[..., None], (B_HEADS, BLK, B_V_DIM))
    zeta = jnp.broadcast_to(jnp.exp((BLK - 1.0 - n[None, :]) * lg[:, None])[..., None],
                            (B_HEADS, BLK, B_QK_DIM))
    g_chunk = tuple(float(math.exp(BLK * math.log(1.0 - 2.0 ** (-5.0 - h)))) for h in range(B_HEADS))
    const = lambda shape: pl.BlockSpec(shape, lambda b, i: (0,) * len(shape))
    return pl.pallas_call(
        functools.partial(_retention_kernel, g_chunk=g_chunk),
        out_shape=jax.ShapeDtypeStruct((rows, vw), BF16),
        grid=(nbatch, nb),
        in_specs=[pl.BlockSpec((BLK, qkw), lambda b, i: (b * nb + i, 4)),
                  pl.BlockSpec((BLK, qkw), lambda b, i: (b * nb + i, 5)),
                  pl.BlockSpec((BLK, vw), lambda b, i: (b * nb + i, 3)),
                  pl.BlockSpec((BLK, vw), lambda b, i: (b * nb + i, 4)),
                  pl.BlockSpec((BLK, B_QK_DIM), lambda b, i: (i, 0)),
                  pl.BlockSpec((BLK, B_QK_DIM), lambda b, i: (i, 0)),
                  const((B_HEADS, BLK, BLK)),
                  const((B_HEADS, BLK, B_V_DIM)),
                  const((B_HEADS, BLK, B_QK_DIM)),
                  const((1, vw))],
        out_specs=pl.BlockSpec((BLK, vw), lambda b, i: (b * nb + i, 0)),
        scratch_shapes=[pltpu.VMEM((B_HEADS, B_QK_DIM, B_V_DIM), F32)],
        compiler_params=_params("parallel", "arbitrary"),
        name="retention",
    )(p0, p0, p0, p0, cosf, sinf, dmat, xi, zeta, gn_gain.reshape(1, vw).astype(F32))


def _proj_residual_kernel(*refs, n_pairs):
    h_ref = refs[2 * n_pairs]
    o_ref = refs[2 * n_pairs + 1]
    acc = h_ref[...]
    for t in range(n_pairs):
        acc = acc + jnp.dot(refs[2 * t][...], refs[2 * t + 1][...], preferred_element_type=F32)
    o_ref[...] = acc


def _proj_residual(pairs, h, tm):
    rows, d = h.shape
    in_specs, args = [], []
    for a, w in pairs:
        in_specs += [pl.BlockSpec((tm, a.shape[1]), lambda i: (i, 0)),
                     pl.BlockSpec(w.shape, lambda i: (0, 0))]
        args += [a, w]
    in_specs.append(pl.BlockSpec((tm, d), lambda i: (i, 0)))
    return pl.pallas_call(
        functools.partial(_proj_residual_kernel, n_pairs=len(pairs)),
        out_shape=jax.ShapeDtypeStruct((rows, d), F32),
        grid=(rows // tm,),
        in_specs=in_specs,
        out_specs=pl.BlockSpec((tm, d), lambda i: (i, 0)),
        compiler_params=_params("parallel"),
        name="proj_residual",
    )(*args, h)


def _ffn_tail_kernel(u_ref, g_ref, halo_ref, cw_ref, cb_ref, wd_ref, h_ref, *rest, final):
    g = g_ref[...].astype(F32)
    halo = halo_ref[...].astype(F32)
    rowi = lax.broadcasted_iota(jnp.int32, g.shape, 0)
    g1 = jnp.where(rowi >= 1, pltpu.roll(g, 1, 0), halo[15:16, :])
    g2 = jnp.where(rowi >= 2, pltpu.roll(g, 2, 0),
                   jnp.where(rowi == 1, halo[15:16, :], halo[14:15, :]))
    gc = g2 * cw_ref[0:1, :] + g1 * cw_ref[1:2, :] + g * cw_ref[2:3, :] + cb_ref[...]
    act = (gc / (1.0 + jnp.exp(-gc))) * u_ref[...].astype(F32)
    x = h_ref[...] + jnp.dot(act.astype(BF16), wd_ref[...], preferred_element_type=F32)
    if final:
        ng_ref, o_ref = rest
        ms = jnp.mean(x * x, axis=-1, keepdims=True)
        o_ref[...] = x * lax.rsqrt(ms + EPS) * ng_ref[...]
    else:
        rest[0][...] = x


def _ffn_tail(ug, conv_w, conv_b, w_down, h, tm, final=None):
    rows, d = h.shape
    dff = w_down.shape[0]
    assert dff % 128 == 0 and tm % 16 == 0
    in_specs = [pl.BlockSpec((tm, dff), lambda i: (i, 0)),
                pl.BlockSpec((tm, dff), lambda i: (i, 1)),
                pl.BlockSpec((16, dff), lambda i: (jnp.maximum(i * (tm // 16) - 1, 0), 1)),
                pl.BlockSpec((CONV_WIDTH, dff), lambda i: (0, 0)),
                pl.BlockSpec((1, dff), lambda i: (0, 0)),
                pl.BlockSpec((dff, d), lambda i: (0, 0)),
                pl.BlockSpec((tm, d), lambda i: (i, 0))]
    args = [ug, ug, ug, conv_w.astype(F32), conv_b.reshape(1, dff).astype(F32), w_down, h]
    if final is None:
        out_shape = jax.ShapeDtypeStruct((rows, d), F32)
        out_spec = pl.BlockSpec((tm, d), lambda i: (i, 0))
        sem = "parallel"
    else:
        gain, nbatch, nb = final
        assert tm == BLK
        in_specs.append(pl.BlockSpec((1, d), lambda i: (0, 0)))
        args.append(gain.reshape(1, d).astype(F32))
        out_shape = jax.ShapeDtypeStruct((nbatch, (nb - 1) * BLK, d), F32)
        out_spec = pl.BlockSpec((None, BLK, d), lambda i: (i // nb, jnp.maximum(i % nb - 1, 0), 0))
        sem = "arbitrary"
    return pl.pallas_call(
        functools.partial(_ffn_tail_kernel, final=final is not None),
        out_shape=out_shape,
        grid=(rows // tm,),
        in_specs=in_specs,
        out_specs=out_spec,
        compiler_params=_params(sem),
        name="ffn_tail",
    )(*args)


SB_GROUP = 4
SB_DEAD = 110.0


def _stick_breaking_kernel(q_ref, k_ref, v_ref, o_ref, acc_ref):
    i = pl.program_id(2)
    row = lax.broadcasted_iota(jnp.int32, (BLK, BLK), 0)
    col = lax.broadcasted_iota(jnp.int32, (BLK, BLK), 1)
    tri_after = (row > col).astype(BF16)
    scale = C_HEAD_DIM ** -0.5

    heads = [slice(h * C_HEAD_DIM, (h + 1) * C_HEAD_DIM) for h in range(SB_GROUP)]
    q_heads = [q_ref[:, hs] * scale for hs in heads]
    acc_ref[...] = jnp.zeros_like(acc_ref)

    def tiles(blocks, carries):
        keys = [(b, h) for b in range(len(blocks)) for h in range(SB_GROUP)]
        zs = {(b, h): lax.dot_general(q_heads[h], k_ref[blocks[b][0], :, heads[h]], NT_DIMS,
                                      preferred_element_type=F32) for b, h in keys}
        log_beta, rests, sums = {}, {}, {}
        for b, h in keys:
            z, mask = zs[b, h], blocks[b][1]
            sp = jnp.maximum(z, 0.0) + jnp.log(1.0 + jnp.exp2(jnp.abs(z) * -LOG2E))
            if mask is not None:
                sp = jnp.where(mask, sp, 0.0)
            log_beta[b, h] = z - sp
            sums[b, h] = jnp.sum(sp, axis=1, keepdims=True)
            rests[b, h] = jnp.dot(sp.astype(BF16), tri_after, preferred_element_type=F32)
        run = list(carries)
        for b, h in keys:
            j, mask = blocks[b]
            a = jnp.exp2((log_beta[b, h] - (rests[b, h] + run[h])) * LOG2E)
            if mask is not None:
                a = jnp.where(mask, a, 0.0)
            acc_ref[h] += jnp.dot(a.astype(BF16), v_ref[j, :, heads[h]], preferred_element_type=F32)
            run[h] = run[h] + sums[b, h]
        return tuple(run)

    def alive(carries):
        return jnp.min(functools.reduce(jnp.minimum, carries)) < SB_DEAD

    qf = row + i * BLK
    kf = col + i * BLK
    carries = tiles([(i, (kf < qf) & (kf >= LEAD))], tuple(jnp.zeros((BLK, 1), F32) for _ in heads))
    carries = lax.cond((i >= 2) & alive(carries), lambda c: tiles([(i - 1, None)], c), lambda c: c, carries)
    n_rest = jnp.maximum(i - 2, 0)
    n_pairs = jnp.right_shift(n_rest, 1)
    _, carries = lax.while_loop(
        lambda s: (s[0] < n_pairs) & alive(s[1]),
        lambda s: (s[0] + 1, tiles([(i - 2 - 2 * s[0], None), (i - 3 - 2 * s[0], None)], s[1])),
        (jnp.int32(0), carries))
    carries = lax.cond(((n_rest & 1) == 1) & alive(carries), lambda c: tiles([(1, None)], c), lambda c: c,
                       carries)

    @pl.when((i > 0) & alive(carries))
    def _():
        tiles([(0, col >= LEAD)], carries)

    o_ref[...] = jnp.concatenate([acc_ref[h] for h in range(SB_GROUP)], axis=1).astype(o_ref.dtype)


def _stick_breaking(p1, nbatch, nb):
    rows = p1.shape[0]
    cw = C_HEADS * C_HEAD_DIM
    gw = SB_GROUP * C_HEAD_DIM
    ngroups = C_HEADS // SB_GROUP
    p1b = p1.reshape(nbatch, nb, BLK, p1.shape[1])
    return pl.pallas_call(
        _stick_breaking_kernel,
        out_shape=jax.ShapeDtypeStruct((rows, cw), BF16),
        grid=(nbatch, ngroups, nb),
        in_specs=[pl.BlockSpec((BLK, gw), lambda b, g, i: (b * nb + i, g)),
                  pl.BlockSpec((None, nb, BLK, gw), lambda b, g, i: (b, 0, 0, ngroups + g)),
                  pl.BlockSpec((None, nb, BLK, gw), lambda b, g, i: (b, 0, 0, 2 * ngroups + g))],
        out_specs=pl.BlockSpec((BLK, gw), lambda b, g, i: (b * nb + i, g)),
        scratch_shapes=[pltpu.VMEM((SB_GROUP, BLK, C_HEAD_DIM), F32)],
        compiler_params=_params("parallel", "parallel", "arbitrary"),
        name="stick_breaking",
    )(p1, p1b, p1b)


def _pick_tile(rows, pref):
    t = pref
    while rows % t:
        t //= 2
    return t


def _pick_cols(n, cap=2816):
    return max(t for t in range(128, min(n, cap) + 1, 128) if n % t == 0)


def kernel(x, meta_tokens, rel_bias, norm_mix, norm_ffn, norm_final, even_w_in, even_gn_gain, even_w_out, odd_w_in, odd_w_out, ffn_w_up, ffn_w_gate, ffn_conv_w, ffn_conv_b, ffn_w_down):
    nbatch, seq, d = x.shape
    assert seq % BLK == 0
    nb = seq // BLK + 1
    rows = nbatch * nb * BLK
    n_keep = min(TOPK_MAX, seq // 4)
    assert n_keep >= N_META
    depth = norm_mix.shape[0]
    tm = _pick_tile(rows, 1024)

    meta = jnp.broadcast_to(meta_tokens[None].astype(x.dtype), (nbatch, N_META, d))
    h = jnp.concatenate([jnp.zeros((nbatch, LEAD, d), x.dtype), meta, x], axis=1).reshape(rows, d)

    aw = A_HEADS * A_HEAD_DIM
    qkw, vw = B_HEADS * B_QK_DIM, B_HEADS * B_V_DIM
    cw = C_HEADS * C_HEAD_DIM
    bias = _bias_tiles(rel_bias)

    for l in range(depth):
        j = l // 2
        if l % 2 == 0:
            w = even_w_in[j]
            o_iq, o_ik, o_iw, o_bq = 3 * aw, 4 * aw, 4 * aw + IDX_DIM, 4 * aw + IDX_DIM + IDX_HEADS
            w_main = jnp.concatenate([w[:, :4 * aw], w[:, o_bq:]], axis=1).astype(BF16)
            w_idx = jnp.concatenate([w[:, o_ik:o_bq], jnp.zeros((d, 128 - IDX_DIM - IDX_HEADS), w.dtype)],
                                    axis=1).astype(BF16)
            p0 = _norm_matmul(h, norm_mix[l], w_main, tm, _pick_cols(w_main.shape[1]))
            ikw = _norm_matmul(h, norm_mix[l], w_idx, tm, 128)
            a_out = _dsa_attention(p0, ikw, bias, nbatch, nb, n_keep)
            r_out = _retention(p0, even_gn_gain[j], nbatch, nb)
            w_out = even_w_out[j].astype(BF16)
            h = _proj_residual([(a_out, w_out[:aw]), (r_out, w_out[aw:])], h, _pick_tile(rows, 512))
        else:
            p1 = _norm_matmul(h, norm_mix[l], odd_w_in[j].astype(BF16), tm, _pick_cols(odd_w_in.shape[2]))
            s_out = _stick_breaking(p1, nbatch, nb)
            h = _proj_residual([(s_out, odd_w_out[j].astype(BF16))], h, _pick_tile(rows, 512))
        w_ug = jnp.concatenate([ffn_w_up[l], ffn_w_gate[l]], axis=1).astype(BF16)
        ug = _norm_matmul(h, norm_ffn[l], w_ug, tm, _pick_cols(w_ug.shape[1]))
        w_down = ffn_w_down[l].astype(BF16)
        if l + 1 < depth:
            h = _ffn_tail(ug, ffn_conv_w[l], ffn_conv_b[l], w_down, h, _pick_tile(rows, 512))
        else:
            h = _ffn_tail(ug, ffn_conv_w[l], ffn_conv_b[l], w_down, h, BLK, final=(norm_final, nbatch, nb))
    return h
```

```python
import functools
import math

import jax
import jax.numpy as jnp
import numpy as np
from jax import lax
from jax.experimental import pallas as pl
from jax.experimental.pallas import tpu as pltpu

N_META = 16
BLK = 256
LEAD = BLK - N_META
A_HEADS, A_HEAD_DIM = 8, 64
IDX_HEADS, IDX_DIM = 8, 64
TOPK_MAX = 256
N_BUCKETS, MAX_DISTANCE = 32, 128
B_HEADS, B_QK_DIM, B_V_DIM = 4, 128, 256
ROPE_BASE = 10000.0
C_HEADS, C_HEAD_DIM = 16, 64
CONV_WIDTH = 3
EPS = 1e-6
BIG = 1e30
LOG2E = 1.4426950408889634
BISECT_ITERS = 40
PEEL_SLACK = 3.0
DSA_HEAD_GROUP = 8
VMEM_LIMIT_BYTES = 56 * 1024 * 1024

F32 = jnp.float32
BF16 = jnp.bfloat16
NT_DIMS = (((1,), (1,)), ((), ()))
TN_DIMS = (((0,), (0,)), ((), ()))


def _params(*sem):
    return pltpu.CompilerParams(dimension_semantics=sem, vmem_limit_bytes=VMEM_LIMIT_BYTES)


def _norm_matmul_kernel(x_ref, g_ref, w_ref, o_ref, xn_ref):
    @pl.when(pl.program_id(1) == 0)
    def _():
        x = x_ref[...]
        ms = jnp.mean(x * x, axis=-1, keepdims=True)
        xn_ref[...] = (x * lax.rsqrt(ms + EPS) * g_ref[...]).astype(BF16)

    o_ref[...] = jnp.dot(xn_ref[...], w_ref[...], preferred_element_type=F32).astype(o_ref.dtype)


def _norm_matmul(x, gain, w, tm, tn):
    rows, d = x.shape
    n = w.shape[1]
    return pl.pallas_call(
        _norm_matmul_kernel,
        out_shape=jax.ShapeDtypeStruct((rows, n), BF16),
        grid=(rows // tm, n // tn),
        in_specs=[pl.BlockSpec((tm, d), lambda i, j: (i, 0)),
                  pl.BlockSpec((1, d), lambda i, j: (0, 0)),
                  pl.BlockSpec((d, tn), lambda i, j: (0, j))],
        out_specs=pl.BlockSpec((tm, tn), lambda i, j: (i, j)),
        scratch_shapes=[pltpu.VMEM((tm, d), BF16)],
        compiler_params=_params("parallel", "arbitrary"),
        name="norm_matmul",
    )(x, gain.reshape(1, d), w)


def _bucket_tiles():
    q = np.arange(BLK)[:, None]
    k = np.arange(BLK)[None, :]
    tiles = []
    for t in range(2):
        n = np.maximum(q - k + t * BLK, 0)
        max_exact = N_BUCKETS // 2
        large = max_exact + (np.log(np.maximum(n, 1).astype(np.float32) / max_exact)
                             / math.log(MAX_DISTANCE / max_exact)
                             * (N_BUCKETS - max_exact)).astype(np.int32)
        large = np.minimum(large, N_BUCKETS - 1)
        tiles.append(np.where(n < max_exact, n, large).astype(np.int32))
    return np.stack(tiles)


def _bias_tiles_kernel(rb_ref, idx_ref, o_ref):
    h = pl.program_id(0)
    far = rb_ref[N_BUCKETS - 1, h]
    for t in range(2):
        idx = idx_ref[t]
        acc = jnp.zeros((BLK, BLK), F32)
        for b in range(N_BUCKETS):
            acc = jnp.where(idx == b, rb_ref[b, h] - far, acc)
        o_ref[t] = acc


def _bias_tiles(rel_bias):
    assert BLK + 1 >= MAX_DISTANCE
    return pl.pallas_call(
        _bias_tiles_kernel,
        out_shape=jax.ShapeDtypeStruct((A_HEADS, 2, BLK, BLK), F32),
        grid=(A_HEADS,),
        in_specs=[pl.BlockSpec(memory_space=pltpu.SMEM),
                  pl.BlockSpec((2, BLK, BLK), lambda h: (0, 0, 0))],
        out_specs=pl.BlockSpec((None, 2, BLK, BLK), lambda h: (h, 0, 0, 0)),
        compiler_params=_params("parallel"),
        name="t5_bias_tiles",
    )(rel_bias.astype(F32), jnp.asarray(_bucket_tiles()))


def _dsa_kernel(aq_ref, iq_ref, ikwq_ref, ak_ref, av_ref, ikw_ref, bias_ref, o_ref,
                s_ref, acc_ref, av1_ref, *, n_keep):
    i = pl.program_id(1)
    row = lax.broadcasted_iota(jnp.int32, (BLK, BLK), 0)
    col = lax.broadcasted_iota(jnp.int32, (BLK, BLK), 1)
    qf = col + i * BLK
    heads = [slice(h * A_HEAD_DIM, (h + 1) * A_HEAD_DIM) for h in range(A_HEADS)]
    heads1 = [slice(h * 2 * A_HEAD_DIM, (h + 1) * 2 * A_HEAD_DIM) for h in range(A_HEADS)]

    @pl.when(i == 0)
    def _():
        def fill(j, _):
            ones = jnp.ones((BLK, A_HEAD_DIM), BF16)
            av1_ref[j] = jnp.concatenate([x for hs in heads for x in (av_ref[j, :, hs], ones)], axis=1)
            return 0
        lax.fori_loop(0, av_ref.shape[0], fill, 0)

    s_ref[0] = jnp.where((row >= LEAD) & (row <= qf), BIG, -BIG)
    eye = (lax.broadcasted_iota(jnp.int32, (128, 128), 0)
           == lax.broadcasted_iota(jnp.int32, (128, 128), 1)).astype(BF16)
    ikwq_t = lax.dot_general(eye, ikwq_ref[...], NT_DIMS, preferred_element_type=F32)
    w_scale = (IDX_DIM ** -0.5) * (IDX_HEADS ** -0.5)
    w_rows = [ikwq_t[IDX_DIM + h:IDX_DIM + h + 1, :] * w_scale for h in range(IDX_HEADS)]
    iq_heads = [iq_ref[:, h * IDX_DIM:(h + 1) * IDX_DIM] for h in range(IDX_HEADS)]

    def score_block(j, mabs):
        ik_j = ikw_ref[j, :, :IDX_DIM]
        acc = jnp.zeros((BLK, BLK), F32)
        for h in range(IDX_HEADS):
            st = lax.dot_general(ik_j, iq_heads[h], NT_DIMS, preferred_element_type=F32)
            acc = acc + jnp.maximum(st, 0.0) * w_rows[h]
        causal = (row + j * BLK) <= qf
        s_ref[j] = jnp.where(causal, acc, -BIG)
        return jnp.maximum(mabs, jnp.max(jnp.where(causal, jnp.abs(acc), 0.0), axis=0, keepdims=True))

    mabs = lax.fori_loop(1, i + 1, score_block, jnp.zeros((1, BLK), F32))

    k_eff = float(n_keep - N_META)
    bound = mabs * 1.000001 + 1e-30

    def count_ge(thr):
        def body(j, c):
            hit = (s_ref[j] >= thr).astype(F32).reshape(4, BLK // 32, 8, BLK)
            return c + jnp.sum(hit, axis=1)
        c = lax.fori_loop(1, i + 1, body, jnp.zeros((4, 8, BLK), F32))
        return jnp.sum(jnp.sum(c, axis=0), axis=0, keepdims=True)

    def unsettled(state):
        it, _, _, cnt_lo = state
        return (it < BISECT_ITERS) & (jnp.max(cnt_lo) > k_eff + PEEL_SLACK)

    def halve(lo, hi, cnt_lo):
        mid = lo + (hi - lo) * 0.5
        cnt = count_ge(mid)
        ge = cnt >= k_eff
        return jnp.where(ge, mid, lo), jnp.where(ge, hi, mid), jnp.where(ge, cnt, cnt_lo)

    def bisect(state):
        it, lo, hi, cnt_lo = state
        return (it + 2,) + halve(*halve(lo, hi, cnt_lo))

    _, lo, _, _ = lax.while_loop(unsettled, bisect, (jnp.int32(0), -bound, bound, count_ge(-bound)))

    def vmin_body(j, v):
        s = s_ref[j]
        return jnp.minimum(v, jnp.min(jnp.where(s >= lo, s, BIG), axis=0, keepdims=True))

    v = lax.fori_loop(1, i + 1, vmin_body, jnp.full((1, BLK), BIG, F32))
    v = jnp.where(v >= BIG, 0.0, v)

    def above(x):
        def body(j, carry):
            cnt, nxt = carry
            s = s_ref[j]
            gt = s > x
            cnt = cnt + jnp.sum(gt.astype(F32).reshape(4, BLK // 32, 8, BLK), axis=1)
            nxt = jnp.minimum(nxt, jnp.min(jnp.where(gt, s, BIG).reshape(4, BLK // 32, 8, BLK), axis=1))
            return cnt, nxt
        cnt, nxt = lax.fori_loop(1, i + 1, body, (jnp.zeros((4, 8, BLK), F32), jnp.full((4, 8, BLK), BIG, F32)))
        return (jnp.sum(jnp.sum(cnt, axis=0), axis=0, keepdims=True),
                jnp.min(jnp.min(nxt, axis=0), axis=0, keepdims=True))

    def peel(state):
        x, cnt_gt, nxt = state
        x = jnp.where(cnt_gt >= k_eff, nxt, x)
        return (x,) + above(x)

    v, cnt_gt, _ = lax.while_loop(lambda s: jnp.max(s[1]) >= k_eff, peel, (v,) + above(v))
    ties_wanted = k_eff - cnt_gt

    acc_ref[...] = jnp.zeros_like(acc_ref)
    tri_incl = (col <= row).astype(BF16)
    aq_heads = [aq_ref[:, hs] * (A_HEAD_DIM ** -0.5) for hs in heads]

    def attend(j, carry, near):
        ties_seen, ms = carry
        s = s_ref[j]
        eq = s == v
        rank = ties_seen + jnp.dot(tri_incl, eq.astype(BF16), preferred_element_type=F32)
        sel = (s > v) | (eq & (rank <= ties_wanted))
        neg = jnp.where(sel, 0.0, -BIG).T
        ms_new = []
        for g in range(0, A_HEADS, DSA_HEAD_GROUP):
            group = range(g, g + DSA_HEAD_GROUP)
            zs = {h: lax.dot_general(aq_heads[h], ak_ref[j, :, heads[h]], NT_DIMS, preferred_element_type=F32)
                  + (bias_ref[h, i - j] + neg if near else neg) for h in group}
            alphas, ps = {}, {}
            for h in group:
                m_new = jnp.maximum(ms[h], jnp.max(zs[h], axis=1, keepdims=True))
                alphas[h] = jnp.exp(ms[h] - m_new)
                ps[h] = jnp.exp(zs[h] - m_new).astype(BF16)
                ms_new.append(m_new)
            for h in group:
                acc_ref[h] = alphas[h] * acc_ref[h] + jnp.dot(ps[h], av1_ref[j, :, heads1[h]],
                                                              preferred_element_type=F32)
        return rank[BLK - 1:BLK, :], tuple(ms_new)

    init = (jnp.zeros((1, BLK), F32), tuple(jnp.full((BLK, 1), -BIG, F32) for _ in heads))
    first_near = jnp.maximum(i - 1, 0)
    carry = lax.fori_loop(0, first_near, functools.partial(attend, near=False), init)
    lax.fori_loop(first_near, i + 1, functools.partial(attend, near=True), carry)

    valid_q = (lax.broadcasted_iota(jnp.int32, (BLK, 1), 0) + i * BLK) >= LEAD
    outs = []
    for h in range(A_HEADS):
        acc = acc_ref[h]
        outs.append((acc / pltpu.roll(acc, A_HEAD_DIM, 1))[:, :A_HEAD_DIM])
    o_ref[...] = jnp.where(valid_q, jnp.concatenate(outs, axis=1), 0.0).astype(o_ref.dtype)


def _dsa_attention(p0, ikw, bias, nbatch, nb, n_keep):
    rows = p0.shape[0]
    aw = A_HEADS * A_HEAD_DIM
    p0b = p0.reshape(nbatch, nb, BLK, p0.shape[1])
    ikwb = ikw.reshape(nbatch, nb, BLK, ikw.shape[1])
    return pl.pallas_call(
        functools.partial(_dsa_kernel, n_keep=n_keep),
        out_shape=jax.ShapeDtypeStruct((rows, aw), BF16),
        grid=(nbatch, nb),
        in_specs=[pl.BlockSpec((BLK, aw), lambda b, i: (b * nb + i, 0)),
                  pl.BlockSpec((BLK, aw), lambda b, i: (b * nb + i, 3)),
                  pl.BlockSpec((BLK, 128), lambda b, i: (b * nb + i, 0)),
                  pl.BlockSpec((None, nb, BLK, aw), lambda b, i: (b, 0, 0, 1)),
                  pl.BlockSpec((None, nb, BLK, aw), lambda b, i: (b, 0, 0, 2)),
                  pl.BlockSpec((None, nb, BLK, 128), lambda b, i: (b, 0, 0, 0)),
                  pl.BlockSpec((A_HEADS, 2, BLK, BLK), lambda b, i: (0, 0, 0, 0))],
        out_specs=pl.BlockSpec((BLK, aw), lambda b, i: (b * nb + i, 0)),
        scratch_shapes=[pltpu.VMEM((nb, BLK, BLK), F32),
                        pltpu.VMEM((A_HEADS, BLK, 2 * A_HEAD_DIM), F32),
                        pltpu.VMEM((nb, BLK, 2 * aw), BF16)],
        compiler_params=_params("parallel", "arbitrary"),
        name="dsa_attention",
    )(p0, p0, ikw, p0b, p0b, ikwb, bias)


def _retention_kernel(bq_ref, bk_ref, bv_ref, bg_ref, cos_ref, sin_ref, dmat_ref, xi_ref, zeta_ref,
                      gain_ref, o_ref, r_ref, *, g_chunk):
    @pl.when(pl.program_id(1) == 0)
    def _():
        r_ref[...] = jnp.zeros_like(r_ref)

    cosf = cos_ref[...]
    sinf = sin_ref[...]

    def rot(x):
        return x * cosf + pltpu.roll(x, B_QK_DIM // 2, 1) * sinf

    for h in range(B_HEADS):
        ks = slice(h * B_QK_DIM, (h + 1) * B_QK_DIM)
        vs = slice(h * B_V_DIM, (h + 1) * B_V_DIM)
        q = rot(bq_ref[:, ks].astype(F32))
        k = rot(bk_ref[:, ks].astype(F32)) * (B_QK_DIM ** -0.5)
        qb = q.astype(BF16)
        kb = k.astype(BF16)
        v = bv_ref[:, vs]
        inner = lax.dot_general(qb, kb, NT_DIMS, preferred_element_type=F32) * dmat_ref[h]
        r_old = r_ref[h]
        o = (jnp.dot(inner.astype(BF16), v, preferred_element_type=F32)
             + jnp.dot(qb, r_old.astype(BF16), preferred_element_type=F32) * xi_ref[h])
        kz = (k * zeta_ref[h]).astype(BF16)
        r_ref[h] = r_old * g_chunk[h] + lax.dot_general(kz, v, TN_DIMS, preferred_element_type=F32)
        mu = jnp.mean(o, axis=-1, keepdims=True)
        oc = o - mu
        var = jnp.mean(oc * oc, axis=-1, keepdims=True)
        rn = oc * lax.rsqrt(var + EPS) * gain_ref[:, vs]
        gate = bg_ref[:, vs].astype(F32)
        o_ref[:, vs] = (rn * (gate / (1.0 + jnp.exp(-gate)))).astype(o_ref.dtype)


def _retention(p0, gn_gain, nbatch, nb):
    rows = p0.shape[0]
    qkw, vw = B_HEADS * B_QK_DIM, B_HEADS * B_V_DIM
    frame = nb * BLK
    half = B_QK_DIM // 2
    pos = (jnp.arange(frame) - LEAD).astype(F32)
    inv = 1.0 / (ROPE_BASE ** (jnp.arange(half, dtype=F32) / half))
    ang = pos[:, None] * inv[None, :]
    cosf = jnp.concatenate([jnp.cos(ang), jnp.cos(ang)], axis=-1)
    sinf = jnp.concatenate([-jnp.sin(ang), jnp.sin(ang)], axis=-1)
    lg = jnp.log(1.0 - 2.0 ** (-5.0 - jnp.arange(B_HEADS, dtype=F32)))
    n = jnp.arange(BLK, dtype=F32)
    diff = n[:, None] - n[None, :]
    dmat = jnp.where(diff[None] >= 0, jnp.exp(jnp.maximum(diff, 0.0)[None] * lg[:, None, None]), 0.0)
    xi = jnp.broadcast_to(jnp.exp((n[None, :] + 1.0) * lg[:, None])[..., None], (B_HEADS, BLK, B_V_DIM))
    zeta = jnp.broadcast_to(jnp.exp((BLK - 1.0 - n[None, :]) * lg[:, None])[..., None],
                            (B_HEADS, BLK, B_QK_DIM))
    g_chunk = tuple(float(math.exp(BLK * math.log(1.0 - 2.0 ** (-5.0 - h)))) for h in range(B_HEADS))
    const = lambda shape: pl.BlockSpec(shape, lambda b, i: (0,) * len(shape))
    return pl.pallas_call(
        functools.partial(_retention_kernel, g_chunk=g_chunk),
        out_shape=jax.ShapeDtypeStruct((rows, vw), BF16),
        grid=(nbatch, nb),
        in_specs=[pl.BlockSpec((BLK, qkw), lambda b, i: (b * nb + i, 4)),
                  pl.BlockSpec((BLK, qkw), lambda b, i: (b * nb + i, 5)),
                  pl.BlockSpec((BLK, vw), lambda b, i: (b * nb + i, 3)),
                  pl.BlockSpec((BLK, vw), lambda b, i: (b * nb + i, 4)),
                  pl.BlockSpec((BLK, B_QK_DIM), lambda b, i: (i, 0)),
                  pl.BlockSpec((BLK, B_QK_DIM), lambda b, i: (i, 0)),
                  const((B_HEADS, BLK, BLK)),
                  const((B_HEADS, BLK, B_V_DIM)),
                  const((B_HEADS, BLK, B_QK_DIM)),
                  const((1, vw))],
        out_specs=pl.BlockSpec((BLK, vw), lambda b, i: (b * nb + i, 0)),
        scratch_shapes=[pltpu.VMEM((B_HEADS, B_QK_DIM, B_V_DIM), F32)],
        compiler_params=_params("parallel", "arbitrary"),
        name="retention",
    )(p0, p0, p0, p0, cosf, sinf, dmat, xi, zeta, gn_gain.reshape(1, vw).astype(F32))


def _proj_residual_kernel(*refs, n_pairs, final):
    x = refs[2 * n_pairs][...]
    for t in range(n_pairs):
        x = x + jnp.dot(refs[2 * t][...], refs[2 * t + 1][...], preferred_element_type=F32)
    if final:
        ng_ref, o_ref = refs[2 * n_pairs + 1:]
        ms = jnp.mean(x * x, axis=-1, keepdims=True)
        o_ref[...] = x * lax.rsqrt(ms + EPS) * ng_ref[...]
    else:
        refs[2 * n_pairs + 1][...] = x


def _proj_residual(pairs, h, tm, final=None):
    rows, d = h.shape
    in_specs, args = [], []
    for a, w in pairs:
        in_specs += [pl.BlockSpec((tm, a.shape[1]), lambda i: (i, 0)),
                     pl.BlockSpec(w.shape, lambda i: (0, 0))]
        args += [a, w]
    in_specs.append(pl.BlockSpec((tm, d), lambda i: (i, 0)))
    args.append(h)
    if final is None:
        out_shape = jax.ShapeDtypeStruct((rows, d), F32)
        out_spec = pl.BlockSpec((tm, d), lambda i: (i, 0))
        sem = "parallel"
    else:
        gain, nbatch, nb = final
        assert tm == BLK
        in_specs.append(pl.BlockSpec((1, d), lambda i: (0, 0)))
        args.append(gain.reshape(1, d).astype(F32))
        out_shape = jax.ShapeDtypeStruct((nbatch, (nb - 1) * BLK, d), F32)
        out_spec = pl.BlockSpec((None, BLK, d), lambda i: (i // nb, jnp.maximum(i % nb - 1, 0), 0))
        sem = "arbitrary"
    return pl.pallas_call(
        functools.partial(_proj_residual_kernel, n_pairs=len(pairs), final=final is not None),
        out_shape=out_shape,
        grid=(rows // tm,),
        in_specs=in_specs,
        out_specs=out_spec,
        compiler_params=_params(sem),
        name="proj_residual",
    )(*args)


def _norm_gate_kernel(x_ref, gn_ref, w_ref, cw_ref, cb_ref, o_ref, xn_ref, halo_ref):
    i, j = pl.program_id(0), pl.program_id(1)
    tc = o_ref.shape[1]

    @pl.when(j == 0)
    def _():
        x = x_ref[...]
        ms = jnp.mean(x * x, axis=-1, keepdims=True)
        xn_ref[...] = (x * lax.rsqrt(ms + EPS) * gn_ref[...]).astype(BF16)

    @pl.when(i == 0)
    def _():
        halo_ref[j] = jnp.zeros(halo_ref.shape[1:], F32)

    r = jnp.dot(xn_ref[...], w_ref[...], preferred_element_type=F32)
    u, g = r[:, :tc], r[:, tc:]
    prev = halo_ref[j]
    halo_ref[j] = g[g.shape[0] - 8:, :]
    row8 = lax.broadcasted_iota(jnp.int32, (8, tc), 0)
    g1 = pltpu.roll(g, 1, 0)
    g2 = pltpu.roll(g, 2, 0)
    g1 = jnp.concatenate([jnp.where(row8 >= 1, g1[:8], prev[7:8]), g1[8:]], axis=0)
    g2 = jnp.concatenate([jnp.where(row8 >= 2, g2[:8], jnp.where(row8 == 1, prev[7:8], prev[6:7])), g2[8:]],
                         axis=0)
    gc = g2 * cw_ref[0:1, :] + g1 * cw_ref[1:2, :] + g * cw_ref[2:3, :] + cb_ref[...]
    o_ref[...] = ((gc / (1.0 + jnp.exp(-gc))) * u).astype(o_ref.dtype)


def _norm_gated_up(x, gain, w_up, w_gate, conv_w, conv_b, tm, tc):
    rows, d = x.shape
    dff = w_up.shape[1]
    nct = dff // tc
    assert dff % tc == 0 and tc % 128 == 0
    w = jnp.concatenate([w_up.reshape(d, nct, tc), w_gate.reshape(d, nct, tc)], axis=2).reshape(d, 2 * dff)
    return pl.pallas_call(
        _norm_gate_kernel,
        out_shape=jax.ShapeDtypeStruct((rows, dff), BF16),
        grid=(rows // tm, nct),
        in_specs=[pl.BlockSpec((tm, d), lambda i, j: (i, 0)),
                  pl.BlockSpec((1, d), lambda i, j: (0, 0)),
                  pl.BlockSpec((d, 2 * tc), lambda i, j: (0, j)),
                  pl.BlockSpec((CONV_WIDTH, tc), lambda i, j: (0, j)),
                  pl.BlockSpec((1, tc), lambda i, j: (0, j))],
        out_specs=pl.BlockSpec((tm, tc), lambda i, j: (i, j)),
        scratch_shapes=[pltpu.VMEM((tm, d), BF16), pltpu.VMEM((nct, 8, tc), F32)],
        compiler_params=_params("arbitrary", "arbitrary"),
        name="norm_gated_up",
    )(x, gain.reshape(1, d), w.astype(BF16), conv_w.astype(F32), conv_b.reshape(1, dff).astype(F32))


SB_GROUP = 4
SB_DEAD = 110.0


def _stick_breaking_kernel(q_ref, k_ref, v_ref, o_ref, acc_ref):
    i = pl.program_id(2)
    row = lax.broadcasted_iota(jnp.int32, (BLK, BLK), 0)
    col = lax.broadcasted_iota(jnp.int32, (BLK, BLK), 1)
    tri_after = (row > col).astype(BF16)
    scale = C_HEAD_DIM ** -0.5

    heads = [slice(h * C_HEAD_DIM, (h + 1) * C_HEAD_DIM) for h in range(SB_GROUP)]
    q_heads = [q_ref[:, hs] * scale for hs in heads]
    acc_ref[...] = jnp.zeros_like(acc_ref)

    def tiles(blocks, carries):
        keys = [(b, h) for b in range(len(blocks)) for h in range(SB_GROUP)]
        zs = {(b, h): lax.dot_general(q_heads[h], k_ref[blocks[b][0], :, heads[h]], NT_DIMS,
                                      preferred_element_type=F32) for b, h in keys}
        log_beta, rests, sums = {}, {}, {}
        for b, h in keys:
            z, mask = zs[b, h], blocks[b][1]
            sp = jnp.maximum(z, 0.0) + jnp.log(1.0 + jnp.exp2(jnp.abs(z) * -LOG2E))
            if mask is not None:
                sp = jnp.where(mask, sp, 0.0)
            log_beta[b, h] = z - sp
            sums[b, h] = jnp.sum(sp, axis=1, keepdims=True)
            rests[b, h] = jnp.dot(sp.astype(BF16), tri_after, preferred_element_type=F32)
        run = list(carries)
        for b, h in keys:
            j, mask = blocks[b]
            a = jnp.exp2((log_beta[b, h] - (rests[b, h] + run[h])) * LOG2E)
            if mask is not None:
                a = jnp.where(mask, a, 0.0)
            acc_ref[h] += jnp.dot(a.astype(BF16), v_ref[j, :, heads[h]], preferred_element_type=F32)
            run[h] = run[h] + sums[b, h]
        return tuple(run)

    def alive(carries):
        return jnp.min(functools.reduce(jnp.minimum, carries)) < SB_DEAD

    qf = row + i * BLK
    kf = col + i * BLK
    carries = tiles([(i, (kf < qf) & (kf >= LEAD))], tuple(jnp.zeros((BLK, 1), F32) for _ in heads))
    carries = lax.cond((i >= 2) & alive(carries), lambda c: tiles([(i - 1, None)], c), lambda c: c, carries)
    n_rest = jnp.maximum(i - 2, 0)
    n_pairs = jnp.right_shift(n_rest, 1)
    _, carries = lax.while_loop(
        lambda s: (s[0] < n_pairs) & alive(s[1]),
        lambda s: (s[0] + 1, tiles([(i - 2 - 2 * s[0], None), (i - 3 - 2 * s[0], None)], s[1])),
        (jnp.int32(0), carries))
    carries = lax.cond(((n_rest & 1) == 1) & alive(carries), lambda c: tiles([(1, None)], c), lambda c: c,
                       carries)

    @pl.when((i > 0) & alive(carries))
    def _():
        tiles([(0, col >= LEAD)], carries)

    o_ref[...] = jnp.concatenate([acc_ref[h] for h in range(SB_GROUP)], axis=1).astype(o_ref.dtype)


def _stick_breaking(p1, nbatch, nb):
    rows = p1.shape[0]
    cw = C_HEADS * C_HEAD_DIM
    gw = SB_GROUP * C_HEAD_DIM
    ngroups = C_HEADS // SB_GROUP
    p1b = p1.reshape(nbatch, nb, BLK, p1.shape[1])
    return pl.pallas_call(
        _stick_breaking_kernel,
        out_shape=jax.ShapeDtypeStruct((rows, cw), BF16),
        grid=(nbatch, ngroups, nb),
        in_specs=[pl.BlockSpec((BLK, gw), lambda b, g, i: (b * nb + i, g)),
                  pl.BlockSpec((None, nb, BLK, gw), lambda b, g, i: (b, 0, 0, ngroups + g)),
                  pl.BlockSpec((None, nb, BLK, gw), lambda b, g, i: (b, 0, 0, 2 * ngroups + g))],
        out_specs=pl.BlockSpec((BLK, gw), lambda b, g, i: (b * nb + i, g)),
        scratch_shapes=[pltpu.VMEM((SB_GROUP, BLK, C_HEAD_DIM), F32)],
        compiler_params=_params("parallel", "parallel", "arbitrary"),
        name="stick_breaking",
    )(p1, p1b, p1b)


def _pick_tile(rows, pref):
    t = pref
    while rows % t:
        t //= 2
    return t


def _pick_cols(n, cap=2816):
    return max(t for t in range(128, min(n, cap) + 1, 128) if n % t == 0)


def kernel(x, meta_tokens, rel_bias, norm_mix, norm_ffn, norm_final, even_w_in, even_gn_gain, even_w_out, odd_w_in, odd_w_out, ffn_w_up, ffn_w_gate, ffn_conv_w, ffn_conv_b, ffn_w_down):
    nbatch, seq, d = x.shape
    assert seq % BLK == 0
    nb = seq // BLK + 1
    rows = nbatch * nb * BLK
    n_keep = min(TOPK_MAX, seq // 4)
    assert n_keep >= N_META
    depth = norm_mix.shape[0]
    tm = _pick_tile(rows, 1024)

    meta = jnp.broadcast_to(meta_tokens[None].astype(x.dtype), (nbatch, N_META, d))
    h = jnp.concatenate([jnp.zeros((nbatch, LEAD, d), x.dtype), meta, x], axis=1).reshape(rows, d)

    aw = A_HEADS * A_HEAD_DIM
    qkw, vw = B_HEADS * B_QK_DIM, B_HEADS * B_V_DIM
    cw = C_HEADS * C_HEAD_DIM
    bias = _bias_tiles(rel_bias)

    for l in range(depth):
        j = l // 2
        if l % 2 == 0:
            w = even_w_in[j]
            o_iq, o_ik, o_iw, o_bq = 3 * aw, 4 * aw, 4 * aw + IDX_DIM, 4 * aw + IDX_DIM + IDX_HEADS
            w_main = jnp.concatenate([w[:, :4 * aw], w[:, o_bq:]], axis=1).astype(BF16)
            w_idx = jnp.concatenate([w[:, o_ik:o_bq], jnp.zeros((d, 128 - IDX_DIM - IDX_HEADS), w.dtype)],
                                    axis=1).astype(BF16)
            p0 = _norm_matmul(h, norm_mix[l], w_main, tm, _pick_cols(w_main.shape[1]))
            ikw = _norm_matmul(h, norm_mix[l], w_idx, tm, 128)
            a_out = _dsa_attention(p0, ikw, bias, nbatch, nb, n_keep)
            r_out = _retention(p0, even_gn_gain[j], nbatch, nb)
            w_out = even_w_out[j].astype(BF16)
            h = _proj_residual([(a_out, w_out[:aw]), (r_out, w_out[aw:])], h, _pick_tile(rows, 512))
        else:
            p1 = _norm_matmul(h, norm_mix[l], odd_w_in[j].astype(BF16), tm, _pick_cols(odd_w_in.shape[2]))
            s_out = _stick_breaking(p1, nbatch, nb)
            h = _proj_residual([(s_out, odd_w_out[j].astype(BF16))], h, _pick_tile(rows, 512))
        dff = ffn_w_up.shape[2]
        act = _norm_gated_up(h, norm_ffn[l], ffn_w_up[l], ffn_w_gate[l], ffn_conv_w[l], ffn_conv_b[l],
                             _pick_tile(rows, 512), _pick_cols(dff, dff // 2))
        w_down = ffn_w_down[l].astype(BF16)
        if l + 1 < depth:
            h = _proj_residual([(act, w_down)], h, _pick_tile(rows, 512))
        else:
            h = _proj_residual([(act, w_down)], h, BLK, final=(norm_final, nbatch, nb))
    return h
```

```python
import functools
import math

import jax
import jax.numpy as jnp
import numpy as np
from jax import lax
from jax.experimental import pallas as pl
from jax.experimental.pallas import tpu as pltpu

N_META = 16
BLK = 256
LEAD = BLK - N_META
A_HEADS, A_HEAD_DIM = 8, 64
IDX_HEADS, IDX_DIM = 8, 64
TOPK_MAX = 256
N_BUCKETS, MAX_DISTANCE = 32, 128
B_HEADS, B_QK_DIM, B_V_DIM = 4, 128, 256
ROPE_BASE = 10000.0
C_HEADS, C_HEAD_DIM = 16, 64
CONV_WIDTH = 3
EPS = 1e-6
BIG = 1e30
LOG2E = 1.4426950408889634
BISECT_ITERS = 40
PEEL_SLACK = 3.0
DSA_LOOKAHEAD = 8
VMEM_LIMIT_BYTES = 56 * 1024 * 1024

F32 = jnp.float32
BF16 = jnp.bfloat16
NT_DIMS = (((1,), (1,)), ((), ()))
TN_DIMS = (((0,), (0,)), ((), ()))


def _params(*sem):
    return pltpu.CompilerParams(dimension_semantics=sem, vmem_limit_bytes=VMEM_LIMIT_BYTES)


def _norm_matmul_kernel(x_ref, g_ref, w_ref, o_ref, xn_ref):
    @pl.when(pl.program_id(1) == 0)
    def _():
        x = x_ref[...]
        ms = jnp.mean(x * x, axis=-1, keepdims=True)
        xn_ref[...] = (x * lax.rsqrt(ms + EPS) * g_ref[...]).astype(BF16)

    o_ref[...] = jnp.dot(xn_ref[...], w_ref[...], preferred_element_type=F32).astype(o_ref.dtype)


def _norm_matmul(x, gain, w, tm, tn):
    rows, d = x.shape
    n = w.shape[1]
    return pl.pallas_call(
        _norm_matmul_kernel,
        out_shape=jax.ShapeDtypeStruct((rows, n), BF16),
        grid=(rows // tm, n // tn),
        in_specs=[pl.BlockSpec((tm, d), lambda i, j: (i, 0)),
                  pl.BlockSpec((1, d), lambda i, j: (0, 0)),
                  pl.BlockSpec((d, tn), lambda i, j: (0, j))],
        out_specs=pl.BlockSpec((tm, tn), lambda i, j: (i, j)),
        scratch_shapes=[pltpu.VMEM((tm, d), BF16)],
        compiler_params=_params("parallel", "arbitrary"),
        name="norm_matmul",
    )(x, gain.reshape(1, d), w)


def _bucket_tiles():
    q = np.arange(BLK)[:, None]
    k = np.arange(BLK)[None, :]
    tiles = []
    for t in range(2):
        n = np.maximum(q - k + t * BLK, 0)
        max_exact = N_BUCKETS // 2
        large = max_exact + (np.log(np.maximum(n, 1).astype(np.float32) / max_exact)
                             / math.log(MAX_DISTANCE / max_exact)
                             * (N_BUCKETS - max_exact)).astype(np.int32)
        large = np.minimum(large, N_BUCKETS - 1)
        tiles.append(np.where(n < max_exact, n, large).astype(np.int32))
    return np.stack(tiles)


def _bias_tiles_kernel(rb_ref, idx_ref, o_ref):
    h = pl.program_id(0)
    far = rb_ref[N_BUCKETS - 1, h]
    for t in range(2):
        idx = idx_ref[t]
        acc = jnp.zeros((BLK, BLK), F32)
        for b in range(N_BUCKETS):
            acc = jnp.where(idx == b, rb_ref[b, h] - far, acc)
        o_ref[t] = acc


def _bias_tiles(rel_bias):
    assert BLK + 1 >= MAX_DISTANCE
    return pl.pallas_call(
        _bias_tiles_kernel,
        out_shape=jax.ShapeDtypeStruct((A_HEADS, 2, BLK, BLK), F32),
        grid=(A_HEADS,),
        in_specs=[pl.BlockSpec(memory_space=pltpu.SMEM),
                  pl.BlockSpec((2, BLK, BLK), lambda h: (0, 0, 0))],
        out_specs=pl.BlockSpec((None, 2, BLK, BLK), lambda h: (h, 0, 0, 0)),
        compiler_params=_params("parallel"),
        name="t5_bias_tiles",
    )(rel_bias.astype(F32), jnp.asarray(_bucket_tiles()))


def _dsa_kernel(aq_ref, iq_ref, ikwq_ref, ak_ref, av_ref, ikw_ref, bias_ref, o_ref,
                s_ref, acc_ref, av1_ref, *, n_keep):
    i = pl.program_id(1)
    row = lax.broadcasted_iota(jnp.int32, (BLK, BLK), 0)
    col = lax.broadcasted_iota(jnp.int32, (BLK, BLK), 1)
    qf = col + i * BLK
    heads = [slice(h * A_HEAD_DIM, (h + 1) * A_HEAD_DIM) for h in range(A_HEADS)]
    heads1 = [slice(h * 2 * A_HEAD_DIM, (h + 1) * 2 * A_HEAD_DIM) for h in range(A_HEADS)]

    @pl.when(i == 0)
    def _():
        def fill(j, _):
            ones = jnp.ones((BLK, A_HEAD_DIM), BF16)
            av1_ref[j] = jnp.concatenate([x for hs in heads for x in (av_ref[j, :, hs], ones)], axis=1)
            return 0
        lax.fori_loop(0, av_ref.shape[0], fill, 0)

    s_ref[0] = jnp.where((row >= LEAD) & (row <= qf), BIG, -BIG)
    eye = (lax.broadcasted_iota(jnp.int32, (128, 128), 0)
           == lax.broadcasted_iota(jnp.int32, (128, 128), 1)).astype(BF16)
    ikwq_t = lax.dot_general(eye, ikwq_ref[...], NT_DIMS, preferred_element_type=F32)
    w_scale = (IDX_DIM ** -0.5) * (IDX_HEADS ** -0.5)
    w_rows = [ikwq_t[IDX_DIM + h:IDX_DIM + h + 1, :] * w_scale for h in range(IDX_HEADS)]
    iq_heads = [iq_ref[:, h * IDX_DIM:(h + 1) * IDX_DIM] for h in range(IDX_HEADS)]

    def score_block(j, mabs):
        ik_j = ikw_ref[j, :, :IDX_DIM]
        acc = jnp.zeros((BLK, BLK), F32)
        for h in range(IDX_HEADS):
            st = lax.dot_general(ik_j, iq_heads[h], NT_DIMS, preferred_element_type=F32)
            acc = acc + jnp.maximum(st, 0.0) * w_rows[h]
        causal = (row + j * BLK) <= qf
        s_ref[j] = jnp.where(causal, acc, -BIG)
        return jnp.maximum(mabs, jnp.max(jnp.where(causal, jnp.abs(acc), 0.0), axis=0, keepdims=True))

    mabs = lax.fori_loop(1, i + 1, score_block, jnp.zeros((1, BLK), F32))

    k_eff = float(n_keep - N_META)
    bound = mabs * 1.000001 + 1e-30

    def count_ge(thr):
        def body(j, c):
            hit = (s_ref[j] >= thr).astype(F32).reshape(4, BLK // 32, 8, BLK)
            return c + jnp.sum(hit, axis=1)
        c = lax.fori_loop(1, i + 1, body, jnp.zeros((4, 8, BLK), F32))
        return jnp.sum(jnp.sum(c, axis=0), axis=0, keepdims=True)

    def unsettled(state):
        it, _, _, cnt_lo = state
        return (it < BISECT_ITERS) & (jnp.max(cnt_lo) > k_eff + PEEL_SLACK)

    def halve(lo, hi, cnt_lo):
        mid = lo + (hi - lo) * 0.5
        cnt = count_ge(mid)
        ge = cnt >= k_eff
        return jnp.where(ge, mid, lo), jnp.where(ge, hi, mid), jnp.where(ge, cnt, cnt_lo)

    def bisect(state):
        it, lo, hi, cnt_lo = state
        return (it + 2,) + halve(*halve(lo, hi, cnt_lo))

    _, lo, _, _ = lax.while_loop(unsettled, bisect, (jnp.int32(0), -bound, bound, count_ge(-bound)))

    def vmin_body(j, v):
        s = s_ref[j]
        return jnp.minimum(v, jnp.min(jnp.where(s >= lo, s, BIG), axis=0, keepdims=True))

    v = lax.fori_loop(1, i + 1, vmin_body, jnp.full((1, BLK), BIG, F32))
    v = jnp.where(v >= BIG, 0.0, v)

    def above(x):
        def body(j, carry):
            cnt, nxt = carry
            s = s_ref[j]
            gt = s > x
            cnt = cnt + jnp.sum(gt.astype(F32).reshape(4, BLK // 32, 8, BLK), axis=1)
            nxt = jnp.minimum(nxt, jnp.min(jnp.where(gt, s, BIG).reshape(4, BLK // 32, 8, BLK), axis=1))
            return cnt, nxt
        cnt, nxt = lax.fori_loop(1, i + 1, body, (jnp.zeros((4, 8, BLK), F32), jnp.full((4, 8, BLK), BIG, F32)))
        return (jnp.sum(jnp.sum(cnt, axis=0), axis=0, keepdims=True),
                jnp.min(jnp.min(nxt, axis=0), axis=0, keepdims=True))

    def peel(state):
        x, cnt_gt, nxt = state
        x = jnp.where(cnt_gt >= k_eff, nxt, x)
        return (x,) + above(x)

    v, cnt_gt, _ = lax.while_loop(lambda s: jnp.max(s[1]) >= k_eff, peel, (v,) + above(v))
    ties_wanted = k_eff - cnt_gt

    acc_ref[...] = jnp.zeros_like(acc_ref)
    tri_incl = (col <= row).astype(BF16)
    aq_heads = [aq_ref[:, hs] * (A_HEAD_DIM ** -0.5) for hs in heads]

    def attend(j, carry, near):
        ties_seen, ms = carry
        s = s_ref[j]
        eq = s == v
        rank = ties_seen + jnp.dot(tri_incl, eq.astype(BF16), preferred_element_type=F32)
        sel = (s > v) | (eq & (rank <= ties_wanted))
        neg = jnp.where(sel, 0.0, -BIG).T
        def logits(h):
            z = lax.dot_general(aq_heads[h], ak_ref[j, :, heads[h]], NT_DIMS, preferred_element_type=F32)
            return z + (bias_ref[h, i - j] + neg if near else neg)

        zs = {h: logits(h) for h in range(DSA_LOOKAHEAD)}
        ms_new = []
        for h in range(A_HEADS):
            if h + DSA_LOOKAHEAD < A_HEADS:
                zs[h + DSA_LOOKAHEAD] = logits(h + DSA_LOOKAHEAD)
            z = zs.pop(h)
            m_new = jnp.maximum(ms[h], jnp.max(z, axis=1, keepdims=True))
            alpha = jnp.exp(ms[h] - m_new)
            p = jnp.exp(z - m_new).astype(BF16)
            ms_new.append(m_new)
            acc_ref[h] = alpha * acc_ref[h] + jnp.dot(p, av1_ref[j, :, heads1[h]], preferred_element_type=F32)
        return rank[BLK - 1:BLK, :], tuple(ms_new)

    init = (jnp.zeros((1, BLK), F32), tuple(jnp.full((BLK, 1), -BIG, F32) for _ in heads))
    first_near = jnp.maximum(i - 1, 0)
    carry = lax.fori_loop(0, first_near, functools.partial(attend, near=False), init)
    lax.fori_loop(first_near, i + 1, functools.partial(attend, near=True), carry)

    valid_q = (lax.broadcasted_iota(jnp.int32, (BLK, 1), 0) + i * BLK) >= LEAD
    outs = []
    for h in range(A_HEADS):
        acc = acc_ref[h]
        outs.append((acc / pltpu.roll(acc, A_HEAD_DIM, 1))[:, :A_HEAD_DIM])
    o_ref[...] = jnp.where(valid_q, jnp.concatenate(outs, axis=1), 0.0).astype(o_ref.dtype)


def _dsa_attention(p0, ikw, bias, nbatch, nb, n_keep):
    rows = p0.shape[0]
    aw = A_HEADS * A_HEAD_DIM
    p0b = p0.reshape(nbatch, nb, BLK, p0.shape[1])
    ikwb = ikw.reshape(nbatch, nb, BLK, ikw.shape[1])
    return pl.pallas_call(
        functools.partial(_dsa_kernel, n_keep=n_keep),
        out_shape=jax.ShapeDtypeStruct((rows, aw), BF16),
        grid=(nbatch, nb),
        in_specs=[pl.BlockSpec((BLK, aw), lambda b, i: (b * nb + i, 0)),
                  pl.BlockSpec((BLK, aw), lambda b, i: (b * nb + i, 3)),
                  pl.BlockSpec((BLK, 128), lambda b, i: (b * nb + i, 0)),
                  pl.BlockSpec((None, nb, BLK, aw), lambda b, i: (b, 0, 0, 1)),
                  pl.BlockSpec((None, nb, BLK, aw), lambda b, i: (b, 0, 0, 2)),
                  pl.BlockSpec((None, nb, BLK, 128), lambda b, i: (b, 0, 0, 0)),
                  pl.BlockSpec((A_HEADS, 2, BLK, BLK), lambda b, i: (0, 0, 0, 0))],
        out_specs=pl.BlockSpec((BLK, aw), lambda b, i: (b * nb + i, 0)),
        scratch_shapes=[pltpu.VMEM((nb, BLK, BLK), F32),
                        pltpu.VMEM((A_HEADS, BLK, 2 * A_HEAD_DIM), F32),
                        pltpu.VMEM((nb, BLK, 2 * aw), BF16)],
        compiler_params=_params("parallel", "arbitrary"),
        name="dsa_attention",
    )(p0, p0, ikw, p0b, p0b, ikwb, bias)


def _retention_kernel(bq_ref, bk_ref, bv_ref, bg_ref, cos_ref, sin_ref, dmat_ref, xi_ref, zeta_ref,
                      gain_ref, o_ref, r_ref, *, g_chunk):
    @pl.when(pl.program_id(1) == 0)
    def _():
        r_ref[...] = jnp.zeros_like(r_ref)

    cosf = cos_ref[...]
    sinf = sin_ref[...]

    def rot(x):
        return x * cosf + pltpu.roll(x, B_QK_DIM // 2, 1) * sinf

    for h in range(B_HEADS):
        ks = slice(h * B_QK_DIM, (h + 1) * B_QK_DIM)
        vs = slice(h * B_V_DIM, (h + 1) * B_V_DIM)
        q = rot(bq_ref[:, ks].astype(F32))
        k = rot(bk_ref[:, ks].astype(F32)) * (B_QK_DIM ** -0.5)
        qb = q.astype(BF16)
        kb = k.astype(BF16)
        v = bv_ref[:, vs]
        inner = lax.dot_general(qb, kb, NT_DIMS, preferred_element_type=F32) * dmat_ref[h]
        r_old = r_ref[h]
        o = (jnp.dot(inner.astype(BF16), v, preferred_element_type=F32)
             + jnp.dot(qb, r_old.astype(BF16), preferred_element_type=F32) * xi_ref[h])
        kz = (k * zeta_ref[h]).astype(BF16)
        r_ref[h] = r_old * g_chunk[h] + lax.dot_general(kz, v, TN_DIMS, preferred_element_type=F32)
        mu = jnp.mean(o, axis=-1, keepdims=True)
        oc = o - mu
        var = jnp.mean(oc * oc, axis=-1, keepdims=True)
        rn = oc * lax.rsqrt(var + EPS) * gain_ref[:, vs]
        gate = bg_ref[:, vs].astype(F32)
        o_ref[:, vs] = (rn * (gate / (1.0 + jnp.exp(-gate)))).astype(o_ref.dtype)


def _retention(p0, gn_gain, nbatch, nb):
    rows = p0.shape[0]
    qkw, vw = B_HEADS * B_QK_DIM, B_HEADS * B_V_DIM
    frame = nb * BLK
    half = B_QK_DIM // 2
    pos = (jnp.arange(frame) - LEAD).astype(F32)
    inv = 1.0 / (ROPE_BASE ** (jnp.arange(half, dtype=F32) / half))
    ang = pos[:, None] * inv[None, :]
    cosf = jnp.concatenate([jnp.cos(ang), jnp.cos(ang)], axis=-1)
    sinf = jnp.concatenate([-jnp.sin(ang), jnp.sin(ang)], axis=-1)
    lg = jnp.log(1.0 - 2.0 ** (-5.0 - jnp.arange(B_HEADS, dtype=F32)))
    n = jnp.arange(BLK, dtype=F32)
    diff = n[:, None] - n[None, :]
    dmat = jnp.where(diff[None] >= 0, jnp.exp(jnp.maximum(diff, 0.0)[None] * lg[:, None, None]), 0.0)
    xi = jnp.broadcast_to(jnp.exp((n[None, :] + 1.0) * lg[:, None])[..., None], (B_HEADS, BLK, B_V_DIM))
    zeta = jnp.broadcast_to(jnp.exp((BLK - 1.0 - n[None, :]) * lg[:, None])[..., None],
                            (B_HEADS, BLK, B_QK_DIM))
    g_chunk = tuple(float(math.exp(BLK * math.log(1.0 - 2.0 ** (-5.0 - h)))) for h in range(B_HEADS))
    const = lambda shape: pl.BlockSpec(shape, lambda b, i: (0,) * len(shape))
    return pl.pallas_call(
        functools.partial(_retention_kernel, g_chunk=g_chunk),
        out_shape=jax.ShapeDtypeStruct((rows, vw), BF16),
        grid=(nbatch, nb),
        in_specs=[pl.BlockSpec((BLK, qkw), lambda b, i: (b * nb + i, 4)),
                  pl.BlockSpec((BLK, qkw), lambda b, i: (b * nb + i, 5)),
                  pl.BlockSpec((BLK, vw), lambda b, i: (b * nb + i, 3)),
                  pl.BlockSpec((BLK, vw), lambda b, i: (b * nb + i, 4)),
                  pl.BlockSpec((BLK, B_QK_DIM), lambda b, i: (i, 0)),
                  pl.BlockSpec((BLK, B_QK_DIM), lambda b, i: (i, 0)),
                  const((B_HEADS, BLK, BLK)),
                  const((B_HEADS, BLK, B_V_DIM)),
                  const((B_HEADS, BLK, B_QK_DIM)),
                  const((1, vw))],
        out_specs=pl.BlockSpec((BLK, vw), lambda b, i: (b * nb + i, 0)),
        scratch_shapes=[pltpu.VMEM((B_HEADS, B_QK_DIM, B_V_DIM), F32)],
        compiler_params=_params("parallel", "arbitrary"),
        name="retention",
    )(p0, p0, p0, p0, cosf, sinf, dmat, xi, zeta, gn_gain.reshape(1, vw).astype(F32))


def _proj_residual_kernel(*refs, n_pairs, final):
    x = refs[2 * n_pairs][...]
    for t in range(n_pairs):
        x = x + jnp.dot(refs[2 * t][...], refs[2 * t + 1][...], preferred_element_type=F32)
    if final:
        ng_ref, o_ref = refs[2 * n_pairs + 1:]
        ms = jnp.mean(x * x, axis=-1, keepdims=True)
        o_ref[...] = x * lax.rsqrt(ms + EPS) * ng_ref[...]
    else:
        refs[2 * n_pairs + 1][...] = x


def _proj_residual(pairs, h, tm, final=None):
    rows, d = h.shape
    in_specs, args = [], []
    for a, w in pairs:
        in_specs += [pl.BlockSpec((tm, a.shape[1]), lambda i: (i, 0)),
                     pl.BlockSpec(w.shape, lambda i: (0, 0))]
        args += [a, w]
    in_specs.append(pl.BlockSpec((tm, d), lambda i: (i, 0)))
    args.append(h)
    if final is None:
        out_shape = jax.ShapeDtypeStruct((rows, d), F32)
        out_spec = pl.BlockSpec((tm, d), lambda i: (i, 0))
        sem = "parallel"
    else:
        gain, nbatch, nb = final
        assert tm == BLK
        in_specs.append(pl.BlockSpec((1, d), lambda i: (0, 0)))
        args.append(gain.reshape(1, d).astype(F32))
        out_shape = jax.ShapeDtypeStruct((nbatch, (nb - 1) * BLK, d), F32)
        out_spec = pl.BlockSpec((None, BLK, d), lambda i: (i // nb, jnp.maximum(i % nb - 1, 0), 0))
        sem = "arbitrary"
    return pl.pallas_call(
        functools.partial(_proj_residual_kernel, n_pairs=len(pairs), final=final is not None),
        out_shape=out_shape,
        grid=(rows // tm,),
        in_specs=in_specs,
        out_specs=out_spec,
        compiler_params=_params(sem),
        name="proj_residual",
    )(*args)


GATE_COLS = 256


def _norm_gate_kernel(x_ref, gn_ref, wu_ref, wg_ref, cw_ref, cb_ref, o_ref, halo_ref):
    @pl.when(pl.program_id(0) == 0)
    def _():
        halo_ref[...] = jnp.zeros_like(halo_ref)

    x = x_ref[...]
    ms = jnp.mean(x * x, axis=-1, keepdims=True)
    xn = (x * lax.rsqrt(ms + EPS) * gn_ref[...]).astype(BF16)
    tm = x.shape[0]
    row8 = lax.broadcasted_iota(jnp.int32, (8, GATE_COLS), 0)
    for c in range(o_ref.shape[1] // GATE_COLS):
        cs = slice(c * GATE_COLS, (c + 1) * GATE_COLS)
        u = jnp.dot(xn, wu_ref[:, cs], preferred_element_type=F32)
        g = jnp.dot(xn, wg_ref[:, cs], preferred_element_type=F32)
        prev = halo_ref[:, cs]
        halo_ref[:, cs] = g[tm - 8:, :]
        g1 = pltpu.roll(g, 1, 0)
        g2 = pltpu.roll(g, 2, 0)
        g1 = jnp.concatenate([jnp.where(row8 >= 1, g1[:8], prev[7:8]), g1[8:]], axis=0)
        g2 = jnp.concatenate([jnp.where(row8 >= 2, g2[:8], jnp.where(row8 == 1, prev[7:8], prev[6:7])),
                              g2[8:]], axis=0)
        gc = g2 * cw_ref[0:1, cs] + g1 * cw_ref[1:2, cs] + g * cw_ref[2:3, cs] + cb_ref[:, cs]
        o_ref[:, cs] = ((gc / (1.0 + jnp.exp(-gc))) * u).astype(o_ref.dtype)


def _norm_gated_up(x, gain, w_up, w_gate, conv_w, conv_b, tm):
    rows, d = x.shape
    dff = w_up.shape[1]
    assert dff % GATE_COLS == 0
    const = lambda shape: pl.BlockSpec(shape, lambda i: (0, 0))
    return pl.pallas_call(
        _norm_gate_kernel,
        out_shape=jax.ShapeDtypeStruct((rows, dff), BF16),
        grid=(rows // tm,),
        in_specs=[pl.BlockSpec((tm, d), lambda i: (i, 0)),
                  const((1, d)), const((d, dff)), const((d, dff)), const((CONV_WIDTH, dff)), const((1, dff))],
        out_specs=pl.BlockSpec((tm, dff), lambda i: (i, 0)),
        scratch_shapes=[pltpu.VMEM((8, dff), F32)],
        compiler_params=_params("arbitrary"),
        name="norm_gated_up",
    )(x, gain.reshape(1, d), w_up.astype(BF16), w_gate.astype(BF16), conv_w.astype(F32),
      conv_b.reshape(1, dff).astype(F32))


SB_GROUP = 4
SB_DEAD = 110.0


def _stick_breaking_kernel(q_ref, k_ref, v_ref, o_ref, acc_ref):
    i = pl.program_id(2)
    row = lax.broadcasted_iota(jnp.int32, (BLK, BLK), 0)
    col = lax.broadcasted_iota(jnp.int32, (BLK, BLK), 1)
    tri_after = (row > col).astype(BF16)
    scale = C_HEAD_DIM ** -0.5

    heads = [slice(h * C_HEAD_DIM, (h + 1) * C_HEAD_DIM) for h in range(SB_GROUP)]
    q_heads = [q_ref[:, hs] * scale for hs in heads]
    acc_ref[...] = jnp.zeros_like(acc_ref)

    def tiles(blocks, carries):
        keys = [(b, h) for b in range(len(blocks)) for h in range(SB_GROUP)]
        zs = {(b, h): lax.dot_general(q_heads[h], k_ref[blocks[b][0], :, heads[h]], NT_DIMS,
                                      preferred_element_type=F32) for b, h in keys}
        log_beta, rests, sums = {}, {}, {}
        for b, h in keys:
            z, mask = zs[b, h], blocks[b][1]
            sp = jnp.maximum(z, 0.0) + jnp.log(1.0 + jnp.exp2(jnp.abs(z) * -LOG2E))
            if mask is not None:
                sp = jnp.where(mask, sp, 0.0)
            log_beta[b, h] = z - sp
            sums[b, h] = jnp.sum(sp, axis=1, keepdims=True)
            rests[b, h] = jnp.dot(sp.astype(BF16), tri_after, preferred_element_type=F32)
        run = list(carries)
        for b, h in keys:
            j, mask = blocks[b]
            a = jnp.exp2((log_beta[b, h] - (rests[b, h] + run[h])) * LOG2E)
            if mask is not None:
                a = jnp.where(mask, a, 0.0)
            acc_ref[h] += jnp.dot(a.astype(BF16), v_ref[j, :, heads[h]], preferred_element_type=F32)
            run[h] = run[h] + sums[b, h]
        return tuple(run)

    def alive(carries):
        return jnp.min(functools.reduce(jnp.minimum, carries)) < SB_DEAD

    qf = row + i * BLK
    kf = col + i * BLK
    carries = tiles([(i, (kf < qf) & (kf >= LEAD))], tuple(jnp.zeros((BLK, 1), F32) for _ in heads))
    carries = lax.cond((i >= 2) & alive(carries), lambda c: tiles([(i - 1, None)], c), lambda c: c, carries)
    n_rest = jnp.maximum(i - 2, 0)
    n_pairs = jnp.right_shift(n_rest, 1)
    _, carries = lax.while_loop(
        lambda s: (s[0] < n_pairs) & alive(s[1]),
        lambda s: (s[0] + 1, tiles([(i - 2 - 2 * s[0], None), (i - 3 - 2 * s[0], None)], s[1])),
        (jnp.int32(0), carries))
    carries = lax.cond(((n_rest & 1) == 1) & alive(carries), lambda c: tiles([(1, None)], c), lambda c: c,
                       carries)

    @pl.when((i > 0) & alive(carries))
    def _():
        tiles([(0, col >= LEAD)], carries)

    o_ref[...] = jnp.concatenate([acc_ref[h] for h in range(SB_GROUP)], axis=1).astype(o_ref.dtype)


def _stick_breaking(p1, nbatch, nb):
    rows = p1.shape[0]
    cw = C_HEADS * C_HEAD_DIM
    gw = SB_GROUP * C_HEAD_DIM
    ngroups = C_HEADS // SB_GROUP
    p1b = p1.reshape(nbatch, nb, BLK, p1.shape[1])
    return pl.pallas_call(
        _stick_breaking_kernel,
        out_shape=jax.ShapeDtypeStruct((rows, cw), BF16),
        grid=(nbatch, ngroups, nb),
        in_specs=[pl.BlockSpec((BLK, gw), lambda b, g, i: (b * nb + i, g)),
                  pl.BlockSpec((None, nb, BLK, gw), lambda b, g, i: (b, 0, 0, ngroups + g)),
                  pl.BlockSpec((None, nb, BLK, gw), lambda b, g, i: (b, 0, 0, 2 * ngroups + g))],
        out_specs=pl.BlockSpec((BLK, gw), lambda b, g, i: (b * nb + i, g)),
        scratch_shapes=[pltpu.VMEM((SB_GROUP, BLK, C_HEAD_DIM), F32)],
        compiler_params=_params("parallel", "parallel", "arbitrary"),
        name="stick_breaking",
    )(p1, p1b, p1b)


def _pick_tile(rows, pref):
    t = pref
    while rows % t:
        t //= 2
    return t


def _pick_cols(n, cap=2816):
    return max(t for t in range(128, min(n, cap) + 1, 128) if n % t == 0)


def kernel(x, meta_tokens, rel_bias, norm_mix, norm_ffn, norm_final, even_w_in, even_gn_gain, even_w_out, odd_w_in, odd_w_out, ffn_w_up, ffn_w_gate, ffn_conv_w, ffn_conv_b, ffn_w_down):
    nbatch, seq, d = x.shape
    assert seq % BLK == 0
    nb = seq // BLK + 1
    rows = nbatch * nb * BLK
    n_keep = min(TOPK_MAX, seq // 4)
    assert n_keep >= N_META
    depth = norm_mix.shape[0]
    tm = _pick_tile(rows, 1024)

    meta = jnp.broadcast_to(meta_tokens[None].astype(x.dtype), (nbatch, N_META, d))
    h = jnp.concatenate([jnp.zeros((nbatch, LEAD, d), x.dtype), meta, x], axis=1).reshape(rows, d)

    aw = A_HEADS * A_HEAD_DIM
    qkw, vw = B_HEADS * B_QK_DIM, B_HEADS * B_V_DIM
    cw = C_HEADS * C_HEAD_DIM
    bias = _bias_tiles(rel_bias)

    for l in range(depth):
        j = l // 2
        if l % 2 == 0:
            w = even_w_in[j]
            o_iq, o_ik, o_iw, o_bq = 3 * aw, 4 * aw, 4 * aw + IDX_DIM, 4 * aw + IDX_DIM + IDX_HEADS
            w_main = jnp.concatenate([w[:, :4 * aw], w[:, o_bq:]], axis=1).astype(BF16)
            w_idx = jnp.concatenate([w[:, o_ik:o_bq], jnp.zeros((d, 128 - IDX_DIM - IDX_HEADS), w.dtype)],
                                    axis=1).astype(BF16)
            p0 = _norm_matmul(h, norm_mix[l], w_main, tm, _pick_cols(w_main.shape[1]))
            ikw = _norm_matmul(h, norm_mix[l], w_idx, tm, 128)
            a_out = _dsa_attention(p0, ikw, bias, nbatch, nb, n_keep)
            r_out = _retention(p0, even_gn_gain[j], nbatch, nb)
            w_out = even_w_out[j].astype(BF16)
            h = _proj_residual([(a_out, w_out[:aw]), (r_out, w_out[aw:])], h, _pick_tile(rows, 512))
        else:
            p1 = _norm_matmul(h, norm_mix[l], odd_w_in[j].astype(BF16), tm, _pick_cols(odd_w_in.shape[2]))
            s_out = _stick_breaking(p1, nbatch, nb)
            h = _proj_residual([(s_out, odd_w_out[j].astype(BF16))], h, _pick_tile(rows, 512))
        act = _norm_gated_up(h, norm_ffn[l], ffn_w_up[l], ffn_w_gate[l], ffn_conv_w[l], ffn_conv_b[l], tm)
        w_down = ffn_w_down[l].astype(BF16)
        if l + 1 < depth:
            h = _proj_residual([(act, w_down)], h, _pick_tile(rows, 512))
        else:
            h = _proj_residual([(act, w_down)], h, BLK, final=(norm_final, nbatch, nb))
    return h
```

```python
import functools
import math

import jax
import jax.numpy as jnp
import numpy as np
from jax import lax
from jax.experimental import pallas as pl
from jax.experimental.pallas import tpu as pltpu

N_META = 16
BLK = 256
LEAD = BLK - N_META
A_HEADS, A_HEAD_DIM = 8, 64
IDX_HEADS, IDX_DIM = 8, 64
TOPK_MAX = 256
N_BUCKETS, MAX_DISTANCE = 32, 128
B_HEADS, B_QK_DIM, B_V_DIM = 4, 128, 256
ROPE_BASE = 10000.0
C_HEADS, C_HEAD_DIM = 16, 64
CONV_WIDTH = 3
EPS = 1e-6
BIG = 1e30
LOG2E = 1.4426950408889634
BISECT_ITERS = 40
PEEL_SLACK = 3.0
DSA_LOOKAHEAD = 8
VMEM_LIMIT_BYTES = 56 * 1024 * 1024

F32 = jnp.float32
BF16 = jnp.bfloat16
NT_DIMS = (((1,), (1,)), ((), ()))
TN_DIMS = (((0,), (0,)), ((), ()))


def _params(*sem):
    return pltpu.CompilerParams(dimension_semantics=sem, vmem_limit_bytes=VMEM_LIMIT_BYTES)


def _norm_matmul_kernel(x_ref, g_ref, w_ref, o_ref, xn_ref):
    @pl.when(pl.program_id(1) == 0)
    def _():
        x = x_ref[...]
        ms = jnp.mean(x * x, axis=-1, keepdims=True)
        xn_ref[...] = (x * lax.rsqrt(ms + EPS) * g_ref[...]).astype(BF16)

    o_ref[...] = jnp.dot(xn_ref[...], w_ref[...], preferred_element_type=F32).astype(o_ref.dtype)


def _norm_matmul(x, gain, w, tm, tn):
    rows, d = x.shape
    n = w.shape[1]
    return pl.pallas_call(
        _norm_matmul_kernel,
        out_shape=jax.ShapeDtypeStruct((rows, n), BF16),
        grid=(rows // tm, n // tn),
        in_specs=[pl.BlockSpec((tm, d), lambda i, j: (i, 0)),
                  pl.BlockSpec((1, d), lambda i, j: (0, 0)),
                  pl.BlockSpec((d, tn), lambda i, j: (0, j))],
        out_specs=pl.BlockSpec((tm, tn), lambda i, j: (i, j)),
        scratch_shapes=[pltpu.VMEM((tm, d), BF16)],
        compiler_params=_params("parallel", "arbitrary"),
        name="norm_matmul",
    )(x, gain.reshape(1, d), w)


def _bucket_tiles():
    q = np.arange(BLK)[:, None]
    k = np.arange(BLK)[None, :]
    tiles = []
    for t in range(2):
        n = np.maximum(q - k + t * BLK, 0)
        max_exact = N_BUCKETS // 2
        large = max_exact + (np.log(np.maximum(n, 1).astype(np.float32) / max_exact)
                             / math.log(MAX_DISTANCE / max_exact)
                             * (N_BUCKETS - max_exact)).astype(np.int32)
        large = np.minimum(large, N_BUCKETS - 1)
        tiles.append(np.where(n < max_exact, n, large).astype(np.int32))
    return np.stack(tiles)


def _bias_tiles_kernel(rb_ref, idx_ref, o_ref):
    h = pl.program_id(0)
    far = rb_ref[N_BUCKETS - 1, h]
    for t in range(2):
        idx = idx_ref[t]
        acc = jnp.zeros((BLK, BLK), F32)
        for b in range(N_BUCKETS):
            acc = jnp.where(idx == b, rb_ref[b, h] - far, acc)
        o_ref[t] = acc


def _bias_tiles(rel_bias):
    assert BLK + 1 >= MAX_DISTANCE
    return pl.pallas_call(
        _bias_tiles_kernel,
        out_shape=jax.ShapeDtypeStruct((A_HEADS, 2, BLK, BLK), F32),
        grid=(A_HEADS,),
        in_specs=[pl.BlockSpec(memory_space=pltpu.SMEM),
                  pl.BlockSpec((2, BLK, BLK), lambda h: (0, 0, 0))],
        out_specs=pl.BlockSpec((None, 2, BLK, BLK), lambda h: (h, 0, 0, 0)),
        compiler_params=_params("parallel"),
        name="t5_bias_tiles",
    )(rel_bias.astype(F32), jnp.asarray(_bucket_tiles()))


def _dsa_kernel(aq_ref, iq_ref, ikwq_ref, ak_ref, av_ref, ikw_ref, bias_ref, o_ref,
                s_ref, acc_ref, av1_ref, *, n_keep):
    i = pl.program_id(1)
    row = lax.broadcasted_iota(jnp.int32, (BLK, BLK), 0)
    col = lax.broadcasted_iota(jnp.int32, (BLK, BLK), 1)
    qf = col + i * BLK
    heads = [slice(h * A_HEAD_DIM, (h + 1) * A_HEAD_DIM) for h in range(A_HEADS)]
    heads1 = [slice(h * 2 * A_HEAD_DIM, (h + 1) * 2 * A_HEAD_DIM) for h in range(A_HEADS)]

    @pl.when(i == 0)
    def _():
        def fill(j, _):
            ones = jnp.ones((BLK, A_HEAD_DIM), BF16)
            av1_ref[j] = jnp.concatenate([x for hs in heads for x in (av_ref[j, :, hs], ones)], axis=1)
            return 0
        lax.fori_loop(0, av_ref.shape[0], fill, 0)

    s_ref[0] = jnp.where((row >= LEAD) & (row <= qf), BIG, -BIG)
    eye = (lax.broadcasted_iota(jnp.int32, (128, 128), 0)
           == lax.broadcasted_iota(jnp.int32, (128, 128), 1)).astype(BF16)
    ikwq_t = lax.dot_general(eye, ikwq_ref[...], NT_DIMS, preferred_element_type=F32)
    w_scale = (IDX_DIM ** -0.5) * (IDX_HEADS ** -0.5)
    w_rows = [ikwq_t[IDX_DIM + h:IDX_DIM + h + 1, :] * w_scale for h in range(IDX_HEADS)]
    iq_heads = [iq_ref[:, h * IDX_DIM:(h + 1) * IDX_DIM] for h in range(IDX_HEADS)]

    def score_block(j, mabs):
        ik_j = ikw_ref[j, :, :IDX_DIM]
        acc = jnp.zeros((BLK, BLK), F32)
        for h in range(IDX_HEADS):
            st = lax.dot_general(ik_j, iq_heads[h], NT_DIMS, preferred_element_type=F32)
            acc = acc + jnp.maximum(st, 0.0) * w_rows[h]
        causal = (row + j * BLK) <= qf
        s_ref[j] = jnp.where(causal, acc, -BIG)
        return jnp.maximum(mabs, jnp.max(jnp.where(causal, jnp.abs(acc), 0.0), axis=0, keepdims=True))

    mabs = lax.fori_loop(1, i + 1, score_block, jnp.zeros((1, BLK), F32))

    k_eff = float(n_keep - N_META)
    bound = mabs * 1.000001 + 1e-30

    def count_ge(thr):
        def body(j, c):
            hit = (s_ref[j] >= thr).astype(F32).reshape(4, BLK // 32, 8, BLK)
            return c + jnp.sum(hit, axis=1)
        c = lax.fori_loop(1, i + 1, body, jnp.zeros((4, 8, BLK), F32))
        return jnp.sum(jnp.sum(c, axis=0), axis=0, keepdims=True)

    def unsettled(state):
        it, _, _, cnt_lo = state
        return (it < BISECT_ITERS) & (jnp.max(cnt_lo) > k_eff + PEEL_SLACK)

    def halve(lo, hi, cnt_lo):
        mid = lo + (hi - lo) * 0.5
        cnt = count_ge(mid)
        ge = cnt >= k_eff
        return jnp.where(ge, mid, lo), jnp.where(ge, hi, mid), jnp.where(ge, cnt, cnt_lo)

    def bisect(state):
        it, lo, hi, cnt_lo = state
        return (it + 2,) + halve(*halve(lo, hi, cnt_lo))

    _, lo, _, _ = lax.while_loop(unsettled, bisect, (jnp.int32(0), -bound, bound, count_ge(-bound)))

    def vmin_body(j, v):
        s = s_ref[j]
        return jnp.minimum(v, jnp.min(jnp.where(s >= lo, s, BIG), axis=0, keepdims=True))

    v = lax.fori_loop(1, i + 1, vmin_body, jnp.full((1, BLK), BIG, F32))
    v = jnp.where(v >= BIG, 0.0, v)

    def above(x):
        def body(j, carry):
            cnt, nxt = carry
            s = s_ref[j]
            gt = s > x
            cnt = cnt + jnp.sum(gt.astype(F32).reshape(4, BLK // 32, 8, BLK), axis=1)
            nxt = jnp.minimum(nxt, jnp.min(jnp.where(gt, s, BIG).reshape(4, BLK // 32, 8, BLK), axis=1))
            return cnt, nxt
        cnt, nxt = lax.fori_loop(1, i + 1, body, (jnp.zeros((4, 8, BLK), F32), jnp.full((4, 8, BLK), BIG, F32)))
        return (jnp.sum(jnp.sum(cnt, axis=0), axis=0, keepdims=True),
                jnp.min(jnp.min(nxt, axis=0), axis=0, keepdims=True))

    def peel(state):
        x, cnt_gt, nxt = state
        x = jnp.where(cnt_gt >= k_eff, nxt, x)
        return (x,) + above(x)

    v, cnt_gt, _ = lax.while_loop(lambda s: jnp.max(s[1]) >= k_eff, peel, (v,) + above(v))
    ties_wanted = k_eff - cnt_gt

    acc_ref[...] = jnp.zeros_like(acc_ref)
    tri_incl = (col <= row).astype(BF16)
    aq_heads = [aq_ref[:, hs] * (A_HEAD_DIM ** -0.5) for hs in heads]

    def attend(js, carry, near):
        ties_seen, ms = carry
        negs = []
        for j in js:
            s = s_ref[j]
            eq = s == v
            rank = ties_seen + jnp.dot(tri_incl, eq.astype(BF16), preferred_element_type=F32)
            sel = (s > v) | (eq & (rank <= ties_wanted))
            negs.append(jnp.where(sel, 0.0, -BIG).T)
            ties_seen = rank[BLK - 1:BLK, :]

        def logits(t, h):
            z = lax.dot_general(aq_heads[h], ak_ref[js[t], :, heads[h]], NT_DIMS, preferred_element_type=F32)
            return z + (bias_ref[h, i - js[t]] + negs[t] if near else negs[t])

        zs = {(t, h): logits(t, h) for h in range(A_HEADS) for t in range(len(js))}
        ms_new = []
        for h in range(A_HEADS):
            zh = [zs.pop((t, h)) for t in range(len(js))]
            m_new = jnp.maximum(ms[h], jnp.max(functools.reduce(jnp.maximum, zh), axis=1, keepdims=True))
            alpha = jnp.exp(ms[h] - m_new)
            ms_new.append(m_new)
            acc = alpha * acc_ref[h]
            for t, z in enumerate(zh):
                acc = acc + jnp.dot(jnp.exp(z - m_new).astype(BF16), av1_ref[js[t], :, heads1[h]],
                                    preferred_element_type=F32)
            acc_ref[h] = acc
        return ties_seen, tuple(ms_new)

    init = (jnp.zeros((1, BLK), F32), tuple(jnp.full((BLK, 1), -BIG, F32) for _ in heads))
    n_far = jnp.maximum(i - 1, 0)
    carry = lax.fori_loop(0, jnp.right_shift(n_far, 1),
                          lambda t, c: attend((2 * t, 2 * t + 1), c, False), init)
    carry = lax.cond((n_far & 1) == 1, lambda c: attend((n_far - 1,), c, False), lambda c: c, carry)
    lax.fori_loop(n_far, i + 1, lambda j, c: attend((j,), c, True), carry)

    valid_q = (lax.broadcasted_iota(jnp.int32, (BLK, 1), 0) + i * BLK) >= LEAD
    outs = []
    for h in range(A_HEADS):
        acc = acc_ref[h]
        outs.append((acc / pltpu.roll(acc, A_HEAD_DIM, 1))[:, :A_HEAD_DIM])
    o_ref[...] = jnp.where(valid_q, jnp.concatenate(outs, axis=1), 0.0).astype(o_ref.dtype)


def _dsa_attention(p0, ikw, bias, nbatch, nb, n_keep):
    rows = p0.shape[0]
    aw = A_HEADS * A_HEAD_DIM
    p0b = p0.reshape(nbatch, nb, BLK, p0.shape[1])
    ikwb = ikw.reshape(nbatch, nb, BLK, ikw.shape[1])
    return pl.pallas_call(
        functools.partial(_dsa_kernel, n_keep=n_keep),
        out_shape=jax.ShapeDtypeStruct((rows, aw), BF16),
        grid=(nbatch, nb),
        in_specs=[pl.BlockSpec((BLK, aw), lambda b, i: (b * nb + i, 0)),
                  pl.BlockSpec((BLK, aw), lambda b, i: (b * nb + i, 3)),
                  pl.BlockSpec((BLK, 128), lambda b, i: (b * nb + i, 0)),
                  pl.BlockSpec((None, nb, BLK, aw), lambda b, i: (b, 0, 0, 1)),
                  pl.BlockSpec((None, nb, BLK, aw), lambda b, i: (b, 0, 0, 2)),
                  pl.BlockSpec((None, nb, BLK, 128), lambda b, i: (b, 0, 0, 0)),
                  pl.BlockSpec((A_HEADS, 2, BLK, BLK), lambda b, i: (0, 0, 0, 0))],
        out_specs=pl.BlockSpec((BLK, aw), lambda b, i: (b * nb + i, 0)),
        scratch_shapes=[pltpu.VMEM((nb, BLK, BLK), F32),
                        pltpu.VMEM((A_HEADS, BLK, 2 * A_HEAD_DIM), F32),
                        pltpu.VMEM((nb, BLK, 2 * aw), BF16)],
        compiler_params=_params("parallel", "arbitrary"),
        name="dsa_attention",
    )(p0, p0, ikw, p0b, p0b, ikwb, bias)


def _retention_kernel(bq_ref, bk_ref, bv_ref, bg_ref, cos_ref, sin_ref, dmat_ref, xi_ref, zeta_ref,
                      gain_ref, o_ref, r_ref, *, g_chunk):
    @pl.when(pl.program_id(1) == 0)
    def _():
        r_ref[...] = jnp.zeros_like(r_ref)

    cosf = cos_ref[...]
    sinf = sin_ref[...]

    def rot(x):
        return x * cosf + pltpu.roll(x, B_QK_DIM // 2, 1) * sinf

    for h in range(B_HEADS):
        ks = slice(h * B_QK_DIM, (h + 1) * B_QK_DIM)
        vs = slice(h * B_V_DIM, (h + 1) * B_V_DIM)
        q = rot(bq_ref[:, ks].astype(F32))
        k = rot(bk_ref[:, ks].astype(F32)) * (B_QK_DIM ** -0.5)
        qb = q.astype(BF16)
        kb = k.astype(BF16)
        v = bv_ref[:, vs]
        inner = lax.dot_general(qb, kb, NT_DIMS, preferred_element_type=F32) * dmat_ref[h]
        r_old = r_ref[h]
        o = (jnp.dot(inner.astype(BF16), v, preferred_element_type=F32)
             + jnp.dot(qb, r_old.astype(BF16), preferred_element_type=F32) * xi_ref[h])
        kz = (k * zeta_ref[h]).astype(BF16)
        r_ref[h] = r_old * g_chunk[h] + lax.dot_general(kz, v, TN_DIMS, preferred_element_type=F32)
        mu = jnp.mean(o, axis=-1, keepdims=True)
        oc = o - mu
        var = jnp.mean(oc * oc, axis=-1, keepdims=True)
        rn = oc * lax.rsqrt(var + EPS) * gain_ref[:, vs]
        gate = bg_ref[:, vs].astype(F32)
        o_ref[:, vs] = (rn * (gate / (1.0 + jnp.exp(-gate)))).astype(o_ref.dtype)


def _retention(p0, gn_gain, nbatch, nb):
    rows = p0.shape[0]
    qkw, vw = B_HEADS * B_QK_DIM, B_HEADS * B_V_DIM
    frame = nb * BLK
    half = B_QK_DIM // 2
    pos = (jnp.arange(frame) - LEAD).astype(F32)
    inv = 1.0 / (ROPE_BASE ** (jnp.arange(half, dtype=F32) / half))
    ang = pos[:, None] * inv[None, :]
    cosf = jnp.concatenate([jnp.cos(ang), jnp.cos(ang)], axis=-1)
    sinf = jnp.concatenate([-jnp.sin(ang), jnp.sin(ang)], axis=-1)
    lg = jnp.log(1.0 - 2.0 ** (-5.0 - jnp.arange(B_HEADS, dtype=F32)))
    n = jnp.arange(BLK, dtype=F32)
    diff = n[:, None] - n[None, :]
    dmat = jnp.where(diff[None] >= 0, jnp.exp(jnp.maximum(diff, 0.0)[None] * lg[:, None, None]), 0.0)
    xi = jnp.broadcast_to(jnp.exp((n[None, :] + 1.0) * lg[:, None])[..., None], (B_HEADS, BLK, B_V_DIM))
    zeta = jnp.broadcast_to(jnp.exp((BLK - 1.0 - n[None, :]) * lg[:, None])[..., None],
                            (B_HEADS, BLK, B_QK_DIM))
    g_chunk = tuple(float(math.exp(BLK * math.log(1.0 - 2.0 ** (-5.0 - h)))) for h in range(B_HEADS))
    const = lambda shape: pl.BlockSpec(shape, lambda b, i: (0,) * len(shape))
    return pl.pallas_call(
        functools.partial(_retention_kernel, g_chunk=g_chunk),
        out_shape=jax.ShapeDtypeStruct((rows, vw), BF16),
        grid=(nbatch, nb),
        in_specs=[pl.BlockSpec((BLK, qkw), lambda b, i: (b * nb + i, 4)),
                  pl.BlockSpec((BLK, qkw), lambda b, i: (b * nb + i, 5)),
                  pl.BlockSpec((BLK, vw), lambda b, i: (b * nb + i, 3)),
                  pl.BlockSpec((BLK, vw), lambda b, i: (b * nb + i, 4)),
                  pl.BlockSpec((BLK, B_QK_DIM), lambda b, i: (i, 0)),
                  pl.BlockSpec((BLK, B_QK_DIM), lambda b, i: (i, 0)),
                  const((B_HEADS, BLK, BLK)),
                  const((B_HEADS, BLK, B_V_DIM)),
                  const((B_HEADS, BLK, B_QK_DIM)),
                  const((1, vw))],
        out_specs=pl.BlockSpec((BLK, vw), lambda b, i: (b * nb + i, 0)),
        scratch_shapes=[pltpu.VMEM((B_HEADS, B_QK_DIM, B_V_DIM), F32)],
        compiler_params=_params("parallel", "arbitrary"),
        name="retention",
    )(p0, p0, p0, p0, cosf, sinf, dmat, xi, zeta, gn_gain.reshape(1, vw).astype(F32))


def _proj_residual_kernel(*refs, n_pairs, final):
    x = refs[2 * n_pairs][...]
    for t in range(n_pairs):
        x = x + jnp.dot(refs[2 * t][...], refs[2 * t + 1][...], preferred_element_type=F32)
    if final:
        ng_ref, o_ref = refs[2 * n_pairs + 1:]
        ms = jnp.mean(x * x, axis=-1, keepdims=True)
        o_ref[...] = x * lax.rsqrt(ms + EPS) * ng_ref[...]
    else:
        refs[2 * n_pairs + 1][...] = x


def _proj_residual(pairs, h, tm, final=None):
    rows, d = h.shape
    in_specs, args = [], []
    for a, w in pairs:
        in_specs += [pl.BlockSpec((tm, a.shape[1]), lambda i: (i, 0)),
                     pl.BlockSpec(w.shape, lambda i: (0, 0))]
        args += [a, w]
    in_specs.append(pl.BlockSpec((tm, d), lambda i: (i, 0)))
    args.append(h)
    if final is None:
        out_shape = jax.ShapeDtypeStruct((rows, d), F32)
        out_spec = pl.BlockSpec((tm, d), lambda i: (i, 0))
        sem = "parallel"
    else:
        gain, nbatch, nb = final
        assert tm == BLK
        in_specs.append(pl.BlockSpec((1, d), lambda i: (0, 0)))
        args.append(gain.reshape(1, d).astype(F32))
        out_shape = jax.ShapeDtypeStruct((nbatch, (nb - 1) * BLK, d), F32)
        out_spec = pl.BlockSpec((None, BLK, d), lambda i: (i // nb, jnp.maximum(i % nb - 1, 0), 0))
        sem = "arbitrary"
    return pl.pallas_call(
        functools.partial(_proj_residual_kernel, n_pairs=len(pairs), final=final is not None),
        out_shape=out_shape,
        grid=(rows // tm,),
        in_specs=in_specs,
        out_specs=out_spec,
        compiler_params=_params(sem),
        name="proj_residual",
    )(*args)


GATE_COLS = 256


def _norm_gate_kernel(x_ref, gn_ref, wu_ref, wg_ref, cw_ref, cb_ref, o_ref, halo_ref):
    @pl.when(pl.program_id(0) == 0)
    def _():
        halo_ref[...] = jnp.zeros_like(halo_ref)

    x = x_ref[...]
    ms = jnp.mean(x * x, axis=-1, keepdims=True)
    xn = (x * lax.rsqrt(ms + EPS) * gn_ref[...]).astype(BF16)
    tm = x.shape[0]
    row8 = lax.broadcasted_iota(jnp.int32, (8, GATE_COLS), 0)
    for c in range(o_ref.shape[1] // GATE_COLS):
        cs = slice(c * GATE_COLS, (c + 1) * GATE_COLS)
        u = jnp.dot(xn, wu_ref[:, cs], preferred_element_type=F32)
        g = jnp.dot(xn, wg_ref[:, cs], preferred_element_type=F32)
        prev = halo_ref[:, cs]
        halo_ref[:, cs] = g[tm - 8:, :]
        g1 = pltpu.roll(g, 1, 0)
        g2 = pltpu.roll(g, 2, 0)
        g1 = jnp.concatenate([jnp.where(row8 >= 1, g1[:8], prev[7:8]), g1[8:]], axis=0)
        g2 = jnp.concatenate([jnp.where(row8 >= 2, g2[:8], jnp.where(row8 == 1, prev[7:8], prev[6:7])),
                              g2[8:]], axis=0)
        gc = g2 * cw_ref[0:1, cs] + g1 * cw_ref[1:2, cs] + g * cw_ref[2:3, cs] + cb_ref[:, cs]
        o_ref[:, cs] = ((gc / (1.0 + jnp.exp(-gc))) * u).astype(o_ref.dtype)


def _norm_gated_up(x, gain, w_up, w_gate, conv_w, conv_b, tm):
    rows, d = x.shape
    dff = w_up.shape[1]
    assert dff % GATE_COLS == 0
    const = lambda shape: pl.BlockSpec(shape, lambda i: (0, 0))
    return pl.pallas_call(
        _norm_gate_kernel,
        out_shape=jax.ShapeDtypeStruct((rows, dff), BF16),
        grid=(rows // tm,),
        in_specs=[pl.BlockSpec((tm, d), lambda i: (i, 0)),
                  const((1, d)), const((d, dff)), const((d, dff)), const((CONV_WIDTH, dff)), const((1, dff))],
        out_specs=pl.BlockSpec((tm, dff), lambda i: (i, 0)),
        scratch_shapes=[pltpu.VMEM((8, dff), F32)],
        compiler_params=_params("arbitrary"),
        name="norm_gated_up",
    )(x, gain.reshape(1, d), w_up.astype(BF16), w_gate.astype(BF16), conv_w.astype(F32),
      conv_b.reshape(1, dff).astype(F32))


SB_GROUP = 4
SB_DEAD = 110.0


def _stick_breaking_kernel(q_ref, k_ref, v_ref, o_ref, acc_ref):
    i = pl.program_id(2)
    row = lax.broadcasted_iota(jnp.int32, (BLK, BLK), 0)
    col = lax.broadcasted_iota(jnp.int32, (BLK, BLK), 1)
    tri_after = (row > col).astype(BF16)
    scale = C_HEAD_DIM ** -0.5

    heads = [slice(h * C_HEAD_DIM, (h + 1) * C_HEAD_DIM) for h in range(SB_GROUP)]
    q_heads = [q_ref[:, hs] * scale for hs in heads]
    acc_ref[...] = jnp.zeros_like(acc_ref)

    def tiles(blocks, carries):
        keys = [(b, h) for b in range(len(blocks)) for h in range(SB_GROUP)]
        zs = {(b, h): lax.dot_general(q_heads[h], k_ref[blocks[b][0], :, heads[h]], NT_DIMS,
                                      preferred_element_type=F32) for b, h in keys}
        log_beta, rests, sums = {}, {}, {}
        for b, h in keys:
            z, mask = zs[b, h], blocks[b][1]
            sp = jnp.maximum(z, 0.0) + jnp.log(1.0 + jnp.exp2(jnp.abs(z) * -LOG2E))
            if mask is not None:
                sp = jnp.where(mask, sp, 0.0)
            log_beta[b, h] = z - sp
            sums[b, h] = jnp.sum(sp, axis=1, keepdims=True)
            rests[b, h] = jnp.dot(sp.astype(BF16), tri_after, preferred_element_type=F32)
        run = list(carries)
        for b, h in keys:
            j, mask = blocks[b]
            a = jnp.exp2((log_beta[b, h] - (rests[b, h] + run[h])) * LOG2E)
            if mask is not None:
                a = jnp.where(mask, a, 0.0)
            acc_ref[h] += jnp.dot(a.astype(BF16), v_ref[j, :, heads[h]], preferred_element_type=F32)
            run[h] = run[h] + sums[b, h]
        return tuple(run)

    def alive(carries):
        return jnp.min(functools.reduce(jnp.minimum, carries)) < SB_DEAD

    qf = row + i * BLK
    kf = col + i * BLK
    carries = tiles([(i, (kf < qf) & (kf >= LEAD))], tuple(jnp.zeros((BLK, 1), F32) for _ in heads))
    carries = lax.cond((i >= 2) & alive(carries), lambda c: tiles([(i - 1, None)], c), lambda c: c, carries)
    n_rest = jnp.maximum(i - 2, 0)
    n_pairs = jnp.right_shift(n_rest, 1)
    _, carries = lax.while_loop(
        lambda s: (s[0] < n_pairs) & alive(s[1]),
        lambda s: (s[0] + 1, tiles([(i - 2 - 2 * s[0], None), (i - 3 - 2 * s[0], None)], s[1])),
        (jnp.int32(0), carries))
    carries = lax.cond(((n_rest & 1) == 1) & alive(carries), lambda c: tiles([(1, None)], c), lambda c: c,
                       carries)

    @pl.when((i > 0) & alive(carries))
    def _():
        tiles([(0, col >= LEAD)], carries)

    o_ref[...] = jnp.concatenate([acc_ref[h] for h in range(SB_GROUP)], axis=1).astype(o_ref.dtype)


def _stick_breaking(p1, nbatch, nb):
    rows = p1.shape[0]
    cw = C_HEADS * C_HEAD_DIM
    gw = SB_GROUP * C_HEAD_DIM
    ngroups = C_HEADS // SB_GROUP
    p1b = p1.reshape(nbatch, nb, BLK, p1.shape[1])
    return pl.pallas_call(
        _stick_breaking_kernel,
        out_shape=jax.ShapeDtypeStruct((rows, cw), BF16),
        grid=(nbatch, ngroups, nb),
        in_specs=[pl.BlockSpec((BLK, gw), lambda b, g, i: (b * nb + i, g)),
                  pl.BlockSpec((None, nb, BLK, gw), lambda b, g, i: (b, 0, 0, ngroups + g)),
                  pl.BlockSpec((None, nb, BLK, gw), lambda b, g, i: (b, 0, 0, 2 * ngroups + g))],
        out_specs=pl.BlockSpec((BLK, gw), lambda b, g, i: (b * nb + i, g)),
        scratch_shapes=[pltpu.VMEM((SB_GROUP, BLK, C_HEAD_DIM), F32)],
        compiler_params=_params("parallel", "parallel", "arbitrary"),
        name="stick_breaking",
    )(p1, p1b, p1b)


def _pick_tile(rows, pref):
    t = pref
    while rows % t:
        t //= 2
    return t


def _pick_cols(n, cap=2816):
    return max(t for t in range(128, min(n, cap) + 1, 128) if n % t == 0)


def kernel(x, meta_tokens, rel_bias, norm_mix, norm_ffn, norm_final, even_w_in, even_gn_gain, even_w_out, odd_w_in, odd_w_out, ffn_w_up, ffn_w_gate, ffn_conv_w, ffn_conv_b, ffn_w_down):
    nbatch, seq, d = x.shape
    assert seq % BLK == 0
    nb = seq // BLK + 1
    rows = nbatch * nb * BLK
    n_keep = min(TOPK_MAX, seq // 4)
    assert n_keep >= N_META
    depth = norm_mix.shape[0]
    tm = _pick_tile(rows, 1024)

    meta = jnp.broadcast_to(meta_tokens[None].astype(x.dtype), (nbatch, N_META, d))
    h = jnp.concatenate([jnp.zeros((nbatch, LEAD, d), x.dtype), meta, x], axis=1).reshape(rows, d)

    aw = A_HEADS * A_HEAD_DIM
    qkw, vw = B_HEADS * B_QK_DIM, B_HEADS * B_V_DIM
    cw = C_HEADS * C_HEAD_DIM
    bias = _bias_tiles(rel_bias)

    for l in range(depth):
        j = l // 2
        if l % 2 == 0:
            w = even_w_in[j]
            o_iq, o_ik, o_iw, o_bq = 3 * aw, 4 * aw, 4 * aw + IDX_DIM, 4 * aw + IDX_DIM + IDX_HEADS
            w_main = jnp.concatenate([w[:, :4 * aw], w[:, o_bq:]], axis=1).astype(BF16)
            w_idx = jnp.concatenate([w[:, o_ik:o_bq], jnp.zeros((d, 128 - IDX_DIM - IDX_HEADS), w.dtype)],
                                    axis=1).astype(BF16)
            p0 = _norm_matmul(h, norm_mix[l], w_main, tm, _pick_cols(w_main.shape[1]))
            ikw = _norm_matmul(h, norm_mix[l], w_idx, tm, 128)
            a_out = _dsa_attention(p0, ikw, bias, nbatch, nb, n_keep)
            r_out = _retention(p0, even_gn_gain[j], nbatch, nb)
            w_out = even_w_out[j].astype(BF16)
            h = _proj_residual([(a_out, w_out[:aw]), (r_out, w_out[aw:])], h, _pick_tile(rows, 512))
        else:
            p1 = _norm_matmul(h, norm_mix[l], odd_w_in[j].astype(BF16), tm, _pick_cols(odd_w_in.shape[2]))
            s_out = _stick_breaking(p1, nbatch, nb)
            h = _proj_residual([(s_out, odd_w_out[j].astype(BF16))], h, _pick_tile(rows, 512))
        act = _norm_gated_up(h, norm_ffn[l], ffn_w_up[l], ffn_w_gate[l], ffn_conv_w[l], ffn_conv_b[l], tm)
        w_down = ffn_w_down[l].astype(BF16)
        if l + 1 < depth:
            h = _proj_residual([(act, w_down)], h, _pick_tile(rows, 512))
        else:
            h = _proj_residual([(act, w_down)], h, BLK, final=(norm_final, nbatch, nb))
    return h
```

```python
import functools
import math

import jax
import jax.numpy as jnp
import numpy as np
from jax import lax
from jax.experimental import pallas as pl
from jax.experimental.pallas import tpu as pltpu

N_META = 16
BLK = 256
LEAD = BLK - N_META
A_HEADS, A_HEAD_DIM = 8, 64
IDX_HEADS, IDX_DIM = 8, 64
TOPK_MAX = 256
N_BUCKETS, MAX_DISTANCE = 32, 128
B_HEADS, B_QK_DIM, B_V_DIM = 4, 128, 256
ROPE_BASE = 10000.0
C_HEADS, C_HEAD_DIM = 16, 64
CONV_WIDTH = 3
EPS = 1e-6
BIG = 1e30
LOG2E = 1.4426950408889634
BISECT_ITERS = 40
PEEL_SLACK = 7.0
DSA_LOOKAHEAD = 8
VMEM_LIMIT_BYTES = 56 * 1024 * 1024

F32 = jnp.float32
BF16 = jnp.bfloat16
NT_DIMS = (((1,), (1,)), ((), ()))
TN_DIMS = (((0,), (0,)), ((), ()))


def _params(*sem):
    return pltpu.CompilerParams(dimension_semantics=sem, vmem_limit_bytes=VMEM_LIMIT_BYTES)


def _norm_matmul_kernel(x_ref, g_ref, w_ref, o_ref, xn_ref):
    @pl.when(pl.program_id(1) == 0)
    def _():
        x = x_ref[...]
        ms = jnp.mean(x * x, axis=-1, keepdims=True)
        xn_ref[...] = (x * lax.rsqrt(ms + EPS) * g_ref[...]).astype(BF16)

    o_ref[...] = jnp.dot(xn_ref[...], w_ref[...], preferred_element_type=F32).astype(o_ref.dtype)


def _norm_matmul(x, gain, w, tm, tn):
    rows, d = x.shape
    n = w.shape[1]
    return pl.pallas_call(
        _norm_matmul_kernel,
        out_shape=jax.ShapeDtypeStruct((rows, n), BF16),
        grid=(rows // tm, n // tn),
        in_specs=[pl.BlockSpec((tm, d), lambda i, j: (i, 0)),
                  pl.BlockSpec((1, d), lambda i, j: (0, 0)),
                  pl.BlockSpec((d, tn), lambda i, j: (0, j))],
        out_specs=pl.BlockSpec((tm, tn), lambda i, j: (i, j)),
        scratch_shapes=[pltpu.VMEM((tm, d), BF16)],
        compiler_params=_params("parallel", "arbitrary"),
        name="norm_matmul",
    )(x, gain.reshape(1, d), w)


def _bucket_tiles():
    q = np.arange(BLK)[:, None]
    k = np.arange(BLK)[None, :]
    tiles = []
    for t in range(2):
        n = np.maximum(q - k + t * BLK, 0)
        max_exact = N_BUCKETS // 2
        large = max_exact + (np.log(np.maximum(n, 1).astype(np.float32) / max_exact)
                             / math.log(MAX_DISTANCE / max_exact)
                             * (N_BUCKETS - max_exact)).astype(np.int32)
        large = np.minimum(large, N_BUCKETS - 1)
        tiles.append(np.where(n < max_exact, n, large).astype(np.int32))
    return np.stack(tiles)


def _bias_tiles_kernel(rb_ref, idx_ref, o_ref):
    h = pl.program_id(0)
    far = rb_ref[N_BUCKETS - 1, h]
    for t in range(2):
        idx = idx_ref[t]
        acc = jnp.zeros((BLK, BLK), F32)
        for b in range(N_BUCKETS):
            acc = jnp.where(idx == b, rb_ref[b, h] - far, acc)
        o_ref[t] = acc


def _bias_tiles(rel_bias):
    assert BLK + 1 >= MAX_DISTANCE
    return pl.pallas_call(
        _bias_tiles_kernel,
        out_shape=jax.ShapeDtypeStruct((A_HEADS, 2, BLK, BLK), F32),
        grid=(A_HEADS,),
        in_specs=[pl.BlockSpec(memory_space=pltpu.SMEM),
                  pl.BlockSpec((2, BLK, BLK), lambda h: (0, 0, 0))],
        out_specs=pl.BlockSpec((None, 2, BLK, BLK), lambda h: (h, 0, 0, 0)),
        compiler_params=_params("parallel"),
        name="t5_bias_tiles",
    )(rel_bias.astype(F32), jnp.asarray(_bucket_tiles()))


def _dsa_kernel(aq_ref, iq_ref, ikwq_ref, ak_ref, av_ref, ikw_ref, bias_ref, o_ref,
                s_ref, acc_ref, av1_ref, *, n_keep):
    i = pl.program_id(1)
    row = lax.broadcasted_iota(jnp.int32, (BLK, BLK), 0)
    col = lax.broadcasted_iota(jnp.int32, (BLK, BLK), 1)
    qf = col + i * BLK
    heads = [slice(h * A_HEAD_DIM, (h + 1) * A_HEAD_DIM) for h in range(A_HEADS)]
    heads1 = [slice(h * 2 * A_HEAD_DIM, (h + 1) * 2 * A_HEAD_DIM) for h in range(A_HEADS)]

    @pl.when(i == 0)
    def _():
        def fill(j, _):
            ones = jnp.ones((BLK, A_HEAD_DIM), BF16)
            av1_ref[j] = jnp.concatenate([x for hs in heads for x in (av_ref[j, :, hs], ones)], axis=1)
            return 0
        lax.fori_loop(0, av_ref.shape[0], fill, 0)

    s_ref[0] = jnp.where((row >= LEAD) & (row <= qf), BIG, -BIG)
    eye = (lax.broadcasted_iota(jnp.int32, (128, 128), 0)
           == lax.broadcasted_iota(jnp.int32, (128, 128), 1)).astype(BF16)
    ikwq_t = lax.dot_general(eye, ikwq_ref[...], NT_DIMS, preferred_element_type=F32)
    w_scale = (IDX_DIM ** -0.5) * (IDX_HEADS ** -0.5)
    w_rows = [ikwq_t[IDX_DIM + h:IDX_DIM + h + 1, :] * w_scale for h in range(IDX_HEADS)]
    iq_heads = [iq_ref[:, h * IDX_DIM:(h + 1) * IDX_DIM] for h in range(IDX_HEADS)]

    def score_block(j, mabs):
        ik_j = ikw_ref[j, :, :IDX_DIM]
        acc = jnp.zeros((BLK, BLK), F32)
        for h in range(IDX_HEADS):
            st = lax.dot_general(ik_j, iq_heads[h], NT_DIMS, preferred_element_type=F32)
            acc = acc + jnp.maximum(st, 0.0) * w_rows[h]
        causal = (row + j * BLK) <= qf
        s_ref[j] = jnp.where(causal, acc, -BIG)
        return jnp.maximum(mabs, jnp.max(jnp.where(causal, jnp.abs(acc), 0.0), axis=0, keepdims=True))

    mabs = lax.fori_loop(1, i + 1, score_block, jnp.zeros((1, BLK), F32))

    k_eff = float(n_keep - N_META)
    bound = mabs * 1.000001 + 1e-30

    def count_ge(thr):
        def body(j, c):
            hit = (s_ref[j] >= thr).astype(F32).reshape(4, BLK // 32, 8, BLK)
            return c + jnp.sum(hit, axis=1)
        c = lax.fori_loop(1, i + 1, body, jnp.zeros((4, 8, BLK), F32))
        return jnp.sum(jnp.sum(c, axis=0), axis=0, keepdims=True)

    def unsettled(state):
        it, _, _, cnt_lo = state
        return (it < BISECT_ITERS) & (jnp.max(cnt_lo) > k_eff + PEEL_SLACK)

    def halve(lo, hi, cnt_lo):
        mid = lo + (hi - lo) * 0.5
        cnt = count_ge(mid)
        ge = cnt >= k_eff
        return jnp.where(ge, mid, lo), jnp.where(ge, hi, mid), jnp.where(ge, cnt, cnt_lo)

    def bisect(state):
        it, lo, hi, cnt_lo = state
        return (it + 2,) + halve(*halve(lo, hi, cnt_lo))

    _, lo, _, _ = lax.while_loop(unsettled, bisect, (jnp.int32(0), -bound, bound, count_ge(-bound)))

    def vmin_body(j, v):
        s = s_ref[j]
        return jnp.minimum(v, jnp.min(jnp.where(s >= lo, s, BIG), axis=0, keepdims=True))

    v = lax.fori_loop(1, i + 1, vmin_body, jnp.full((1, BLK), BIG, F32))
    v = jnp.where(v >= BIG, 0.0, v)

    def above(x):
        def body(j, carry):
            cnt, nxt = carry
            s = s_ref[j]
            gt = s > x
            cnt = cnt + jnp.sum(gt.astype(F32).reshape(4, BLK // 32, 8, BLK), axis=1)
            nxt = jnp.minimum(nxt, jnp.min(jnp.where(gt, s, BIG).reshape(4, BLK // 32, 8, BLK), axis=1))
            return cnt, nxt
        cnt, nxt = lax.fori_loop(1, i + 1, body, (jnp.zeros((4, 8, BLK), F32), jnp.full((4, 8, BLK), BIG, F32)))
        return (jnp.sum(jnp.sum(cnt, axis=0), axis=0, keepdims=True),
                jnp.min(jnp.min(nxt, axis=0), axis=0, keepdims=True))

    def peel(state):
        x, cnt_gt, nxt = state
        x = jnp.where(cnt_gt >= k_eff, nxt, x)
        return (x,) + above(x)

    v, cnt_gt, _ = lax.while_loop(lambda s: jnp.max(s[1]) >= k_eff, peel, (v,) + above(v))
    ties_wanted = k_eff - cnt_gt

    acc_ref[...] = jnp.zeros_like(acc_ref)
    tri_incl = (col <= row).astype(BF16)
    aq_heads = [aq_ref[:, hs] * (A_HEAD_DIM ** -0.5) for hs in heads]

    def attend(js, carry, near):
        ties_seen, ms = carry
        negs = []
        for j in js:
            s = s_ref[j]
            eq = s == v
            rank = ties_seen + jnp.dot(tri_incl, eq.astype(BF16), preferred_element_type=F32)
            sel = (s > v) | (eq & (rank <= ties_wanted))
            negs.append(jnp.where(sel, 0.0, -BIG).T)
            ties_seen = rank[BLK - 1:BLK, :]

        def logits(t, h):
            z = lax.dot_general(aq_heads[h], ak_ref[js[t], :, heads[h]], NT_DIMS, preferred_element_type=F32)
            return z + (bias_ref[h, i - js[t]] + negs[t] if near else negs[t])

        zs = {(t, h): logits(t, h) for h in range(A_HEADS) for t in range(len(js))}
        ms_new = []
        for h in range(A_HEADS):
            zh = [zs.pop((t, h)) for t in range(len(js))]
            m_new = jnp.maximum(ms[h], jnp.max(functools.reduce(jnp.maximum, zh), axis=1, keepdims=True))
            alpha = jnp.exp(ms[h] - m_new)
            ms_new.append(m_new)
            acc = alpha * acc_ref[h]
            for t, z in enumerate(zh):
                acc = acc + jnp.dot(jnp.exp(z - m_new).astype(BF16), av1_ref[js[t], :, heads1[h]],
                                    preferred_element_type=F32)
            acc_ref[h] = acc
        return ties_seen, tuple(ms_new)

    init = (jnp.zeros((1, BLK), F32), tuple(jnp.full((BLK, 1), -BIG, F32) for _ in heads))
    n_far = jnp.maximum(i - 1, 0)
    carry = lax.fori_loop(0, jnp.right_shift(n_far, 1),
                          lambda t, c: attend((2 * t, 2 * t + 1), c, False), init)
    carry = lax.cond((n_far & 1) == 1, lambda c: attend((n_far - 1,), c, False), lambda c: c, carry)
    lax.fori_loop(n_far, i + 1, lambda j, c: attend((j,), c, True), carry)

    valid_q = (lax.broadcasted_iota(jnp.int32, (BLK, 1), 0) + i * BLK) >= LEAD
    outs = []
    for h in range(A_HEADS):
        acc = acc_ref[h]
        outs.append((acc / pltpu.roll(acc, A_HEAD_DIM, 1))[:, :A_HEAD_DIM])
    o_ref[...] = jnp.where(valid_q, jnp.concatenate(outs, axis=1), 0.0).astype(o_ref.dtype)


def _dsa_attention(p0, ikw, bias, nbatch, nb, n_keep):
    rows = p0.shape[0]
    aw = A_HEADS * A_HEAD_DIM
    p0b = p0.reshape(nbatch, nb, BLK, p0.shape[1])
    ikwb = ikw.reshape(nbatch, nb, BLK, ikw.shape[1])
    return pl.pallas_call(
        functools.partial(_dsa_kernel, n_keep=n_keep),
        out_shape=jax.ShapeDtypeStruct((rows, aw), BF16),
        grid=(nbatch, nb),
        in_specs=[pl.BlockSpec((BLK, aw), lambda b, i: (b * nb + i, 0)),
                  pl.BlockSpec((BLK, aw), lambda b, i: (b * nb + i, 3)),
                  pl.BlockSpec((BLK, 128), lambda b, i: (b * nb + i, 0)),
                  pl.BlockSpec((None, nb, BLK, aw), lambda b, i: (b, 0, 0, 1)),
                  pl.BlockSpec((None, nb, BLK, aw), lambda b, i: (b, 0, 0, 2)),
                  pl.BlockSpec((None, nb, BLK, 128), lambda b, i: (b, 0, 0, 0)),
                  pl.BlockSpec((A_HEADS, 2, BLK, BLK), lambda b, i: (0, 0, 0, 0))],
        out_specs=pl.BlockSpec((BLK, aw), lambda b, i: (b * nb + i, 0)),
        scratch_shapes=[pltpu.VMEM((nb, BLK, BLK), F32),
                        pltpu.VMEM((A_HEADS, BLK, 2 * A_HEAD_DIM), F32),
                        pltpu.VMEM((nb, BLK, 2 * aw), BF16)],
        compiler_params=_params("parallel", "arbitrary"),
        name="dsa_attention",
    )(p0, p0, ikw, p0b, p0b, ikwb, bias)


def _retention_kernel(bq_ref, bk_ref, bv_ref, bg_ref, cos_ref, sin_ref, dmat_ref, xi_ref, zeta_ref,
                      gain_ref, o_ref, r_ref, *, g_chunk):
    @pl.when(pl.program_id(1) == 0)
    def _():
        r_ref[...] = jnp.zeros_like(r_ref)

    cosf = cos_ref[...]
    sinf = sin_ref[...]

    def rot(x):
        return x * cosf + pltpu.roll(x, B_QK_DIM // 2, 1) * sinf

    for h in range(B_HEADS):
        ks = slice(h * B_QK_DIM, (h + 1) * B_QK_DIM)
        vs = slice(h * B_V_DIM, (h + 1) * B_V_DIM)
        q = rot(bq_ref[:, ks].astype(F32))
        k = rot(bk_ref[:, ks].astype(F32)) * (B_QK_DIM ** -0.5)
        qb = q.astype(BF16)
        kb = k.astype(BF16)
        v = bv_ref[:, vs]
        inner = lax.dot_general(qb, kb, NT_DIMS, preferred_element_type=F32) * dmat_ref[h]
        r_old = r_ref[h]
        o = (jnp.dot(inner.astype(BF16), v, preferred_element_type=F32)
             + jnp.dot(qb, r_old.astype(BF16), preferred_element_type=F32) * xi_ref[h])
        kz = (k * zeta_ref[h]).astype(BF16)
        r_ref[h] = r_old * g_chunk[h] + lax.dot_general(kz, v, TN_DIMS, preferred_element_type=F32)
        mu = jnp.mean(o, axis=-1, keepdims=True)
        oc = o - mu
        var = jnp.mean(oc * oc, axis=-1, keepdims=True)
        rn = oc * lax.rsqrt(var + EPS) * gain_ref[:, vs]
        gate = bg_ref[:, vs].astype(F32)
        o_ref[:, vs] = (rn * (gate / (1.0 + jnp.exp(-gate)))).astype(o_ref.dtype)


def _retention(p0, gn_gain, nbatch, nb):
    rows = p0.shape[0]
    qkw, vw = B_HEADS * B_QK_DIM, B_HEADS * B_V_DIM
    frame = nb * BLK
    half = B_QK_DIM // 2
    pos = (jnp.arange(frame) - LEAD).astype(F32)
    inv = 1.0 / (ROPE_BASE ** (jnp.arange(half, dtype=F32) / half))
    ang = pos[:, None] * inv[None, :]
    cosf = jnp.concatenate([jnp.cos(ang), jnp.cos(ang)], axis=-1)
    sinf = jnp.concatenate([-jnp.sin(ang), jnp.sin(ang)], axis=-1)
    lg = jnp.log(1.0 - 2.0 ** (-5.0 - jnp.arange(B_HEADS, dtype=F32)))
    n = jnp.arange(BLK, dtype=F32)
    diff = n[:, None] - n[None, :]
    dmat = jnp.where(diff[None] >= 0, jnp.exp(jnp.maximum(diff, 0.0)[None] * lg[:, None, None]), 0.0)
    xi = jnp.broadcast_to(jnp.exp((n[None, :] + 1.0) * lg[:, None])[..., None], (B_HEADS, BLK, B_V_DIM))
    zeta = jnp.broadcast_to(jnp.exp((BLK - 1.0 - n[None, :]) * lg[:, None])[..., None],
                            (B_HEADS, BLK, B_QK_DIM))
    g_chunk = tuple(float(math.exp(BLK * math.log(1.0 - 2.0 ** (-5.0 - h)))) for h in range(B_HEADS))
    const = lambda shape: pl.BlockSpec(shape, lambda b, i: (0,) * len(shape))
    return pl.pallas_call(
        functools.partial(_retention_kernel, g_chunk=g_chunk),
        out_shape=jax.ShapeDtypeStruct((rows, vw), BF16),
        grid=(nbatch, nb),
        in_specs=[pl.BlockSpec((BLK, qkw), lambda b, i: (b * nb + i, 4)),
                  pl.BlockSpec((BLK, qkw), lambda b, i: (b * nb + i, 5)),
                  pl.BlockSpec((BLK, vw), lambda b, i: (b * nb + i, 3)),
                  pl.BlockSpec((BLK, vw), lambda b, i: (b * nb + i, 4)),
                  pl.BlockSpec((BLK, B_QK_DIM), lambda b, i: (i, 0)),
                  pl.BlockSpec((BLK, B_QK_DIM), lambda b, i: (i, 0)),
                  const((B_HEADS, BLK, BLK)),
                  const((B_HEADS, BLK, B_V_DIM)),
                  const((B_HEADS, BLK, B_QK_DIM)),
                  const((1, vw))],
        out_specs=pl.BlockSpec((BLK, vw), lambda b, i: (b * nb + i, 0)),
        scratch_shapes=[pltpu.VMEM((B_HEADS, B_QK_DIM, B_V_DIM), F32)],
        compiler_params=_params("parallel", "arbitrary"),
        name="retention",
    )(p0, p0, p0, p0, cosf, sinf, dmat, xi, zeta, gn_gain.reshape(1, vw).astype(F32))


def _proj_residual_kernel(*refs, n_pairs, final):
    x = refs[2 * n_pairs][...]
    for t in range(n_pairs):
        x = x + jnp.dot(refs[2 * t][...], refs[2 * t + 1][...], preferred_element_type=F32)
    if final:
        ng_ref, o_ref = refs[2 * n_pairs + 1:]
        ms = jnp.mean(x * x, axis=-1, keepdims=True)
        o_ref[...] = x * lax.rsqrt(ms + EPS) * ng_ref[...]
    else:
        refs[2 * n_pairs + 1][...] = x


def _proj_residual(pairs, h, tm, final=None):
    rows, d = h.shape
    in_specs, args = [], []
    for a, w in pairs:
        in_specs += [pl.BlockSpec((tm, a.shape[1]), lambda i: (i, 0)),
                     pl.BlockSpec(w.shape, lambda i: (0, 0))]
        args += [a, w]
    in_specs.append(pl.BlockSpec((tm, d), lambda i: (i, 0)))
    args.append(h)
    if final is None:
        out_shape = jax.ShapeDtypeStruct((rows, d), F32)
        out_spec = pl.BlockSpec((tm, d), lambda i: (i, 0))
        sem = "parallel"
    else:
        gain, nbatch, nb = final
        assert tm == BLK
        in_specs.append(pl.BlockSpec((1, d), lambda i: (0, 0)))
        args.append(gain.reshape(1, d).astype(F32))
        out_shape = jax.ShapeDtypeStruct((nbatch, (nb - 1) * BLK, d), F32)
        out_spec = pl.BlockSpec((None, BLK, d), lambda i: (i // nb, jnp.maximum(i % nb - 1, 0), 0))
        sem = "arbitrary"
    return pl.pallas_call(
        functools.partial(_proj_residual_kernel, n_pairs=len(pairs), final=final is not None),
        out_shape=out_shape,
        grid=(rows // tm,),
        in_specs=in_specs,
        out_specs=out_spec,
        compiler_params=_params(sem),
        name="proj_residual",
    )(*args)


GATE_COLS = 256


def _norm_gate_kernel(x_ref, gn_ref, wu_ref, wg_ref, cw_ref, cb_ref, o_ref, halo_ref):
    @pl.when(pl.program_id(0) == 0)
    def _():
        halo_ref[...] = jnp.zeros_like(halo_ref)

    x = x_ref[...]
    ms = jnp.mean(x * x, axis=-1, keepdims=True)
    xn = (x * lax.rsqrt(ms + EPS) * gn_ref[...]).astype(BF16)
    tm = x.shape[0]
    row8 = lax.broadcasted_iota(jnp.int32, (8, GATE_COLS), 0)
    for c in range(o_ref.shape[1] // GATE_COLS):
        cs = slice(c * GATE_COLS, (c + 1) * GATE_COLS)
        u = jnp.dot(xn, wu_ref[:, cs], preferred_element_type=F32)
        g = jnp.dot(xn, wg_ref[:, cs], preferred_element_type=F32)
        prev = halo_ref[:, cs]
        halo_ref[:, cs] = g[tm - 8:, :]
        g1 = pltpu.roll(g, 1, 0)
        g2 = pltpu.roll(g, 2, 0)
        g1 = jnp.concatenate([jnp.where(row8 >= 1, g1[:8], prev[7:8]), g1[8:]], axis=0)
        g2 = jnp.concatenate([jnp.where(row8 >= 2, g2[:8], jnp.where(row8 == 1, prev[7:8], prev[6:7])),
                              g2[8:]], axis=0)
        gc = g2 * cw_ref[0:1, cs] + g1 * cw_ref[1:2, cs] + g * cw_ref[2:3, cs] + cb_ref[:, cs]
        o_ref[:, cs] = ((gc / (1.0 + jnp.exp(-gc))) * u).astype(o_ref.dtype)


def _norm_gated_up(x, gain, w_up, w_gate, conv_w, conv_b, tm):
    rows, d = x.shape
    dff = w_up.shape[1]
    assert dff % GATE_COLS == 0
    const = lambda shape: pl.BlockSpec(shape, lambda i: (0, 0))
    return pl.pallas_call(
        _norm_gate_kernel,
        out_shape=jax.ShapeDtypeStruct((rows, dff), BF16),
        grid=(rows // tm,),
        in_specs=[pl.BlockSpec((tm, d), lambda i: (i, 0)),
                  const((1, d)), const((d, dff)), const((d, dff)), const((CONV_WIDTH, dff)), const((1, dff))],
        out_specs=pl.BlockSpec((tm, dff), lambda i: (i, 0)),
        scratch_shapes=[pltpu.VMEM((8, dff), F32)],
        compiler_params=_params("arbitrary"),
        name="norm_gated_up",
    )(x, gain.reshape(1, d), w_up.astype(BF16), w_gate.astype(BF16), conv_w.astype(F32),
      conv_b.reshape(1, dff).astype(F32))


SB_GROUP = 8
SB_DEAD = 110.0


def _stick_breaking_kernel(q_ref, k_ref, v_ref, o_ref, acc_ref):
    i = pl.program_id(2)
    row = lax.broadcasted_iota(jnp.int32, (BLK, BLK), 0)
    col = lax.broadcasted_iota(jnp.int32, (BLK, BLK), 1)
    tri_after = (row > col).astype(BF16)
    scale = C_HEAD_DIM ** -0.5

    heads = [slice(h * C_HEAD_DIM, (h + 1) * C_HEAD_DIM) for h in range(SB_GROUP)]
    q_heads = [q_ref[:, hs] * scale for hs in heads]
    acc_ref[...] = jnp.zeros_like(acc_ref)

    def tiles(blocks, carries):
        keys = [(b, h) for b in range(len(blocks)) for h in range(SB_GROUP)]
        zs = {(b, h): lax.dot_general(q_heads[h], k_ref[blocks[b][0], :, heads[h]], NT_DIMS,
                                      preferred_element_type=F32) for b, h in keys}
        log_beta, rests, sums = {}, {}, {}
        for b, h in keys:
            z, mask = zs[b, h], blocks[b][1]
            sp = jnp.maximum(z, 0.0) + jnp.log(1.0 + jnp.exp2(jnp.abs(z) * -LOG2E))
            if mask is not None:
                sp = jnp.where(mask, sp, 0.0)
            log_beta[b, h] = z - sp
            sums[b, h] = jnp.sum(sp, axis=1, keepdims=True)
            rests[b, h] = jnp.dot(sp.astype(BF16), tri_after, preferred_element_type=F32)
        run = list(carries)
        for b, h in keys:
            j, mask = blocks[b]
            a = jnp.exp2((log_beta[b, h] - (rests[b, h] + run[h])) * LOG2E)
            if mask is not None:
                a = jnp.where(mask, a, 0.0)
            acc_ref[h] += jnp.dot(a.astype(BF16), v_ref[j, :, heads[h]], preferred_element_type=F32)
            run[h] = run[h] + sums[b, h]
        return tuple(run)

    def alive(carries):
        return jnp.min(functools.reduce(jnp.minimum, carries)) < SB_DEAD

    qf = row + i * BLK
    kf = col + i * BLK
    carries = tiles([(i, (kf < qf) & (kf >= LEAD))], tuple(jnp.zeros((BLK, 1), F32) for _ in heads))
    carries = lax.cond((i >= 2) & alive(carries), lambda c: tiles([(i - 1, None)], c), lambda c: c, carries)
    n_rest = jnp.maximum(i - 2, 0)
    n_pairs = jnp.right_shift(n_rest, 1)
    _, carries = lax.while_loop(
        lambda s: (s[0] < n_pairs) & alive(s[1]),
        lambda s: (s[0] + 1, tiles([(i - 2 - 2 * s[0], None), (i - 3 - 2 * s[0], None)], s[1])),
        (jnp.int32(0), carries))
    carries = lax.cond(((n_rest & 1) == 1) & alive(carries), lambda c: tiles([(1, None)], c), lambda c: c,
                       carries)

    @pl.when((i > 0) & alive(carries))
    def _():
        tiles([(0, col >= LEAD)], carries)

    o_ref[...] = jnp.concatenate([acc_ref[h] for h in range(SB_GROUP)], axis=1).astype(o_ref.dtype)


def _stick_breaking(p1, nbatch, nb):
    rows = p1.shape[0]
    cw = C_HEADS * C_HEAD_DIM
    gw = SB_GROUP * C_HEAD_DIM
    ngroups = C_HEADS // SB_GROUP
    p1b = p1.reshape(nbatch, nb, BLK, p1.shape[1])
    return pl.pallas_call(
        _stick_breaking_kernel,
        out_shape=jax.ShapeDtypeStruct((rows, cw), BF16),
        grid=(nbatch, ngroups, nb),
        in_specs=[pl.BlockSpec((BLK, gw), lambda b, g, i: (b * nb + i, g)),
                  pl.BlockSpec((None, nb, BLK, gw), lambda b, g, i: (b, 0, 0, ngroups + g)),
                  pl.BlockSpec((None, nb, BLK, gw), lambda b, g, i: (b, 0, 0, 2 * ngroups + g))],
        out_specs=pl.BlockSpec((BLK, gw), lambda b, g, i: (b * nb + i, g)),
        scratch_shapes=[pltpu.VMEM((SB_GROUP, BLK, C_HEAD_DIM), F32)],
        compiler_params=_params("parallel", "parallel", "arbitrary"),
        name="stick_breaking",
    )(p1, p1b, p1b)


def _pick_tile(rows, pref):
    t = pref
    while rows % t:
        t //= 2
    return t


def _pick_cols(n, cap=2816):
    return max(t for t in range(128, min(n, cap) + 1, 128) if n % t == 0)


def kernel(x, meta_tokens, rel_bias, norm_mix, norm_ffn, norm_final, even_w_in, even_gn_gain, even_w_out, odd_w_in, odd_w_out, ffn_w_up, ffn_w_gate, ffn_conv_w, ffn_conv_b, ffn_w_down):
    nbatch, seq, d = x.shape
    assert seq % BLK == 0
    nb = seq // BLK + 1
    rows = nbatch * nb * BLK
    n_keep = min(TOPK_MAX, seq // 4)
    assert n_keep >= N_META
    depth = norm_mix.shape[0]
    tm = _pick_tile(rows, 1024)

    meta = jnp.broadcast_to(meta_tokens[None].astype(x.dtype), (nbatch, N_META, d))
    h = jnp.concatenate([jnp.zeros((nbatch, LEAD, d), x.dtype), meta, x], axis=1).reshape(rows, d)

    aw = A_HEADS * A_HEAD_DIM
    qkw, vw = B_HEADS * B_QK_DIM, B_HEADS * B_V_DIM
    cw = C_HEADS * C_HEAD_DIM
    bias = _bias_tiles(rel_bias)

    for l in range(depth):
        j = l // 2
        if l % 2 == 0:
            w = even_w_in[j]
            o_iq, o_ik, o_iw, o_bq = 3 * aw, 4 * aw, 4 * aw + IDX_DIM, 4 * aw + IDX_DIM + IDX_HEADS
            w_main = jnp.concatenate([w[:, :4 * aw], w[:, o_bq:]], axis=1).astype(BF16)
            w_idx = jnp.concatenate([w[:, o_ik:o_bq], jnp.zeros((d, 128 - IDX_DIM - IDX_HEADS), w.dtype)],
                                    axis=1).astype(BF16)
            p0 = _norm_matmul(h, norm_mix[l], w_main, tm, _pick_cols(w_main.shape[1]))
            ikw = _norm_matmul(h, norm_mix[l], w_idx, tm, 128)
            a_out = _dsa_attention(p0, ikw, bias, nbatch, nb, n_keep)
            r_out = _retention(p0, even_gn_gain[j], nbatch, nb)
            w_out = even_w_out[j].astype(BF16)
            h = _proj_residual([(a_out, w_out[:aw]), (r_out, w_out[aw:])], h, _pick_tile(rows, 512))
        else:
            p1 = _norm_matmul(h, norm_mix[l], odd_w_in[j].astype(BF16), tm, _pick_cols(odd_w_in.shape[2]))
            s_out = _stick_breaking(p1, nbatch, nb)
            h = _proj_residual([(s_out, odd_w_out[j].astype(BF16))], h, _pick_tile(rows, 512))
        act = _norm_gated_up(h, norm_ffn[l], ffn_w_up[l], ffn_w_gate[l], ffn_conv_w[l], ffn_conv_b[l], tm)
        w_down = ffn_w_down[l].astype(BF16)
        if l + 1 < depth:
            h = _proj_residual([(act, w_down)], h, _pick_tile(rows, 512))
        else:
            h = _proj_residual([(act, w_down)], h, BLK, final=(norm_final, nbatch, nb))
    return h
```

```python
import functools
import math

import jax
import jax.numpy as jnp
import numpy as np
from jax import lax
from jax.experimental import pallas as pl
from jax.experimental.pallas import tpu as pltpu

N_META = 16
BLK = 256
LEAD = BLK - N_META
A_HEADS, A_HEAD_DIM = 8, 64
IDX_HEADS, IDX_DIM = 8, 64
TOPK_MAX = 256
N_BUCKETS, MAX_DISTANCE = 32, 128
B_HEADS, B_QK_DIM, B_V_DIM = 4, 128, 256
ROPE_BASE = 10000.0
C_HEADS, C_HEAD_DIM = 16, 64
CONV_WIDTH = 3
EPS = 1e-6
BIG = 1e30
LOG2E = 1.4426950408889634
BISECT_ITERS = 40
PEEL_SLACK = 7.0
DSA_LOOKAHEAD = 8
VMEM_LIMIT_BYTES = 56 * 1024 * 1024

F32 = jnp.float32
BF16 = jnp.bfloat16
NT_DIMS = (((1,), (1,)), ((), ()))
TN_DIMS = (((0,), (0,)), ((), ()))


def _params(*sem):
    return pltpu.CompilerParams(dimension_semantics=sem, vmem_limit_bytes=VMEM_LIMIT_BYTES)


def _norm_matmul_kernel(x_ref, g_ref, w_ref, o_ref, xn_ref):
    @pl.when(pl.program_id(1) == 0)
    def _():
        x = x_ref[...]
        ms = jnp.mean(x * x, axis=-1, keepdims=True)
        xn_ref[...] = (x * lax.rsqrt(ms + EPS) * g_ref[...]).astype(BF16)

    o_ref[...] = jnp.dot(xn_ref[...], w_ref[...], preferred_element_type=F32).astype(o_ref.dtype)


def _norm_matmul(x, gain, w, tm, tn):
    rows, d = x.shape
    n = w.shape[1]
    return pl.pallas_call(
        _norm_matmul_kernel,
        out_shape=jax.ShapeDtypeStruct((rows, n), BF16),
        grid=(rows // tm, n // tn),
        in_specs=[pl.BlockSpec((tm, d), lambda i, j: (i, 0)),
                  pl.BlockSpec((1, d), lambda i, j: (0, 0)),
                  pl.BlockSpec((d, tn), lambda i, j: (0, j))],
        out_specs=pl.BlockSpec((tm, tn), lambda i, j: (i, j)),
        scratch_shapes=[pltpu.VMEM((tm, d), BF16)],
        compiler_params=_params("parallel", "arbitrary"),
        name="norm_matmul",
    )(x, gain.reshape(1, d), w)


def _bucket_tiles():
    q = np.arange(BLK)[:, None]
    k = np.arange(BLK)[None, :]
    tiles = []
    for t in range(2):
        n = np.maximum(q - k + t * BLK, 0)
        max_exact = N_BUCKETS // 2
        large = max_exact + (np.log(np.maximum(n, 1).astype(np.float32) / max_exact)
                             / math.log(MAX_DISTANCE / max_exact)
                             * (N_BUCKETS - max_exact)).astype(np.int32)
        large = np.minimum(large, N_BUCKETS - 1)
        tiles.append(np.where(n < max_exact, n, large).astype(np.int32))
    return np.stack(tiles)


def _bias_tiles_kernel(rb_ref, idx_ref, o_ref):
    h = pl.program_id(0)
    far = rb_ref[N_BUCKETS - 1, h]
    for t in range(2):
        idx = idx_ref[t]
        acc = jnp.zeros((BLK, BLK), F32)
        for b in range(N_BUCKETS):
            acc = jnp.where(idx == b, rb_ref[b, h] - far, acc)
        o_ref[t] = acc


def _bias_tiles(rel_bias):
    assert BLK + 1 >= MAX_DISTANCE
    return pl.pallas_call(
        _bias_tiles_kernel,
        out_shape=jax.ShapeDtypeStruct((A_HEADS, 2, BLK, BLK), F32),
        grid=(A_HEADS,),
        in_specs=[pl.BlockSpec(memory_space=pltpu.SMEM),
                  pl.BlockSpec((2, BLK, BLK), lambda h: (0, 0, 0))],
        out_specs=pl.BlockSpec((None, 2, BLK, BLK), lambda h: (h, 0, 0, 0)),
        compiler_params=_params("parallel"),
        name="t5_bias_tiles",
    )(rel_bias.astype(F32), jnp.asarray(_bucket_tiles()))


def _dsa_kernel(aq_ref, iq_ref, ikwq_ref, ak_ref, av_ref, ikw_ref, bias_ref, o_ref,
                s_ref, acc_ref, av1_ref, *, n_keep):
    i = pl.program_id(1)
    row = lax.broadcasted_iota(jnp.int32, (BLK, BLK), 0)
    col = lax.broadcasted_iota(jnp.int32, (BLK, BLK), 1)
    qf = col + i * BLK
    heads = [slice(h * A_HEAD_DIM, (h + 1) * A_HEAD_DIM) for h in range(A_HEADS)]
    heads1 = [slice(h * 2 * A_HEAD_DIM, (h + 1) * 2 * A_HEAD_DIM) for h in range(A_HEADS)]

    @pl.when(i == 0)
    def _():
        def fill(j, _):
            ones = jnp.ones((BLK, A_HEAD_DIM), BF16)
            av1_ref[j] = jnp.concatenate([x for hs in heads for x in (av_ref[j, :, hs], ones)], axis=1)
            return 0
        lax.fori_loop(0, av_ref.shape[0], fill, 0)

    s_ref[0] = jnp.where((row >= LEAD) & (row <= qf), BIG, -BIG)
    eye = (lax.broadcasted_iota(jnp.int32, (128, 128), 0)
           == lax.broadcasted_iota(jnp.int32, (128, 128), 1)).astype(BF16)
    ikwq_t = lax.dot_general(eye, ikwq_ref[...], NT_DIMS, preferred_element_type=F32)
    w_scale = (IDX_DIM ** -0.5) * (IDX_HEADS ** -0.5)
    w_rows = [ikwq_t[IDX_DIM + h:IDX_DIM + h + 1, :] * w_scale for h in range(IDX_HEADS)]
    iq_heads = [iq_ref[:, h * IDX_DIM:(h + 1) * IDX_DIM] for h in range(IDX_HEADS)]

    def score_blocks(js, mabs):
        accs = [jnp.zeros((BLK, BLK), F32) for _ in js]
        for h in range(IDX_HEADS):
            for t, j in enumerate(js):
                st = lax.dot_general(ikw_ref[j, :, :IDX_DIM], iq_heads[h], NT_DIMS, preferred_element_type=F32)
                accs[t] = accs[t] + jnp.maximum(st, 0.0) * w_rows[h]
        for j, acc in zip(js, accs):
            causal = (row + j * BLK) <= qf
            s_ref[j] = jnp.where(causal, acc, -BIG)
            mabs = jnp.maximum(mabs, jnp.max(jnp.where(causal, jnp.abs(acc), 0.0), axis=0, keepdims=True))
        return mabs

    mabs = lax.fori_loop(0, jnp.right_shift(i, 1), lambda t, m: score_blocks((2 * t + 1, 2 * t + 2), m),
                         jnp.zeros((1, BLK), F32))
    mabs = lax.cond((i & 1) == 1, lambda m: score_blocks((i,), m), lambda m: m, mabs)

    k_eff = float(n_keep - N_META)
    bound = mabs * 1.000001 + 1e-30

    def count_ge(thr):
        def body(j, c):
            hit = (s_ref[j] >= thr).astype(F32).reshape(4, BLK // 32, 8, BLK)
            return c + jnp.sum(hit, axis=1)
        c = lax.fori_loop(1, i + 1, body, jnp.zeros((4, 8, BLK), F32))
        return jnp.sum(jnp.sum(c, axis=0), axis=0, keepdims=True)

    def unsettled(state):
        it, _, _, cnt_lo = state
        return (it < BISECT_ITERS) & (jnp.max(cnt_lo) > k_eff + PEEL_SLACK)

    def halve(lo, hi, cnt_lo):
        mid = lo + (hi - lo) * 0.5
        cnt = count_ge(mid)
        ge = cnt >= k_eff
        return jnp.where(ge, mid, lo), jnp.where(ge, hi, mid), jnp.where(ge, cnt, cnt_lo)

    def bisect(state):
        it, lo, hi, cnt_lo = state
        return (it + 2,) + halve(*halve(lo, hi, cnt_lo))

    _, lo, _, _ = lax.while_loop(unsettled, bisect, (jnp.int32(0), -bound, bound, count_ge(-bound)))

    def vmin_body(j, v):
        s = s_ref[j]
        return jnp.minimum(v, jnp.min(jnp.where(s >= lo, s, BIG), axis=0, keepdims=True))

    v = lax.fori_loop(1, i + 1, vmin_body, jnp.full((1, BLK), BIG, F32))
    v = jnp.where(v >= BIG, 0.0, v)

    def above(x):
        def body(j, carry):
            cnt, nxt = carry
            s = s_ref[j]
            gt = s > x
            cnt = cnt + jnp.sum(gt.astype(F32).reshape(4, BLK // 32, 8, BLK), axis=1)
            nxt = jnp.minimum(nxt, jnp.min(jnp.where(gt, s, BIG).reshape(4, BLK // 32, 8, BLK), axis=1))
            return cnt, nxt
        cnt, nxt = lax.fori_loop(1, i + 1, body, (jnp.zeros((4, 8, BLK), F32), jnp.full((4, 8, BLK), BIG, F32)))
        return (jnp.sum(jnp.sum(cnt, axis=0), axis=0, keepdims=True),
                jnp.min(jnp.min(nxt, axis=0), axis=0, keepdims=True))

    def peel(state):
        x, cnt_gt, nxt = state
        x = jnp.where(cnt_gt >= k_eff, nxt, x)
        return (x,) + above(x)

    v, cnt_gt, _ = lax.while_loop(lambda s: jnp.max(s[1]) >= k_eff, peel, (v,) + above(v))
    ties_wanted = k_eff - cnt_gt

    acc_ref[...] = jnp.zeros_like(acc_ref)
    tri_incl = (col <= row).astype(BF16)
    aq_heads = [aq_ref[:, hs] * (A_HEAD_DIM ** -0.5) for hs in heads]

    def attend(js, carry, near):
        ties_seen, ms = carry
        negs = []
        for j in js:
            s = s_ref[j]
            eq = s == v
            rank = ties_seen + jnp.dot(tri_incl, eq.astype(BF16), preferred_element_type=F32)
            sel = (s > v) | (eq & (rank <= ties_wanted))
            negs.append(jnp.where(sel, 0.0, -BIG).T)
            ties_seen = rank[BLK - 1:BLK, :]

        def logits(t, h):
            z = lax.dot_general(aq_heads[h], ak_ref[js[t], :, heads[h]], NT_DIMS, preferred_element_type=F32)
            return z + (bias_ref[h, i - js[t]] + negs[t] if near else negs[t])

        zs = {(t, h): logits(t, h) for h in range(A_HEADS) for t in range(len(js))}
        ms_new = []
        for h in range(A_HEADS):
            zh = [zs.pop((t, h)) for t in range(len(js))]
            m_new = jnp.maximum(ms[h], jnp.max(functools.reduce(jnp.maximum, zh), axis=1, keepdims=True))
            alpha = jnp.exp(ms[h] - m_new)
            ms_new.append(m_new)
            acc = alpha * acc_ref[h]
            for t, z in enumerate(zh):
                acc = acc + jnp.dot(jnp.exp(z - m_new).astype(BF16), av1_ref[js[t], :, heads1[h]],
                                    preferred_element_type=F32)
            acc_ref[h] = acc
        return ties_seen, tuple(ms_new)

    init = (jnp.zeros((1, BLK), F32), tuple(jnp.full((BLK, 1), -BIG, F32) for _ in heads))
    n_far = jnp.maximum(i - 1, 0)
    carry = lax.fori_loop(0, jnp.right_shift(n_far, 1),
                          lambda t, c: attend((2 * t, 2 * t + 1), c, False), init)
    carry = lax.cond((n_far & 1) == 1, lambda c: attend((n_far - 1,), c, False), lambda c: c, carry)
    lax.fori_loop(n_far, i + 1, lambda j, c: attend((j,), c, True), carry)

    valid_q = (lax.broadcasted_iota(jnp.int32, (BLK, 1), 0) + i * BLK) >= LEAD
    outs = []
    for h in range(A_HEADS):
        acc = acc_ref[h]
        outs.append((acc / pltpu.roll(acc, A_HEAD_DIM, 1))[:, :A_HEAD_DIM])
    o_ref[...] = jnp.where(valid_q, jnp.concatenate(outs, axis=1), 0.0).astype(o_ref.dtype)


def _dsa_attention(p0, ikw, bias, nbatch, nb, n_keep):
    rows = p0.shape[0]
    aw = A_HEADS * A_HEAD_DIM
    p0b = p0.reshape(nbatch, nb, BLK, p0.shape[1])
    ikwb = ikw.reshape(nbatch, nb, BLK, ikw.shape[1])
    return pl.pallas_call(
        functools.partial(_dsa_kernel, n_keep=n_keep),
        out_shape=jax.ShapeDtypeStruct((rows, aw), BF16),
        grid=(nbatch, nb),
        in_specs=[pl.BlockSpec((BLK, aw), lambda b, i: (b * nb + i, 0)),
                  pl.BlockSpec((BLK, aw), lambda b, i: (b * nb + i, 3)),
                  pl.BlockSpec((BLK, 128), lambda b, i: (b * nb + i, 0)),
                  pl.BlockSpec((None, nb, BLK, aw), lambda b, i: (b, 0, 0, 1)),
                  pl.BlockSpec((None, nb, BLK, aw), lambda b, i: (b, 0, 0, 2)),
                  pl.BlockSpec((None, nb, BLK, 128), lambda b, i: (b, 0, 0, 0)),
                  pl.BlockSpec((A_HEADS, 2, BLK, BLK), lambda b, i: (0, 0, 0, 0))],
        out_specs=pl.BlockSpec((BLK, aw), lambda b, i: (b * nb + i, 0)),
        scratch_shapes=[pltpu.VMEM((nb, BLK, BLK), F32),
                        pltpu.VMEM((A_HEADS, BLK, 2 * A_HEAD_DIM), F32),
                        pltpu.VMEM((nb, BLK, 2 * aw), BF16)],
        compiler_params=_params("parallel", "arbitrary"),
        name="dsa_attention",
    )(p0, p0, ikw, p0b, p0b, ikwb, bias)


def _retention_kernel(bq_ref, bk_ref, bv_ref, bg_ref, cos_ref, sin_ref, dmat_ref, xi_ref, zeta_ref,
                      gain_ref, o_ref, r_ref, *, g_chunk):
    @pl.when(pl.program_id(1) == 0)
    def _():
        r_ref[...] = jnp.zeros_like(r_ref)

    cosf = cos_ref[...]
    sinf = sin_ref[...]

    def rot(x):
        return x * cosf + pltpu.roll(x, B_QK_DIM // 2, 1) * sinf

    for h in range(B_HEADS):
        ks = slice(h * B_QK_DIM, (h + 1) * B_QK_DIM)
        vs = slice(h * B_V_DIM, (h + 1) * B_V_DIM)
        q = rot(bq_ref[:, ks].astype(F32))
        k = rot(bk_ref[:, ks].astype(F32)) * (B_QK_DIM ** -0.5)
        qb = q.astype(BF16)
        kb = k.astype(BF16)
        v = bv_ref[:, vs]
        inner = lax.dot_general(qb, kb, NT_DIMS, preferred_element_type=F32) * dmat_ref[h]
        r_old = r_ref[h]
        o = (jnp.dot(inner.astype(BF16), v, preferred_element_type=F32)
             + jnp.dot(qb, r_old.astype(BF16), preferred_element_type=F32) * xi_ref[h])
        kz = (k * zeta_ref[h]).astype(BF16)
        r_ref[h] = r_old * g_chunk[h] + lax.dot_general(kz, v, TN_DIMS, preferred_element_type=F32)
        mu = jnp.mean(o, axis=-1, keepdims=True)
        oc = o - mu
        var = jnp.mean(oc * oc, axis=-1, keepdims=True)
        rn = oc * lax.rsqrt(var + EPS) * gain_ref[:, vs]
        gate = bg_ref[:, vs].astype(F32)
        o_ref[:, vs] = (rn * (gate / (1.0 + jnp.exp(-gate)))).astype(o_ref.dtype)


def _retention(p0, gn_gain, nbatch, nb):
    rows = p0.shape[0]
    qkw, vw = B_HEADS * B_QK_DIM, B_HEADS * B_V_DIM
    frame = nb * BLK
    half = B_QK_DIM // 2
    pos = (jnp.arange(frame) - LEAD).astype(F32)
    inv = 1.0 / (ROPE_BASE ** (jnp.arange(half, dtype=F32) / half))
    ang = pos[:, None] * inv[None, :]
    cosf = jnp.concatenate([jnp.cos(ang), jnp.cos(ang)], axis=-1)
    sinf = jnp.concatenate([-jnp.sin(ang), jnp.sin(ang)], axis=-1)
    lg = jnp.log(1.0 - 2.0 ** (-5.0 - jnp.arange(B_HEADS, dtype=F32)))
    n = jnp.arange(BLK, dtype=F32)
    diff = n[:, None] - n[None, :]
    dmat = jnp.where(diff[None] >= 0, jnp.exp(jnp.maximum(diff, 0.0)[None] * lg[:, None, None]), 0.0)
    xi = jnp.broadcast_to(jnp.exp((n[None, :] + 1.0) * lg[:, None])[..., None], (B_HEADS, BLK, B_V_DIM))
    zeta = jnp.broadcast_to(jnp.exp((BLK - 1.0 - n[None, :]) * lg[:, None])[..., None],
                            (B_HEADS, BLK, B_QK_DIM))
    g_chunk = tuple(float(math.exp(BLK * math.log(1.0 - 2.0 ** (-5.0 - h)))) for h in range(B_HEADS))
    const = lambda shape: pl.BlockSpec(shape, lambda b, i: (0,) * len(shape))
    return pl.pallas_call(
        functools.partial(_retention_kernel, g_chunk=g_chunk),
        out_shape=jax.ShapeDtypeStruct((rows, vw), BF16),
        grid=(nbatch, nb),
        in_specs=[pl.BlockSpec((BLK, qkw), lambda b, i: (b * nb + i, 4)),
                  pl.BlockSpec((BLK, qkw), lambda b, i: (b * nb + i, 5)),
                  pl.BlockSpec((BLK, vw), lambda b, i: (b * nb + i, 3)),
                  pl.BlockSpec((BLK, vw), lambda b, i: (b * nb + i, 4)),
                  pl.BlockSpec((BLK, B_QK_DIM), lambda b, i: (i, 0)),
                  pl.BlockSpec((BLK, B_QK_DIM), lambda b, i: (i, 0)),
                  const((B_HEADS, BLK, BLK)),
                  const((B_HEADS, BLK, B_V_DIM)),
                  const((B_HEADS, BLK, B_QK_DIM)),
                  const((1, vw))],
        out_specs=pl.BlockSpec((BLK, vw), lambda b, i: (b * nb + i, 0)),
        scratch_shapes=[pltpu.VMEM((B_HEADS, B_QK_DIM, B_V_DIM), F32)],
        compiler_params=_params("parallel", "arbitrary"),
        name="retention",
    )(p0, p0, p0, p0, cosf, sinf, dmat, xi, zeta, gn_gain.reshape(1, vw).astype(F32))


def _proj_residual_kernel(*refs, n_pairs, final):
    x = refs[2 * n_pairs][...]
    for t in range(n_pairs):
        x = x + jnp.dot(refs[2 * t][...], refs[2 * t + 1][...], preferred_element_type=F32)
    if final:
        ng_ref, o_ref = refs[2 * n_pairs + 1:]
        ms = jnp.mean(x * x, axis=-1, keepdims=True)
        o_ref[...] = x * lax.rsqrt(ms + EPS) * ng_ref[...]
    else:
        refs[2 * n_pairs + 1][...] = x


def _proj_residual(pairs, h, tm, final=None):
    rows, d = h.shape
    in_specs, args = [], []
    for a, w, *blk in pairs:
        kb, ac, wr = blk if blk else (a.shape[1], 0, 0)
        in_specs += [pl.BlockSpec((tm, kb), lambda i, ac=ac: (i, ac)),
                     pl.BlockSpec((kb, d), lambda i, wr=wr: (wr, 0))]
        args += [a, w]
    in_specs.append(pl.BlockSpec((tm, d), lambda i: (i, 0)))
    args.append(h)
    if final is None:
        out_shape = jax.ShapeDtypeStruct((rows, d), F32)
        out_spec = pl.BlockSpec((tm, d), lambda i: (i, 0))
        sem = "parallel"
    else:
        gain, nbatch, nb = final
        assert tm == BLK
        in_specs.append(pl.BlockSpec((1, d), lambda i: (0, 0)))
        args.append(gain.reshape(1, d).astype(F32))
        out_shape = jax.ShapeDtypeStruct((nbatch, (nb - 1) * BLK, d), F32)
        out_spec = pl.BlockSpec((None, BLK, d), lambda i: (i // nb, jnp.maximum(i % nb - 1, 0), 0))
        sem = "arbitrary"
    return pl.pallas_call(
        functools.partial(_proj_residual_kernel, n_pairs=len(pairs), final=final is not None),
        out_shape=out_shape,
        grid=(rows // tm,),
        in_specs=in_specs,
        out_specs=out_spec,
        compiler_params=_params(sem),
        name="proj_residual",
    )(*args)


GATE_COLS = 256


def _norm_gate_kernel(x_ref, gn_ref, wu_ref, wg_ref, cw_ref, cb_ref, o_ref, halo_ref):
    @pl.when(pl.program_id(0) == 0)
    def _():
        halo_ref[...] = jnp.zeros_like(halo_ref)

    x = x_ref[...]
    ms = jnp.mean(x * x, axis=-1, keepdims=True)
    xn = (x * lax.rsqrt(ms + EPS) * gn_ref[...]).astype(BF16)
    tm = x.shape[0]
    row8 = lax.broadcasted_iota(jnp.int32, (8, GATE_COLS), 0)
    for c in range(o_ref.shape[1] // GATE_COLS):
        cs = slice(c * GATE_COLS, (c + 1) * GATE_COLS)
        u = jnp.dot(xn, wu_ref[:, cs], preferred_element_type=F32)
        g = jnp.dot(xn, wg_ref[:, cs], preferred_element_type=F32)
        prev = halo_ref[:, cs]
        halo_ref[:, cs] = g[tm - 8:, :]
        g1 = pltpu.roll(g, 1, 0)
        g2 = pltpu.roll(g, 2, 0)
        g1 = jnp.concatenate([jnp.where(row8 >= 1, g1[:8], prev[7:8]), g1[8:]], axis=0)
        g2 = jnp.concatenate([jnp.where(row8 >= 2, g2[:8], jnp.where(row8 == 1, prev[7:8], prev[6:7])),
                              g2[8:]], axis=0)
        gc = g2 * cw_ref[0:1, cs] + g1 * cw_ref[1:2, cs] + g * cw_ref[2:3, cs] + cb_ref[:, cs]
        o_ref[:, cs] = ((gc / (1.0 + jnp.exp(-gc))) * u).astype(o_ref.dtype)


def _norm_gated_up(x, gain, w_up, w_gate, conv_w, conv_b, tm):
    rows, d = x.shape
    dff = w_up.shape[1]
    assert dff % GATE_COLS == 0
    const = lambda shape: pl.BlockSpec(shape, lambda i: (0, 0))
    return pl.pallas_call(
        _norm_gate_kernel,
        out_shape=jax.ShapeDtypeStruct((rows, dff), BF16),
        grid=(rows // tm,),
        in_specs=[pl.BlockSpec((tm, d), lambda i: (i, 0)),
                  const((1, d)), const((d, dff)), const((d, dff)), const((CONV_WIDTH, dff)), const((1, dff))],
        out_specs=pl.BlockSpec((tm, dff), lambda i: (i, 0)),
        scratch_shapes=[pltpu.VMEM((8, dff), F32)],
        compiler_params=_params("arbitrary"),
        name="norm_gated_up",
    )(x, gain.reshape(1, d), w_up.astype(BF16), w_gate.astype(BF16), conv_w.astype(F32),
      conv_b.reshape(1, dff).astype(F32))


SB_GROUP = 8
SB_DEAD = 110.0


def _stick_breaking_kernel(q_ref, k_ref, v_ref, o_ref, acc_ref):
    i = pl.program_id(2)
    row = lax.broadcasted_iota(jnp.int32, (BLK, BLK), 0)
    col = lax.broadcasted_iota(jnp.int32, (BLK, BLK), 1)
    tri_after = (row > col).astype(BF16)
    scale = C_HEAD_DIM ** -0.5

    heads = [slice(h * C_HEAD_DIM, (h + 1) * C_HEAD_DIM) for h in range(SB_GROUP)]
    q_heads = [q_ref[:, hs] * scale for hs in heads]
    acc_ref[...] = jnp.zeros_like(acc_ref)

    def tiles(blocks, carries):
        keys = [(b, h) for b in range(len(blocks)) for h in range(SB_GROUP)]
        zs = {(b, h): lax.dot_general(q_heads[h], k_ref[blocks[b][0], :, heads[h]], NT_DIMS,
                                      preferred_element_type=F32) for b, h in keys}
        log_beta, rests, sums = {}, {}, {}
        for b, h in keys:
            z, mask = zs[b, h], blocks[b][1]
            sp = jnp.maximum(z, 0.0) + jnp.log(1.0 + jnp.exp2(jnp.abs(z) * -LOG2E))
            if mask is not None:
                sp = jnp.where(mask, sp, 0.0)
            log_beta[b, h] = z - sp
            sums[b, h] = jnp.sum(sp, axis=1, keepdims=True)
            rests[b, h] = jnp.dot(sp.astype(BF16), tri_after, preferred_element_type=F32)
        run = list(carries)
        for b, h in keys:
            j, mask = blocks[b]
            a = jnp.exp2((log_beta[b, h] - (rests[b, h] + run[h])) * LOG2E)
            if mask is not None:
                a = jnp.where(mask, a, 0.0)
            acc_ref[h] += jnp.dot(a.astype(BF16), v_ref[j, :, heads[h]], preferred_element_type=F32)
            run[h] = run[h] + sums[b, h]
        return tuple(run)

    def alive(carries):
        return jnp.min(functools.reduce(jnp.minimum, carries)) < SB_DEAD

    qf = row + i * BLK
    kf = col + i * BLK
    diag = (kf < qf) & (kf >= LEAD)
    carries = lax.cond(
        i >= 2, lambda c: tiles([(i, diag), (i - 1, None)], c),
        lambda c: lax.cond(i == 1, lambda c1: tiles([(1, diag), (0, col >= LEAD)], c1),
                           lambda c1: tiles([(0, diag)], c1), c),
        tuple(jnp.zeros((BLK, 1), F32) for _ in heads))
    n_rest = jnp.maximum(i - 2, 0)
    n_pairs = jnp.right_shift(n_rest, 1)
    _, carries = lax.while_loop(
        lambda s: (s[0] < n_pairs) & alive(s[1]),
        lambda s: (s[0] + 1, tiles([(i - 2 - 2 * s[0], None), (i - 3 - 2 * s[0], None)], s[1])),
        (jnp.int32(0), carries))
    carries = lax.cond(((n_rest & 1) == 1) & alive(carries), lambda c: tiles([(1, None)], c), lambda c: c,
                       carries)

    @pl.when((i >= 2) & alive(carries))
    def _():
        tiles([(0, col >= LEAD)], carries)

    o_ref[...] = jnp.concatenate([acc_ref[h] for h in range(SB_GROUP)], axis=1).astype(o_ref.dtype)


def _stick_breaking(p1, nbatch, nb):
    rows = p1.shape[0]
    cw = C_HEADS * C_HEAD_DIM
    gw = SB_GROUP * C_HEAD_DIM
    ngroups = C_HEADS // SB_GROUP
    p1b = p1.reshape(nbatch, nb, BLK, p1.shape[1])
    return pl.pallas_call(
        _stick_breaking_kernel,
        out_shape=jax.ShapeDtypeStruct((rows, cw), BF16),
        grid=(nbatch, ngroups, nb),
        in_specs=[pl.BlockSpec((BLK, gw), lambda b, g, i: (b * nb + i, g)),
                  pl.BlockSpec((None, nb, BLK, gw), lambda b, g, i: (b, 0, 0, ngroups + g)),
                  pl.BlockSpec((None, nb, BLK, gw), lambda b, g, i: (b, 0, 0, 2 * ngroups + g))],
        out_specs=pl.BlockSpec((BLK, gw), lambda b, g, i: (b * nb + i, g)),
        scratch_shapes=[pltpu.VMEM((SB_GROUP, BLK, C_HEAD_DIM), F32)],
        compiler_params=_params("parallel", "parallel", "arbitrary"),
        name="stick_breaking",
    )(p1, p1b, p1b)


def _pick_tile(rows, pref):
    t = pref
    while rows % t:
        t //= 2
    return t


def _pick_cols(n, cap=2816):
    return max(t for t in range(128, min(n, cap) + 1, 128) if n % t == 0)


def kernel(x, meta_tokens, rel_bias, norm_mix, norm_ffn, norm_final, even_w_in, even_gn_gain, even_w_out, odd_w_in, odd_w_out, ffn_w_up, ffn_w_gate, ffn_conv_w, ffn_conv_b, ffn_w_down):
    nbatch, seq, d = x.shape
    assert seq % BLK == 0
    nb = seq // BLK + 1
    rows = nbatch * nb * BLK
    n_keep = min(TOPK_MAX, seq // 4)
    assert n_keep >= N_META
    depth = norm_mix.shape[0]
    tm = _pick_tile(rows, 1024)

    meta = jnp.broadcast_to(meta_tokens[None].astype(x.dtype), (nbatch, N_META, d))
    h = jnp.concatenate([jnp.zeros((nbatch, LEAD, d), x.dtype), meta, x], axis=1).reshape(rows, d)

    aw = A_HEADS * A_HEAD_DIM
    qkw, vw = B_HEADS * B_QK_DIM, B_HEADS * B_V_DIM
    cw = C_HEADS * C_HEAD_DIM
    bias = _bias_tiles(rel_bias)

    for l in range(depth):
        j = l // 2
        if l % 2 == 0:
            w = even_w_in[j]
            o_iq, o_ik, o_iw, o_bq = 3 * aw, 4 * aw, 4 * aw + IDX_DIM, 4 * aw + IDX_DIM + IDX_HEADS
            w_main = jnp.concatenate([w[:, :4 * aw], w[:, o_bq:]], axis=1).astype(BF16)
            w_idx = jnp.concatenate([w[:, o_ik:o_bq], jnp.zeros((d, 128 - IDX_DIM - IDX_HEADS), w.dtype)],
                                    axis=1).astype(BF16)
            p0 = _norm_matmul(h, norm_mix[l], w_main, tm, _pick_cols(w_main.shape[1]))
            ikw = _norm_matmul(h, norm_mix[l], w_idx, tm, 128)
            a_out = _dsa_attention(p0, ikw, bias, nbatch, nb, n_keep)
            r_out = _retention(p0, even_gn_gain[j], nbatch, nb)
            w_out = even_w_out[j].astype(BF16)
            h = _proj_residual([(a_out, w_out, aw, 0, 0)]
                               + [(r_out, w_out, aw, t, t + 1) for t in range(vw // aw)], h, _pick_tile(rows, 512))
        else:
            p1 = _norm_matmul(h, norm_mix[l], odd_w_in[j].astype(BF16), tm, _pick_cols(odd_w_in.shape[2]))
            s_out = _stick_breaking(p1, nbatch, nb)
            h = _proj_residual([(s_out, odd_w_out[j].astype(BF16))], h, _pick_tile(rows, 512))
        act = _norm_gated_up(h, norm_ffn[l], ffn_w_up[l], ffn_w_gate[l], ffn_conv_w[l], ffn_conv_b[l], tm)
        w_down = ffn_w_down[l].astype(BF16)
        if l + 1 < depth:
            h = _proj_residual([(act, w_down)], h, _pick_tile(rows, 512))
        else:
            h = _proj_residual([(act, w_down)], h, BLK, final=(norm_final, nbatch, nb))
    return h
```

```python
import functools
import math

import jax
import jax.numpy as jnp
import numpy as np
from jax import lax
from jax.experimental import pallas as pl
from jax.experimental.pallas import tpu as pltpu

N_META = 16
BLK = 256
LEAD = BLK - N_META
A_HEADS, A_HEAD_DIM = 8, 64
IDX_HEADS, IDX_DIM = 8, 64
TOPK_MAX = 256
N_BUCKETS, MAX_DISTANCE = 32, 128
B_HEADS, B_QK_DIM, B_V_DIM = 4, 128, 256
ROPE_BASE = 10000.0
C_HEADS, C_HEAD_DIM = 16, 64
CONV_WIDTH = 3
EPS = 1e-6
BIG = 1e30
LOG2E = 1.4426950408889634
BISECT_ITERS = 40
PEEL_SLACK = 7.0
VMEM_LIMIT_BYTES = 56 * 1024 * 1024

F32 = jnp.float32
BF16 = jnp.bfloat16
NT_DIMS = (((1,), (1,)), ((), ()))
TN_DIMS = (((0,), (0,)), ((), ()))


def _params(*sem):
    return pltpu.CompilerParams(dimension_semantics=sem, vmem_limit_bytes=VMEM_LIMIT_BYTES)


def _norm_matmul_kernel(x_ref, g_ref, w_ref, o_ref, xn_ref):
    @pl.when(pl.program_id(1) == 0)
    def _():
        x = x_ref[...]
        ms = jnp.mean(x * x, axis=-1, keepdims=True)
        xn_ref[...] = (x * lax.rsqrt(ms + EPS) * g_ref[...]).astype(BF16)

    o_ref[...] = jnp.dot(xn_ref[...], w_ref[...], preferred_element_type=F32).astype(o_ref.dtype)


def _norm_matmul(x, gain, w, tm, tn):
    rows, d = x.shape
    n = w.shape[1]
    return pl.pallas_call(
        _norm_matmul_kernel,
        out_shape=jax.ShapeDtypeStruct((rows, n), BF16),
        grid=(rows // tm, n // tn),
        in_specs=[pl.BlockSpec((tm, d), lambda i, j: (i, 0)),
                  pl.BlockSpec((1, d), lambda i, j: (0, 0)),
                  pl.BlockSpec((d, tn), lambda i, j: (0, j))],
        out_specs=pl.BlockSpec((tm, tn), lambda i, j: (i, j)),
        scratch_shapes=[pltpu.VMEM((tm, d), BF16)],
        compiler_params=_params("parallel", "arbitrary"),
        name="norm_matmul",
    )(x, gain.reshape(1, d), w)


def _bucket_tiles():
    q = np.arange(BLK)[:, None]
    k = np.arange(BLK)[None, :]
    tiles = []
    for t in range(2):
        n = np.maximum(q - k + t * BLK, 0)
        max_exact = N_BUCKETS // 2
        large = max_exact + (np.log(np.maximum(n, 1).astype(np.float32) / max_exact)
                             / math.log(MAX_DISTANCE / max_exact)
                             * (N_BUCKETS - max_exact)).astype(np.int32)
        large = np.minimum(large, N_BUCKETS - 1)
        tiles.append(np.where(n < max_exact, n, large).astype(np.int32))
    return np.stack(tiles)


def _bias_tiles_kernel(rb_ref, idx_ref, o_ref):
    h = pl.program_id(0)
    far = rb_ref[N_BUCKETS - 1, h]
    for t in range(2):
        idx = idx_ref[t]
        acc = jnp.zeros((BLK, BLK), F32)
        for b in range(N_BUCKETS):
            acc = jnp.where(idx == b, rb_ref[b, h] - far, acc)
        o_ref[t] = acc


def _bias_tiles(rel_bias):
    assert BLK + 1 >= MAX_DISTANCE
    return pl.pallas_call(
        _bias_tiles_kernel,
        out_shape=jax.ShapeDtypeStruct((A_HEADS, 2, BLK, BLK), F32),
        grid=(A_HEADS,),
        in_specs=[pl.BlockSpec(memory_space=pltpu.SMEM),
                  pl.BlockSpec((2, BLK, BLK), lambda h: (0, 0, 0))],
        out_specs=pl.BlockSpec((None, 2, BLK, BLK), lambda h: (h, 0, 0, 0)),
        compiler_params=_params("parallel"),
        name="t5_bias_tiles",
    )(rel_bias.astype(F32), jnp.asarray(_bucket_tiles()))


def _dsa_kernel(aq_ref, iq_ref, ikwq_ref, ak_ref, av_ref, ikw_ref, bias_ref, o_ref,
                s_ref, acc_ref, av1_ref, *, n_keep):
    i = pl.program_id(1)
    row = lax.broadcasted_iota(jnp.int32, (BLK, BLK), 0)
    col = lax.broadcasted_iota(jnp.int32, (BLK, BLK), 1)
    qf = col + i * BLK
    heads = [slice(h * A_HEAD_DIM, (h + 1) * A_HEAD_DIM) for h in range(A_HEADS)]
    heads1 = [slice(h * 2 * A_HEAD_DIM, (h + 1) * 2 * A_HEAD_DIM) for h in range(A_HEADS)]

    @pl.when(i == 0)
    def _():
        def fill(j, _):
            ones = jnp.ones((BLK, A_HEAD_DIM), BF16)
            av1_ref[j] = jnp.concatenate([x for hs in heads for x in (av_ref[j, :, hs], ones)], axis=1)
            return 0
        lax.fori_loop(0, av_ref.shape[0], fill, 0)

    s_ref[0] = jnp.where((row >= LEAD) & (row <= qf), BIG, -BIG)
    eye = (lax.broadcasted_iota(jnp.int32, (128, 128), 0)
           == lax.broadcasted_iota(jnp.int32, (128, 128), 1)).astype(BF16)
    ikwq_t = lax.dot_general(eye, ikwq_ref[...], NT_DIMS, preferred_element_type=F32)
    w_scale = (IDX_DIM ** -0.5) * (IDX_HEADS ** -0.5)
    w_rows = [ikwq_t[IDX_DIM + h:IDX_DIM + h + 1, :] * w_scale for h in range(IDX_HEADS)]
    iq_heads = [iq_ref[:, h * IDX_DIM:(h + 1) * IDX_DIM] for h in range(IDX_HEADS)]

    def score_blocks(js, mabs):
        accs = [jnp.zeros((BLK, BLK), F32) for _ in js]
        for h in range(IDX_HEADS):
            for t, j in enumerate(js):
                st = lax.dot_general(ikw_ref[j, :, :IDX_DIM], iq_heads[h], NT_DIMS, preferred_element_type=F32)
                accs[t] = accs[t] + jnp.maximum(st, 0.0) * w_rows[h]
        for j, acc in zip(js, accs):
            causal = (row + j * BLK) <= qf
            s_ref[j] = jnp.where(causal, acc, -BIG)
            mabs = jnp.maximum(mabs, jnp.max(jnp.where(causal, jnp.abs(acc), 0.0), axis=0, keepdims=True))
        return mabs

    mabs = lax.fori_loop(0, jnp.right_shift(i, 1), lambda t, m: score_blocks((2 * t + 1, 2 * t + 2), m),
                         jnp.zeros((1, BLK), F32))
    mabs = lax.cond((i & 1) == 1, lambda m: score_blocks((i,), m), lambda m: m, mabs)

    k_eff = float(n_keep - N_META)
    bound = mabs * 1.000001 + 1e-30

    def count_ge(thr):
        def body(j, c):
            hit = (s_ref[j] >= thr).astype(F32).reshape(4, BLK // 32, 8, BLK)
            return c + jnp.sum(hit, axis=1)
        c = lax.fori_loop(1, i + 1, body, jnp.zeros((4, 8, BLK), F32))
        return jnp.sum(jnp.sum(c, axis=0), axis=0, keepdims=True)

    def unsettled(state):
        it, _, _, cnt_lo = state
        return (it < BISECT_ITERS) & (jnp.max(cnt_lo) > k_eff + PEEL_SLACK)

    def halve(lo, hi, cnt_lo):
        mid = lo + (hi - lo) * 0.5
        cnt = count_ge(mid)
        ge = cnt >= k_eff
        return jnp.where(ge, mid, lo), jnp.where(ge, hi, mid), jnp.where(ge, cnt, cnt_lo)

    def bisect(state):
        it, lo, hi, cnt_lo = state
        return (it + 2,) + halve(*halve(lo, hi, cnt_lo))

    _, lo, _, _ = lax.while_loop(unsettled, bisect, (jnp.int32(0), -bound, bound, count_ge(-bound)))

    def vmin_body(j, v):
        s = s_ref[j]
        return jnp.minimum(v, jnp.min(jnp.where(s >= lo, s, BIG), axis=0, keepdims=True))

    v = lax.fori_loop(1, i + 1, vmin_body, jnp.full((1, BLK), BIG, F32))
    v = jnp.where(v >= BIG, 0.0, v)

    def above(x):
        def body(j, carry):
            cnt, nxt = carry
            s = s_ref[j]
            gt = s > x
            cnt = cnt + jnp.sum(gt.astype(F32).reshape(4, BLK // 32, 8, BLK), axis=1)
            nxt = jnp.minimum(nxt, jnp.min(jnp.where(gt, s, BIG).reshape(4, BLK // 32, 8, BLK), axis=1))
            return cnt, nxt
        cnt, nxt = lax.fori_loop(1, i + 1, body, (jnp.zeros((4, 8, BLK), F32), jnp.full((4, 8, BLK), BIG, F32)))
        return (jnp.sum(jnp.sum(cnt, axis=0), axis=0, keepdims=True),
                jnp.min(jnp.min(nxt, axis=0), axis=0, keepdims=True))

    def peel(state):
        x, cnt_gt, nxt = state
        x = jnp.where(cnt_gt >= k_eff, nxt, x)
        return (x,) + above(x)

    v, cnt_gt, _ = lax.while_loop(lambda s: jnp.max(s[1]) >= k_eff, peel, (v,) + above(v))
    ties_wanted = k_eff - cnt_gt

    acc_ref[...] = jnp.zeros_like(acc_ref)
    tri_incl = (col <= row).astype(BF16)
    aq_heads = [aq_ref[:, hs] * (A_HEAD_DIM ** -0.5) for hs in heads]

    def attend(js, carry, near):
        ties_seen, ms = carry
        negs = []
        for j in js:
            s = s_ref[j]
            eq = s == v
            rank = ties_seen + jnp.dot(tri_incl, eq.astype(BF16), preferred_element_type=F32)
            sel = (s > v) | (eq & (rank <= ties_wanted))
            negs.append(jnp.where(sel, 0.0, -BIG).T)
            ties_seen = rank[BLK - 1:BLK, :]

        def logits(t, h):
            z = lax.dot_general(aq_heads[h], ak_ref[js[t], :, heads[h]], NT_DIMS, preferred_element_type=F32)
            return z + (bias_ref[h, i - js[t]] + negs[t] if near else negs[t])

        zs = {(t, h): logits(t, h) for h in range(A_HEADS) for t in range(len(js))}
        ms_new = []
        for h in range(A_HEADS):
            zh = [zs.pop((t, h)) for t in range(len(js))]
            m_new = jnp.maximum(ms[h], jnp.max(functools.reduce(jnp.maximum, zh), axis=1, keepdims=True))
            alpha = jnp.exp(ms[h] - m_new)
            ms_new.append(m_new)
            acc = alpha * acc_ref[h]
            for t, z in enumerate(zh):
                acc = acc + jnp.dot(jnp.exp(z - m_new).astype(BF16), av1_ref[js[t], :, heads1[h]],
                                    preferred_element_type=F32)
            acc_ref[h] = acc
        return ties_seen, tuple(ms_new)

    init = (jnp.zeros((1, BLK), F32), tuple(jnp.full((BLK, 1), -BIG, F32) for _ in heads))
    n_far = jnp.maximum(i - 1, 0)
    carry = lax.fori_loop(0, jnp.right_shift(n_far, 1),
                          lambda t, c: attend((2 * t, 2 * t + 1), c, False), init)
    carry = lax.cond((n_far & 1) == 1, lambda c: attend((n_far - 1,), c, False), lambda c: c, carry)

    @pl.when(i >= 1)
    def _():
        attend((i - 1, i), carry, True)

    @pl.when(i == 0)
    def _():
        attend((0,), carry, True)

    valid_q = (lax.broadcasted_iota(jnp.int32, (BLK, 1), 0) + i * BLK) >= LEAD
    outs = []
    for h in range(A_HEADS):
        acc = acc_ref[h]
        outs.append((acc / pltpu.roll(acc, A_HEAD_DIM, 1))[:, :A_HEAD_DIM])
    o_ref[...] = jnp.where(valid_q, jnp.concatenate(outs, axis=1), 0.0).astype(o_ref.dtype)


def _dsa_attention(p0, ikw, bias, nbatch, nb, n_keep):
    rows = p0.shape[0]
    aw = A_HEADS * A_HEAD_DIM
    p0b = p0.reshape(nbatch, nb, BLK, p0.shape[1])
    ikwb = ikw.reshape(nbatch, nb, BLK, ikw.shape[1])
    return pl.pallas_call(
        functools.partial(_dsa_kernel, n_keep=n_keep),
        out_shape=jax.ShapeDtypeStruct((rows, aw), BF16),
        grid=(nbatch, nb),
        in_specs=[pl.BlockSpec((BLK, aw), lambda b, i: (b * nb + i, 0)),
                  pl.BlockSpec((BLK, aw), lambda b, i: (b * nb + i, 3)),
                  pl.BlockSpec((BLK, 128), lambda b, i: (b * nb + i, 0)),
                  pl.BlockSpec((None, nb, BLK, aw), lambda b, i: (b, 0, 0, 1)),
                  pl.BlockSpec((None, nb, BLK, aw), lambda b, i: (b, 0, 0, 2)),
                  pl.BlockSpec((None, nb, BLK, 128), lambda b, i: (b, 0, 0, 0)),
                  pl.BlockSpec((A_HEADS, 2, BLK, BLK), lambda b, i: (0, 0, 0, 0))],
        out_specs=pl.BlockSpec((BLK, aw), lambda b, i: (b * nb + i, 0)),
        scratch_shapes=[pltpu.VMEM((nb, BLK, BLK), F32),
                        pltpu.VMEM((A_HEADS, BLK, 2 * A_HEAD_DIM), F32),
                        pltpu.VMEM((nb, BLK, 2 * aw), BF16)],
        compiler_params=_params("parallel", "arbitrary"),
        name="dsa_attention",
    )(p0, p0, ikw, p0b, p0b, ikwb, bias)


def _retention_kernel(bq_ref, bk_ref, bv_ref, bg_ref, cos_ref, sin_ref, dmat_ref, xi_ref, zeta_ref,
                      gain_ref, o_ref, r_ref, *, g_chunk):
    @pl.when(pl.program_id(1) == 0)
    def _():
        r_ref[...] = jnp.zeros_like(r_ref)

    cosf = cos_ref[...]
    sinf = sin_ref[...]

    def rot(x):
        return x * cosf + pltpu.roll(x, B_QK_DIM // 2, 1) * sinf

    for h in range(B_HEADS):
        ks = slice(h * B_QK_DIM, (h + 1) * B_QK_DIM)
        vs = slice(h * B_V_DIM, (h + 1) * B_V_DIM)
        q = rot(bq_ref[:, ks].astype(F32))
        k = rot(bk_ref[:, ks].astype(F32)) * (B_QK_DIM ** -0.5)
        qb = q.astype(BF16)
        kb = k.astype(BF16)
        v = bv_ref[:, vs]
        inner = lax.dot_general(qb, kb, NT_DIMS, preferred_element_type=F32) * dmat_ref[h]
        r_old = r_ref[h]
        o = (jnp.dot(inner.astype(BF16), v, preferred_element_type=F32)
             + jnp.dot(qb, r_old.astype(BF16), preferred_element_type=F32) * xi_ref[h])
        kz = (k * zeta_ref[h]).astype(BF16)
        r_ref[h] = r_old * g_chunk[h] + lax.dot_general(kz, v, TN_DIMS, preferred_element_type=F32)
        mu = jnp.mean(o, axis=-1, keepdims=True)
        oc = o - mu
        var = jnp.mean(oc * oc, axis=-1, keepdims=True)
        rn = oc * lax.rsqrt(var + EPS) * gain_ref[:, vs]
        gate = bg_ref[:, vs].astype(F32)
        o_ref[:, vs] = (rn * (gate / (1.0 + jnp.exp(-gate)))).astype(o_ref.dtype)


def _retention(p0, gn_gain, nbatch, nb):
    rows = p0.shape[0]
    qkw, vw = B_HEADS * B_QK_DIM, B_HEADS * B_V_DIM
    frame = nb * BLK
    half = B_QK_DIM // 2
    pos = (jnp.arange(frame) - LEAD).astype(F32)
    inv = 1.0 / (ROPE_BASE ** (jnp.arange(half, dtype=F32) / half))
    ang = pos[:, None] * inv[None, :]
    cosf = jnp.concatenate([jnp.cos(ang), jnp.cos(ang)], axis=-1)
    sinf = jnp.concatenate([-jnp.sin(ang), jnp.sin(ang)], axis=-1)
    lg = jnp.log(1.0 - 2.0 ** (-5.0 - jnp.arange(B_HEADS, dtype=F32)))
    n = jnp.arange(BLK, dtype=F32)
    diff = n[:, None] - n[None, :]
    dmat = jnp.where(diff[None] >= 0, jnp.exp(jnp.maximum(diff, 0.0)[None] * lg[:, None, None]), 0.0)
    xi = jnp.broadcast_to(jnp.exp((n[None, :] + 1.0) * lg[:, None])[..., None], (B_HEADS, BLK, B_V_DIM))
    zeta = jnp.broadcast_to(jnp.exp((BLK - 1.0 - n[None, :]) * lg[:, None])[..., None],
                            (B_HEADS, BLK, B_QK_DIM))
    g_chunk = tuple(float(math.exp(BLK * math.log(1.0 - 2.0 ** (-5.0 - h)))) for h in range(B_HEADS))
    const = lambda shape: pl.BlockSpec(shape, lambda b, i: (0,) * len(shape))
    return pl.pallas_call(
        functools.partial(_retention_kernel, g_chunk=g_chunk),
        out_shape=jax.ShapeDtypeStruct((rows, vw), BF16),
        grid=(nbatch, nb),
        in_specs=[pl.BlockSpec((BLK, qkw), lambda b, i: (b * nb + i, 4)),
                  pl.BlockSpec((BLK, qkw), lambda b, i: (b * nb + i, 5)),
                  pl.BlockSpec((BLK, vw), lambda b, i: (b * nb + i, 3)),
                  pl.BlockSpec((BLK, vw), lambda b, i: (b * nb + i, 4)),
                  pl.BlockSpec((BLK, B_QK_DIM), lambda b, i: (i, 0)),
                  pl.BlockSpec((BLK, B_QK_DIM), lambda b, i: (i, 0)),
                  const((B_HEADS, BLK, BLK)),
                  const((B_HEADS, BLK, B_V_DIM)),
                  const((B_HEADS, BLK, B_QK_DIM)),
                  const((1, vw))],
        out_specs=pl.BlockSpec((BLK, vw), lambda b, i: (b * nb + i, 0)),
        scratch_shapes=[pltpu.VMEM((B_HEADS, B_QK_DIM, B_V_DIM), F32)],
        compiler_params=_params("parallel", "arbitrary"),
        name="retention",
    )(p0, p0, p0, p0, cosf, sinf, dmat, xi, zeta, gn_gain.reshape(1, vw).astype(F32))


def _proj_residual_kernel(*refs, n_pairs, final):
    x = refs[2 * n_pairs][...]
    for t in range(n_pairs):
        x = x + jnp.dot(refs[2 * t][...], refs[2 * t + 1][...], preferred_element_type=F32)
    if final:
        ng_ref, o_ref = refs[2 * n_pairs + 1:]
        ms = jnp.mean(x * x, axis=-1, keepdims=True)
        o_ref[...] = x * lax.rsqrt(ms + EPS) * ng_ref[...]
    else:
        refs[2 * n_pairs + 1][...] = x


def _proj_residual(pairs, h, tm, final=None):
    rows, d = h.shape
    in_specs, args = [], []
    for a, w, *blk in pairs:
        kb, ac, wr = blk if blk else (a.shape[1], 0, 0)
        in_specs += [pl.BlockSpec((tm, kb), lambda i, ac=ac: (i, ac)),
                     pl.BlockSpec((kb, d), lambda i, wr=wr: (wr, 0))]
        args += [a, w]
    in_specs.append(pl.BlockSpec((tm, d), lambda i: (i, 0)))
    args.append(h)
    if final is None:
        out_shape = jax.ShapeDtypeStruct((rows, d), F32)
        out_spec = pl.BlockSpec((tm, d), lambda i: (i, 0))
        sem = "parallel"
    else:
        gain, nbatch, nb = final
        assert tm == BLK
        in_specs.append(pl.BlockSpec((1, d), lambda i: (0, 0)))
        args.append(gain.reshape(1, d).astype(F32))
        out_shape = jax.ShapeDtypeStruct((nbatch, (nb - 1) * BLK, d), F32)
        out_spec = pl.BlockSpec((None, BLK, d), lambda i: (i // nb, jnp.maximum(i % nb - 1, 0), 0))
        sem = "arbitrary"
    return pl.pallas_call(
        functools.partial(_proj_residual_kernel, n_pairs=len(pairs), final=final is not None),
        out_shape=out_shape,
        grid=(rows // tm,),
        in_specs=in_specs,
        out_specs=out_spec,
        compiler_params=_params(sem),
        name="proj_residual",
    )(*args)


GATE_COLS = 256


def _norm_gate_kernel(x_ref, gn_ref, wu_ref, wg_ref, cw_ref, cb_ref, o_ref, halo_ref):
    @pl.when(pl.program_id(0) == 0)
    def _():
        halo_ref[...] = jnp.zeros_like(halo_ref)

    x = x_ref[...]
    ms = jnp.mean(x * x, axis=-1, keepdims=True)
    xn = (x * lax.rsqrt(ms + EPS) * gn_ref[...]).astype(BF16)
    tm = x.shape[0]
    row8 = lax.broadcasted_iota(jnp.int32, (8, GATE_COLS), 0)
    for c in range(o_ref.shape[1] // GATE_COLS):
        cs = slice(c * GATE_COLS, (c + 1) * GATE_COLS)
        u = jnp.dot(xn, wu_ref[:, cs], preferred_element_type=F32)
        g = jnp.dot(xn, wg_ref[:, cs], preferred_element_type=F32)
        prev = halo_ref[:, cs]
        halo_ref[:, cs] = g[tm - 8:, :]
        g1 = pltpu.roll(g, 1, 0)
        g2 = pltpu.roll(g, 2, 0)
        g1 = jnp.concatenate([jnp.where(row8 >= 1, g1[:8], prev[7:8]), g1[8:]], axis=0)
        g2 = jnp.concatenate([jnp.where(row8 >= 2, g2[:8], jnp.where(row8 == 1, prev[7:8], prev[6:7])),
                              g2[8:]], axis=0)
        gc = g2 * cw_ref[0:1, cs] + g1 * cw_ref[1:2, cs] + g * cw_ref[2:3, cs] + cb_ref[:, cs]
        o_ref[:, cs] = ((gc / (1.0 + jnp.exp(-gc))) * u).astype(o_ref.dtype)


def _norm_gated_up(x, gain, w_up, w_gate, conv_w, conv_b, tm):
    rows, d = x.shape
    dff = w_up.shape[1]
    assert dff % GATE_COLS == 0
    const = lambda shape: pl.BlockSpec(shape, lambda i: (0, 0))
    return pl.pallas_call(
        _norm_gate_kernel,
        out_shape=jax.ShapeDtypeStruct((rows, dff), BF16),
        grid=(rows // tm,),
        in_specs=[pl.BlockSpec((tm, d), lambda i: (i, 0)),
                  const((1, d)), const((d, dff)), const((d, dff)), const((CONV_WIDTH, dff)), const((1, dff))],
        out_specs=pl.BlockSpec((tm, dff), lambda i: (i, 0)),
        scratch_shapes=[pltpu.VMEM((8, dff), F32)],
        compiler_params=_params("arbitrary"),
        name="norm_gated_up",
    )(x, gain.reshape(1, d), w_up.astype(BF16), w_gate.astype(BF16), conv_w.astype(F32),
      conv_b.reshape(1, dff).astype(F32))


SB_GROUP = 8
SB_DEAD = 110.0


def _stick_breaking_kernel(q_ref, k_ref, v_ref, o_ref, acc_ref):
    i = pl.program_id(2)
    row = lax.broadcasted_iota(jnp.int32, (BLK, BLK), 0)
    col = lax.broadcasted_iota(jnp.int32, (BLK, BLK), 1)
    tri_after = (row > col).astype(BF16)
    scale = C_HEAD_DIM ** -0.5

    heads = [slice(h * C_HEAD_DIM, (h + 1) * C_HEAD_DIM) for h in range(SB_GROUP)]
    q_heads = [q_ref[:, hs] * scale for hs in heads]
    acc_ref[...] = jnp.zeros_like(acc_ref)

    def tiles(blocks, carries):
        keys = [(b, h) for b in range(len(blocks)) for h in range(SB_GROUP)]
        zs = {(b, h): lax.dot_general(q_heads[h], k_ref[blocks[b][0], :, heads[h]], NT_DIMS,
                                      preferred_element_type=F32) for b, h in keys}
        log_beta, rests, sums = {}, {}, {}
        for b, h in keys:
            z, mask = zs[b, h], blocks[b][1]
            sp = jnp.maximum(z, 0.0) + jnp.log(1.0 + jnp.exp2(jnp.abs(z) * -LOG2E))
            if mask is not None:
                sp = jnp.where(mask, sp, 0.0)
            log_beta[b, h] = z - sp
            sums[b, h] = jnp.sum(sp, axis=1, keepdims=True)
            rests[b, h] = jnp.dot(sp.astype(BF16), tri_after, preferred_element_type=F32)
        run = list(carries)
        for b, h in keys:
            j, mask = blocks[b]
            a = jnp.exp2((log_beta[b, h] - (rests[b, h] + run[h])) * LOG2E)
            if mask is not None:
                a = jnp.where(mask, a, 0.0)
            acc_ref[h] += jnp.dot(a.astype(BF16), v_ref[j, :, heads[h]], preferred_element_type=F32)
            run[h] = run[h] + sums[b, h]
        return tuple(run)

    def alive(carries):
        return jnp.min(functools.reduce(jnp.minimum, carries)) < SB_DEAD

    qf = row + i * BLK
    kf = col + i * BLK
    diag = (kf < qf) & (kf >= LEAD)
    carries = lax.cond(
        i >= 2, lambda c: tiles([(i, diag), (i - 1, None)], c),
        lambda c: lax.cond(i == 1, lambda c1: tiles([(1, diag), (0, col >= LEAD)], c1),
                           lambda c1: tiles([(0, diag)], c1), c),
        tuple(jnp.zeros((BLK, 1), F32) for _ in heads))
    n_rest = jnp.maximum(i - 2, 0)
    n_pairs = jnp.right_shift(n_rest, 1)
    _, carries = lax.while_loop(
        lambda s: (s[0] < n_pairs) & alive(s[1]),
        lambda s: (s[0] + 1, tiles([(i - 2 - 2 * s[0], None), (i - 3 - 2 * s[0], None)], s[1])),
        (jnp.int32(0), carries))
    carries = lax.cond(((n_rest & 1) == 1) & alive(carries), lambda c: tiles([(1, None)], c), lambda c: c,
                       carries)

    @pl.when((i >= 2) & alive(carries))
    def _():
        tiles([(0, col >= LEAD)], carries)

    o_ref[...] = jnp.concatenate([acc_ref[h] for h in range(SB_GROUP)], axis=1).astype(o_ref.dtype)


def _stick_breaking(p1, nbatch, nb):
    rows = p1.shape[0]
    cw = C_HEADS * C_HEAD_DIM
    gw = SB_GROUP * C_HEAD_DIM
    ngroups = C_HEADS // SB_GROUP
    p1b = p1.reshape(nbatch, nb, BLK, p1.shape[1])
    return pl.pallas_call(
        _stick_breaking_kernel,
        out_shape=jax.ShapeDtypeStruct((rows, cw), BF16),
        grid=(nbatch, ngroups, nb),
        in_specs=[pl.BlockSpec((BLK, gw), lambda b, g, i: (b * nb + i, g)),
                  pl.BlockSpec((None, nb, BLK, gw), lambda b, g, i: (b, 0, 0, ngroups + g)),
                  pl.BlockSpec((None, nb, BLK, gw), lambda b, g, i: (b, 0, 0, 2 * ngroups + g))],
        out_specs=pl.BlockSpec((BLK, gw), lambda b, g, i: (b * nb + i, g)),
        scratch_shapes=[pltpu.VMEM((SB_GROUP, BLK, C_HEAD_DIM), F32)],
        compiler_params=_params("parallel", "parallel", "arbitrary"),
        name="stick_breaking",
    )(p1, p1b, p1b)


def _pick_tile(rows, pref):
    t = pref
    while rows % t:
        t //= 2
    return t


def _pick_cols(n, cap=2816):
    return max(t for t in range(128, min(n, cap) + 1, 128) if n % t == 0)


def kernel(x, meta_tokens, rel_bias, norm_mix, norm_ffn, norm_final, even_w_in, even_gn_gain, even_w_out, odd_w_in, odd_w_out, ffn_w_up, ffn_w_gate, ffn_conv_w, ffn_conv_b, ffn_w_down):
    nbatch, seq, d = x.shape
    assert seq % BLK == 0
    nb = seq // BLK + 1
    rows = nbatch * nb * BLK
    n_keep = min(TOPK_MAX, seq // 4)
    assert n_keep >= N_META
    depth = norm_mix.shape[0]
    tm = _pick_tile(rows, 1024)

    meta = jnp.broadcast_to(meta_tokens[None].astype(x.dtype), (nbatch, N_META, d))
    h = jnp.concatenate([jnp.zeros((nbatch, LEAD, d), x.dtype), meta, x], axis=1).reshape(rows, d)

    aw = A_HEADS * A_HEAD_DIM
    qkw, vw = B_HEADS * B_QK_DIM, B_HEADS * B_V_DIM
    cw = C_HEADS * C_HEAD_DIM
    bias = _bias_tiles(rel_bias)

    for l in range(depth):
        j = l // 2
        if l % 2 == 0:
            w = even_w_in[j]
            o_iq, o_ik, o_iw, o_bq = 3 * aw, 4 * aw, 4 * aw + IDX_DIM, 4 * aw + IDX_DIM + IDX_HEADS
            w_main = jnp.concatenate([w[:, :4 * aw], w[:, o_bq:]], axis=1).astype(BF16)
            w_idx = jnp.concatenate([w[:, o_ik:o_bq], jnp.zeros((d, 128 - IDX_DIM - IDX_HEADS), w.dtype)],
                                    axis=1).astype(BF16)
            p0 = _norm_matmul(h, norm_mix[l], w_main, tm, _pick_cols(w_main.shape[1]))
            ikw = _norm_matmul(h, norm_mix[l], w_idx, tm, 128)
            a_out = _dsa_attention(p0, ikw, bias, nbatch, nb, n_keep)
            r_out = _retention(p0, even_gn_gain[j], nbatch, nb)
            w_out = even_w_out[j].astype(BF16)
            h = _proj_residual([(a_out, w_out, aw, 0, 0)]
                               + [(r_out, w_out, aw, t, t + 1) for t in range(vw // aw)], h, _pick_tile(rows, 512))
        else:
            p1 = _norm_matmul(h, norm_mix[l], odd_w_in[j].astype(BF16), tm, _pick_cols(odd_w_in.shape[2]))
            s_out = _stick_breaking(p1, nbatch, nb)
            h = _proj_residual([(s_out, odd_w_out[j].astype(BF16))], h, _pick_tile(rows, 512))
        act = _norm_gated_up(h, norm_ffn[l], ffn_w_up[l], ffn_w_gate[l], ffn_conv_w[l], ffn_conv_b[l], tm)
        w_down = ffn_w_down[l].astype(BF16)
        if l + 1 < depth:
            h = _proj_residual([(act, w_down)], h, _pick_tile(rows, 512))
        else:
            h = _proj_residual([(act, w_down)], h, BLK, final=(norm_final, nbatch, nb))
    return h
```

```python
import functools
import math

import jax
import jax.numpy as jnp
import numpy as np
from jax import lax
from jax.experimental import pallas as pl
from jax.experimental.pallas import tpu as pltpu

N_META = 16
BLK = 256
LEAD = BLK - N_META
A_HEADS, A_HEAD_DIM = 8, 64
IDX_HEADS, IDX_DIM = 8, 64
TOPK_MAX = 256
N_BUCKETS, MAX_DISTANCE = 32, 128
B_HEADS, B_QK_DIM, B_V_DIM = 4, 128, 256
ROPE_BASE = 10000.0
C_HEADS, C_HEAD_DIM = 16, 64
CONV_WIDTH = 3
EPS = 1e-6
BIG = 1e30
LOG2E = 1.4426950408889634
BISECT_ITERS = 40
PEEL_SLACK = 7.0
VMEM_LIMIT_BYTES = 56 * 1024 * 1024

F32 = jnp.float32
BF16 = jnp.bfloat16
NT_DIMS = (((1,), (1,)), ((), ()))
TN_DIMS = (((0,), (0,)), ((), ()))


def _params(*sem):
    return pltpu.CompilerParams(dimension_semantics=sem, vmem_limit_bytes=VMEM_LIMIT_BYTES)


def _norm_matmul_kernel(x_ref, g_ref, w_ref, *rest, side):
    if side:
        ws_ref, o_ref, os_ref, xn_ref = rest
    else:
        o_ref, xn_ref = rest

    @pl.when(pl.program_id(1) == 0)
    def _():
        x = x_ref[...]
        ms = jnp.mean(x * x, axis=-1, keepdims=True)
        xn_ref[...] = (x * lax.rsqrt(ms + EPS) * g_ref[...]).astype(BF16)
        if side:
            os_ref[...] = jnp.dot(xn_ref[...], ws_ref[...], preferred_element_type=F32).astype(os_ref.dtype)

    o_ref[...] = jnp.dot(xn_ref[...], w_ref[...], preferred_element_type=F32).astype(o_ref.dtype)


def _norm_matmul(x, gain, w, tm, tn, w_side=None):
    rows, d = x.shape
    n = w.shape[1]
    in_specs = [pl.BlockSpec((tm, d), lambda i, j: (i, 0)),
                pl.BlockSpec((1, d), lambda i, j: (0, 0)),
                pl.BlockSpec((d, tn), lambda i, j: (0, j))]
    out_shape = [jax.ShapeDtypeStruct((rows, n), BF16)]
    out_specs = [pl.BlockSpec((tm, tn), lambda i, j: (i, j))]
    args = [x, gain.reshape(1, d), w]
    if w_side is not None:
        ns = w_side.shape[1]
        in_specs.append(pl.BlockSpec((d, ns), lambda i, j: (0, 0)))
        out_shape.append(jax.ShapeDtypeStruct((rows, ns), BF16))
        out_specs.append(pl.BlockSpec((tm, ns), lambda i, j: (i, 0)))
        args.append(w_side)
    out = pl.pallas_call(
        functools.partial(_norm_matmul_kernel, side=w_side is not None),
        out_shape=out_shape,
        grid=(rows // tm, n // tn),
        in_specs=in_specs,
        out_specs=out_specs,
        scratch_shapes=[pltpu.VMEM((tm, d), BF16)],
        compiler_params=_params("parallel", "arbitrary"),
        name="norm_matmul",
    )(*args)
    return out if w_side is not None else out[0]


def _bucket_tiles():
    q = np.arange(BLK)[:, None]
    k = np.arange(BLK)[None, :]
    tiles = []
    for t in range(2):
        n = np.maximum(q - k + t * BLK, 0)
        max_exact = N_BUCKETS // 2
        large = max_exact + (np.log(np.maximum(n, 1).astype(np.float32) / max_exact)
                             / math.log(MAX_DISTANCE / max_exact)
                             * (N_BUCKETS - max_exact)).astype(np.int32)
        large = np.minimum(large, N_BUCKETS - 1)
        tiles.append(np.where(n < max_exact, n, large).astype(np.int32))
    return np.stack(tiles)


def _bias_tiles_kernel(rb_ref, idx_ref, o_ref):
    h = pl.program_id(0)
    far = rb_ref[N_BUCKETS - 1, h]
    for t in range(2):
        idx = idx_ref[t]
        acc = jnp.zeros((BLK, BLK), F32)
        for b in range(N_BUCKETS):
            acc = jnp.where(idx == b, rb_ref[b, h] - far, acc)
        o_ref[t] = acc


def _bias_tiles(rel_bias):
    assert BLK + 1 >= MAX_DISTANCE
    return pl.pallas_call(
        _bias_tiles_kernel,
        out_shape=jax.ShapeDtypeStruct((A_HEADS, 2, BLK, BLK), F32),
        grid=(A_HEADS,),
        in_specs=[pl.BlockSpec(memory_space=pltpu.SMEM),
                  pl.BlockSpec((2, BLK, BLK), lambda h: (0, 0, 0))],
        out_specs=pl.BlockSpec((None, 2, BLK, BLK), lambda h: (h, 0, 0, 0)),
        compiler_params=_params("parallel"),
        name="t5_bias_tiles",
    )(rel_bias.astype(F32), jnp.asarray(_bucket_tiles()))


def _dsa_kernel(aq_ref, iq_ref, ikwq_ref, ak_ref, av_ref, ikw_ref, bias_ref, o_ref,
                s_ref, acc_ref, av1_ref, *, n_keep):
    i = pl.program_id(1)
    row = lax.broadcasted_iota(jnp.int32, (BLK, BLK), 0)
    col = lax.broadcasted_iota(jnp.int32, (BLK, BLK), 1)
    qf = col + i * BLK
    heads = [slice(h * A_HEAD_DIM, (h + 1) * A_HEAD_DIM) for h in range(A_HEADS)]
    heads1 = [slice(h * 2 * A_HEAD_DIM, (h + 1) * 2 * A_HEAD_DIM) for h in range(A_HEADS)]

    @pl.when(i == 0)
    def _():
        def fill(j, _):
            ones = jnp.ones((BLK, A_HEAD_DIM), BF16)
            av1_ref[j] = jnp.concatenate([x for hs in heads for x in (av_ref[j, :, hs], ones)], axis=1)
            return 0
        lax.fori_loop(0, av_ref.shape[0], fill, 0)

    s_ref[0] = jnp.where((row >= LEAD) & (row <= qf), BIG, -BIG)
    eye = (lax.broadcasted_iota(jnp.int32, (128, 128), 0)
           == lax.broadcasted_iota(jnp.int32, (128, 128), 1)).astype(BF16)
    ikwq_t = lax.dot_general(eye, ikwq_ref[...], NT_DIMS, preferred_element_type=F32)
    w_scale = (IDX_DIM ** -0.5) * (IDX_HEADS ** -0.5)
    w_rows = [ikwq_t[IDX_DIM + h:IDX_DIM + h + 1, :] * w_scale for h in range(IDX_HEADS)]
    iq_heads = [iq_ref[:, h * IDX_DIM:(h + 1) * IDX_DIM] for h in range(IDX_HEADS)]

    def score_blocks(js, mabs):
        accs = [jnp.zeros((BLK, BLK), F32) for _ in js]
        for h in range(IDX_HEADS):
            for t, j in enumerate(js):
                st = lax.dot_general(ikw_ref[j, :, :IDX_DIM], iq_heads[h], NT_DIMS, preferred_element_type=F32)
                accs[t] = accs[t] + jnp.maximum(st, 0.0) * w_rows[h]
        for j, acc in zip(js, accs):
            causal = (row + j * BLK) <= qf
            s_ref[j] = jnp.where(causal, acc, -BIG)
            mabs = jnp.maximum(mabs, jnp.max(jnp.where(causal, jnp.abs(acc), 0.0), axis=0, keepdims=True))
        return mabs

    mabs = lax.fori_loop(0, jnp.right_shift(i, 1), lambda t, m: score_blocks((2 * t + 1, 2 * t + 2), m),
                         jnp.zeros((1, BLK), F32))
    mabs = lax.cond((i & 1) == 1, lambda m: score_blocks((i,), m), lambda m: m, mabs)

    k_eff = float(n_keep - N_META)
    bound = mabs * 1.000001 + 1e-30

    def count_ge(thr):
        def body(j, c):
            hit = (s_ref[j] >= thr).astype(F32).reshape(4, BLK // 32, 8, BLK)
            return c + jnp.sum(hit, axis=1)
        c = lax.fori_loop(1, i + 1, body, jnp.zeros((4, 8, BLK), F32))
        return jnp.sum(jnp.sum(c, axis=0), axis=0, keepdims=True)

    def unsettled(state):
        it, _, _, cnt_lo = state
        return (it < BISECT_ITERS) & (jnp.max(cnt_lo) > k_eff + PEEL_SLACK)

    def halve(lo, hi, cnt_lo):
        mid = lo + (hi - lo) * 0.5
        cnt = count_ge(mid)
        ge = cnt >= k_eff
        return jnp.where(ge, mid, lo), jnp.where(ge, hi, mid), jnp.where(ge, cnt, cnt_lo)

    def bisect(state):
        it, lo, hi, cnt_lo = state
        return (it + 2,) + halve(*halve(lo, hi, cnt_lo))

    _, lo, _, _ = lax.while_loop(unsettled, bisect, (jnp.int32(0), -bound, bound, count_ge(-bound)))

    def vmin_body(j, v):
        s = s_ref[j]
        return jnp.minimum(v, jnp.min(jnp.where(s >= lo, s, BIG), axis=0, keepdims=True))

    v = lax.fori_loop(1, i + 1, vmin_body, jnp.full((1, BLK), BIG, F32))
    v = jnp.where(v >= BIG, 0.0, v)

    def above(x):
        def body(j, carry):
            cnt, nxt = carry
            s = s_ref[j]
            gt = s > x
            cnt = cnt + jnp.sum(gt.astype(F32).reshape(4, BLK // 32, 8, BLK), axis=1)
            nxt = jnp.minimum(nxt, jnp.min(jnp.where(gt, s, BIG).reshape(4, BLK // 32, 8, BLK), axis=1))
            return cnt, nxt
        cnt, nxt = lax.fori_loop(1, i + 1, body, (jnp.zeros((4, 8, BLK), F32), jnp.full((4, 8, BLK), BIG, F32)))
        return (jnp.sum(jnp.sum(cnt, axis=0), axis=0, keepdims=True),
                jnp.min(jnp.min(nxt, axis=0), axis=0, keepdims=True))

    def peel(state):
        x, cnt_gt, nxt = state
        x = jnp.where(cnt_gt >= k_eff, nxt, x)
        return (x,) + above(x)

    v, cnt_gt, _ = lax.while_loop(lambda s: jnp.max(s[1]) >= k_eff, peel, (v,) + above(v))
    ties_wanted = k_eff - cnt_gt

    acc_ref[...] = jnp.zeros_like(acc_ref)
    tri_incl = (col <= row).astype(BF16)
    aq_heads = [aq_ref[:, hs] * (A_HEAD_DIM ** -0.5) for hs in heads]

    def attend(js, carry, near):
        ties_seen, ms = carry
        negs = []
        for j in js:
            s = s_ref[j]
            eq = s == v
            rank = ties_seen + jnp.dot(tri_incl, eq.astype(BF16), preferred_element_type=F32)
            sel = (s > v) | (eq & (rank <= ties_wanted))
            negs.append(jnp.where(sel, 0.0, -BIG).T)
            ties_seen = rank[BLK - 1:BLK, :]

        def logits(t, h):
            z = lax.dot_general(aq_heads[h], ak_ref[js[t], :, heads[h]], NT_DIMS, preferred_element_type=F32)
            return z + (bias_ref[h, i - js[t]] + negs[t] if near else negs[t])

        zs = {(t, h): logits(t, h) for h in range(A_HEADS) for t in range(len(js))}
        ms_new = []
        for h in range(A_HEADS):
            zh = [zs.pop((t, h)) for t in range(len(js))]
            m_new = jnp.maximum(ms[h], jnp.max(functools.reduce(jnp.maximum, zh), axis=1, keepdims=True))
            alpha = jnp.exp(ms[h] - m_new)
            ms_new.append(m_new)
            acc = alpha * acc_ref[h]
            for t, z in enumerate(zh):
                acc = acc + jnp.dot(jnp.exp(z - m_new).astype(BF16), av1_ref[js[t], :, heads1[h]],
                                    preferred_element_type=F32)
            acc_ref[h] = acc
        return ties_seen, tuple(ms_new)

    init = (jnp.zeros((1, BLK), F32), tuple(jnp.full((BLK, 1), -BIG, F32) for _ in heads))
    n_far = jnp.maximum(i - 1, 0)
    carry = lax.fori_loop(0, jnp.right_shift(n_far, 2),
                          lambda t, c: attend((4 * t, 4 * t + 1, 4 * t + 2, 4 * t + 3), c, False), init)
    done = n_far & ~3
    carry = lax.cond((n_far & 2) == 2, lambda c: attend((done, done + 1), c, False), lambda c: c, carry)
    carry = lax.cond((n_far & 1) == 1, lambda c: attend((n_far - 1,), c, False), lambda c: c, carry)

    @pl.when(i >= 1)
    def _():
        attend((i - 1, i), carry, True)

    @pl.when(i == 0)
    def _():
        attend((0,), carry, True)

    valid_q = (lax.broadcasted_iota(jnp.int32, (BLK, 1), 0) + i * BLK) >= LEAD
    outs = []
    for h in range(A_HEADS):
        acc = acc_ref[h]
        outs.append((acc / pltpu.roll(acc, A_HEAD_DIM, 1))[:, :A_HEAD_DIM])
    o_ref[...] = jnp.where(valid_q, jnp.concatenate(outs, axis=1), 0.0).astype(o_ref.dtype)


def _dsa_attention(p0, ikw, bias, nbatch, nb, n_keep):
    rows = p0.shape[0]
    aw = A_HEADS * A_HEAD_DIM
    p0b = p0.reshape(nbatch, nb, BLK, p0.shape[1])
    ikwb = ikw.reshape(nbatch, nb, BLK, ikw.shape[1])
    return pl.pallas_call(
        functools.partial(_dsa_kernel, n_keep=n_keep),
        out_shape=jax.ShapeDtypeStruct((rows, aw), BF16),
        grid=(nbatch, nb),
        in_specs=[pl.BlockSpec((BLK, aw), lambda b, i: (b * nb + i, 0)),
                  pl.BlockSpec((BLK, aw), lambda b, i: (b * nb + i, 3)),
                  pl.BlockSpec((BLK, 128), lambda b, i: (b * nb + i, 0)),
                  pl.BlockSpec((None, nb, BLK, aw), lambda b, i: (b, 0, 0, 1)),
                  pl.BlockSpec((None, nb, BLK, aw), lambda b, i: (b, 0, 0, 2)),
                  pl.BlockSpec((None, nb, BLK, 128), lambda b, i: (b, 0, 0, 0)),
                  pl.BlockSpec((A_HEADS, 2, BLK, BLK), lambda b, i: (0, 0, 0, 0))],
        out_specs=pl.BlockSpec((BLK, aw), lambda b, i: (b * nb + i, 0)),
        scratch_shapes=[pltpu.VMEM((nb, BLK, BLK), F32),
                        pltpu.VMEM((A_HEADS, BLK, 2 * A_HEAD_DIM), F32),
                        pltpu.VMEM((nb, BLK, 2 * aw), BF16)],
        compiler_params=_params("parallel", "arbitrary"),
        name="dsa_attention",
    )(p0, p0, ikw, p0b, p0b, ikwb, bias)


def _retention_kernel(bq_ref, bk_ref, bv_ref, bg_ref, cos_ref, sin_ref, dmat_ref, xi_ref, zeta_ref,
                      gain_ref, o_ref, r_ref, *, g_chunk):
    @pl.when(pl.program_id(1) == 0)
    def _():
        r_ref[...] = jnp.zeros_like(r_ref)

    cosf = cos_ref[...]
    sinf = sin_ref[...]

    def rot(x):
        return x * cosf + pltpu.roll(x, B_QK_DIM // 2, 1) * sinf

    for h in range(B_HEADS):
        ks = slice(h * B_QK_DIM, (h + 1) * B_QK_DIM)
        vs = slice(h * B_V_DIM, (h + 1) * B_V_DIM)
        q = rot(bq_ref[:, ks].astype(F32))
        k = rot(bk_ref[:, ks].astype(F32)) * (B_QK_DIM ** -0.5)
        qb = q.astype(BF16)
        kb = k.astype(BF16)
        v = bv_ref[:, vs]
        inner = lax.dot_general(qb, kb, NT_DIMS, preferred_element_type=F32) * dmat_ref[h]
        r_old = r_ref[h]
        o = (jnp.dot(inner.astype(BF16), v, preferred_element_type=F32)
             + jnp.dot(qb, r_old.astype(BF16), preferred_element_type=F32) * xi_ref[h])
        kz = (k * zeta_ref[h]).astype(BF16)
        r_ref[h] = r_old * g_chunk[h] + lax.dot_general(kz, v, TN_DIMS, preferred_element_type=F32)
        mu = jnp.mean(o, axis=-1, keepdims=True)
        oc = o - mu
        var = jnp.mean(oc * oc, axis=-1, keepdims=True)
        rn = oc * lax.rsqrt(var + EPS) * gain_ref[:, vs]
        gate = bg_ref[:, vs].astype(F32)
        o_ref[:, vs] = (rn * (gate / (1.0 + jnp.exp(-gate)))).astype(o_ref.dtype)


def _retention(p0, gn_gain, nbatch, nb):
    rows = p0.shape[0]
    qkw, vw = B_HEADS * B_QK_DIM, B_HEADS * B_V_DIM
    frame = nb * BLK
    half = B_QK_DIM // 2
    pos = (jnp.arange(frame) - LEAD).astype(F32)
    inv = 1.0 / (ROPE_BASE ** (jnp.arange(half, dtype=F32) / half))
    ang = pos[:, None] * inv[None, :]
    cosf = jnp.concatenate([jnp.cos(ang), jnp.cos(ang)], axis=-1)
    sinf = jnp.concatenate([-jnp.sin(ang), jnp.sin(ang)], axis=-1)
    lg = jnp.log(1.0 - 2.0 ** (-5.0 - jnp.arange(B_HEADS, dtype=F32)))
    n = jnp.arange(BLK, dtype=F32)
    diff = n[:, None] - n[None, :]
    dmat = jnp.where(diff[None] >= 0, jnp.exp(jnp.maximum(diff, 0.0)[None] * lg[:, None, None]), 0.0)
    xi = jnp.broadcast_to(jnp.exp((n[None, :] + 1.0) * lg[:, None])[..., None], (B_HEADS, BLK, B_V_DIM))
    zeta = jnp.broadcast_to(jnp.exp((BLK - 1.0 - n[None, :]) * lg[:, None])[..., None],
                            (B_HEADS, BLK, B_QK_DIM))
    g_chunk = tuple(float(math.exp(BLK * math.log(1.0 - 2.0 ** (-5.0 - h)))) for h in range(B_HEADS))
    const = lambda shape: pl.BlockSpec(shape, lambda b, i: (0,) * len(shape))
    return pl.pallas_call(
        functools.partial(_retention_kernel, g_chunk=g_chunk),
        out_shape=jax.ShapeDtypeStruct((rows, vw), BF16),
        grid=(nbatch, nb),
        in_specs=[pl.BlockSpec((BLK, qkw), lambda b, i: (b * nb + i, 4)),
                  pl.BlockSpec((BLK, qkw), lambda b, i: (b * nb + i, 5)),
                  pl.BlockSpec((BLK, vw), lambda b, i: (b * nb + i, 3)),
                  pl.BlockSpec((BLK, vw), lambda b, i: (b * nb + i, 4)),
                  pl.BlockSpec((BLK, B_QK_DIM), lambda b, i: (i, 0)),
                  pl.BlockSpec((BLK, B_QK_DIM), lambda b, i: (i, 0)),
                  const((B_HEADS, BLK, BLK)),
                  const((B_HEADS, BLK, B_V_DIM)),
                  const((B_HEADS, BLK, B_QK_DIM)),
                  const((1, vw))],
        out_specs=pl.BlockSpec((BLK, vw), lambda b, i: (b * nb + i, 0)),
        scratch_shapes=[pltpu.VMEM((B_HEADS, B_QK_DIM, B_V_DIM), F32)],
        compiler_params=_params("parallel", "arbitrary"),
        name="retention",
    )(p0, p0, p0, p0, cosf, sinf, dmat, xi, zeta, gn_gain.reshape(1, vw).astype(F32))


def _proj_residual_kernel(*refs, n_pairs, final):
    x = refs[2 * n_pairs][...]
    for t in range(n_pairs):
        x = x + jnp.dot(refs[2 * t][...], refs[2 * t + 1][...], preferred_element_type=F32)
    if final:
        ng_ref, o_ref = refs[2 * n_pairs + 1:]
        ms = jnp.mean(x * x, axis=-1, keepdims=True)
        o_ref[...] = x * lax.rsqrt(ms + EPS) * ng_ref[...]
    else:
        refs[2 * n_pairs + 1][...] = x


def _proj_residual(pairs, h, tm, final=None):
    rows, d = h.shape
    in_specs, args = [], []
    for a, w, *blk in pairs:
        kb, ac, wr = blk if blk else (a.shape[1], 0, 0)
        in_specs += [pl.BlockSpec((tm, kb), lambda i, ac=ac: (i, ac)),
                     pl.BlockSpec((kb, d), lambda i, wr=wr: (wr, 0))]
        args += [a, w]
    in_specs.append(pl.BlockSpec((tm, d), lambda i: (i, 0)))
    args.append(h)
    if final is None:
        out_shape = jax.ShapeDtypeStruct((rows, d), F32)
        out_spec = pl.BlockSpec((tm, d), lambda i: (i, 0))
        sem = "parallel"
    else:
        gain, nbatch, nb = final
        assert tm == BLK
        in_specs.append(pl.BlockSpec((1, d), lambda i: (0, 0)))
        args.append(gain.reshape(1, d).astype(F32))
        out_shape = jax.ShapeDtypeStruct((nbatch, (nb - 1) * BLK, d), F32)
        out_spec = pl.BlockSpec((None, BLK, d), lambda i: (i // nb, jnp.maximum(i % nb - 1, 0), 0))
        sem = "arbitrary"
    return pl.pallas_call(
        functools.partial(_proj_residual_kernel, n_pairs=len(pairs), final=final is not None),
        out_shape=out_shape,
        grid=(rows // tm,),
        in_specs=in_specs,
        out_specs=out_spec,
        compiler_params=_params(sem),
        name="proj_residual",
    )(*args)


GATE_COLS = 256


def _norm_gate_kernel(x_ref, gn_ref, wu_ref, wg_ref, cw_ref, cb_ref, o_ref, halo_ref):
    @pl.when(pl.program_id(0) == 0)
    def _():
        halo_ref[...] = jnp.zeros_like(halo_ref)

    x = x_ref[...]
    ms = jnp.mean(x * x, axis=-1, keepdims=True)
    xn = (x * lax.rsqrt(ms + EPS) * gn_ref[...]).astype(BF16)
    tm = x.shape[0]
    row8 = lax.broadcasted_iota(jnp.int32, (8, GATE_COLS), 0)
    for c in range(o_ref.shape[1] // GATE_COLS):
        cs = slice(c * GATE_COLS, (c + 1) * GATE_COLS)
        u = jnp.dot(xn, wu_ref[:, cs], preferred_element_type=F32)
        g = jnp.dot(xn, wg_ref[:, cs], preferred_element_type=F32)
        prev = halo_ref[:, cs]
        halo_ref[:, cs] = g[tm - 8:, :]
        g1 = pltpu.roll(g, 1, 0)
        g2 = pltpu.roll(g, 2, 0)
        g1 = jnp.concatenate([jnp.where(row8 >= 1, g1[:8], prev[7:8]), g1[8:]], axis=0)
        g2 = jnp.concatenate([jnp.where(row8 >= 2, g2[:8], jnp.where(row8 == 1, prev[7:8], prev[6:7])),
                              g2[8:]], axis=0)
        gc = g2 * cw_ref[0:1, cs] + g1 * cw_ref[1:2, cs] + g * cw_ref[2:3, cs] + cb_ref[:, cs]
        o_ref[:, cs] = ((gc / (1.0 + jnp.exp(-gc))) * u).astype(o_ref.dtype)


def _norm_gated_up(x, gain, w_up, w_gate, conv_w, conv_b, tm):
    rows, d = x.shape
    dff = w_up.shape[1]
    assert dff % GATE_COLS == 0
    const = lambda shape: pl.BlockSpec(shape, lambda i: (0, 0))
    return pl.pallas_call(
        _norm_gate_kernel,
        out_shape=jax.ShapeDtypeStruct((rows, dff), BF16),
        grid=(rows // tm,),
        in_specs=[pl.BlockSpec((tm, d), lambda i: (i, 0)),
                  const((1, d)), const((d, dff)), const((d, dff)), const((CONV_WIDTH, dff)), const((1, dff))],
        out_specs=pl.BlockSpec((tm, dff), lambda i: (i, 0)),
        scratch_shapes=[pltpu.VMEM((8, dff), F32)],
        compiler_params=_params("arbitrary"),
        name="norm_gated_up",
    )(x, gain.reshape(1, d), w_up.astype(BF16), w_gate.astype(BF16), conv_w.astype(F32),
      conv_b.reshape(1, dff).astype(F32))


SB_GROUP = 8
SB_DEAD = 110.0


def _stick_breaking_kernel(q_ref, k_ref, v_ref, o_ref, acc_ref):
    i = pl.program_id(2)
    row = lax.broadcasted_iota(jnp.int32, (BLK, BLK), 0)
    col = lax.broadcasted_iota(jnp.int32, (BLK, BLK), 1)
    tri_after = (row > col).astype(BF16)
    scale = C_HEAD_DIM ** -0.5

    heads = [slice(h * C_HEAD_DIM, (h + 1) * C_HEAD_DIM) for h in range(SB_GROUP)]
    q_heads = [q_ref[:, hs] * scale for hs in heads]
    acc_ref[...] = jnp.zeros_like(acc_ref)

    def tiles(blocks, carries):
        keys = [(b, h) for b in range(len(blocks)) for h in range(SB_GROUP)]
        zs = {(b, h): lax.dot_general(q_heads[h], k_ref[blocks[b][0], :, heads[h]], NT_DIMS,
                                      preferred_element_type=F32) for b, h in keys}
        log_beta, rests, sums = {}, {}, {}
        for b, h in keys:
            z, mask = zs[b, h], blocks[b][1]
            sp = jnp.maximum(z, 0.0) + jnp.log(1.0 + jnp.exp2(jnp.abs(z) * -LOG2E))
            if mask is not None:
                sp = jnp.where(mask, sp, 0.0)
            log_beta[b, h] = z - sp
            sums[b, h] = jnp.sum(sp, axis=1, keepdims=True)
            rests[b, h] = jnp.dot(sp.astype(BF16), tri_after, preferred_element_type=F32)
        run = list(carries)
        for b, h in keys:
            j, mask = blocks[b]
            a = jnp.exp2((log_beta[b, h] - (rests[b, h] + run[h])) * LOG2E)
            if mask is not None:
                a = jnp.where(mask, a, 0.0)
            acc_ref[h] += jnp.dot(a.astype(BF16), v_ref[j, :, heads[h]], preferred_element_type=F32)
            run[h] = run[h] + sums[b, h]
        return tuple(run)

    def alive(carries):
        return jnp.min(functools.reduce(jnp.minimum, carries)) < SB_DEAD

    qf = row + i * BLK
    kf = col + i * BLK
    diag = (kf < qf) & (kf >= LEAD)
    carries = lax.cond(
        i >= 2, lambda c: tiles([(i, diag), (i - 1, None)], c),
        lambda c: lax.cond(i == 1, lambda c1: tiles([(1, diag), (0, col >= LEAD)], c1),
                           lambda c1: tiles([(0, diag)], c1), c),
        tuple(jnp.zeros((BLK, 1), F32) for _ in heads))
    n_rest = jnp.maximum(i - 2, 0)
    n_pairs = jnp.right_shift(n_rest, 1)
    _, carries = lax.while_loop(
        lambda s: (s[0] < n_pairs) & alive(s[1]),
        lambda s: (s[0] + 1, tiles([(i - 2 - 2 * s[0], None), (i - 3 - 2 * s[0], None)], s[1])),
        (jnp.int32(0), carries))
    carries = lax.cond(((n_rest & 1) == 1) & alive(carries), lambda c: tiles([(1, None)], c), lambda c: c,
                       carries)

    @pl.when((i >= 2) & alive(carries))
    def _():
        tiles([(0, col >= LEAD)], carries)

    o_ref[...] = jnp.concatenate([acc_ref[h] for h in range(SB_GROUP)], axis=1).astype(o_ref.dtype)


def _stick_breaking(p1, nbatch, nb):
    rows = p1.shape[0]
    cw = C_HEADS * C_HEAD_DIM
    gw = SB_GROUP * C_HEAD_DIM
    ngroups = C_HEADS // SB_GROUP
    p1b = p1.reshape(nbatch, nb, BLK, p1.shape[1])
    return pl.pallas_call(
        _stick_breaking_kernel,
        out_shape=jax.ShapeDtypeStruct((rows, cw), BF16),
        grid=(nbatch, ngroups, nb),
        in_specs=[pl.BlockSpec((BLK, gw), lambda b, g, i: (b * nb + i, g)),
                  pl.BlockSpec((None, nb, BLK, gw), lambda b, g, i: (b, 0, 0, ngroups + g)),
                  pl.BlockSpec((None, nb, BLK, gw), lambda b, g, i: (b, 0, 0, 2 * ngroups + g))],
        out_specs=pl.BlockSpec((BLK, gw), lambda b, g, i: (b * nb + i, g)),
        scratch_shapes=[pltpu.VMEM((SB_GROUP, BLK, C_HEAD_DIM), F32)],
        compiler_params=_params("parallel", "parallel", "arbitrary"),
        name="stick_breaking",
    )(p1, p1b, p1b)


def _pick_tile(rows, pref):
    t = pref
    while rows % t:
        t //= 2
    return t


def _pick_cols(n, cap=2816):
    return max(t for t in range(128, min(n, cap) + 1, 128) if n % t == 0)


def kernel(x, meta_tokens, rel_bias, norm_mix, norm_ffn, norm_final, even_w_in, even_gn_gain, even_w_out, odd_w_in, odd_w_out, ffn_w_up, ffn_w_gate, ffn_conv_w, ffn_conv_b, ffn_w_down):
    nbatch, seq, d = x.shape
    assert seq % BLK == 0
    nb = seq // BLK + 1
    rows = nbatch * nb * BLK
    n_keep = min(TOPK_MAX, seq // 4)
    assert n_keep >= N_META
    depth = norm_mix.shape[0]
    tm = _pick_tile(rows, 1024)

    meta = jnp.broadcast_to(meta_tokens[None].astype(x.dtype), (nbatch, N_META, d))
    h = jnp.concatenate([jnp.zeros((nbatch, LEAD, d), x.dtype), meta, x], axis=1).reshape(rows, d)

    aw = A_HEADS * A_HEAD_DIM
    qkw, vw = B_HEADS * B_QK_DIM, B_HEADS * B_V_DIM
    cw = C_HEADS * C_HEAD_DIM
    bias = _bias_tiles(rel_bias)

    for l in range(depth):
        j = l // 2
        if l % 2 == 0:
            w = even_w_in[j]
            o_iq, o_ik, o_iw, o_bq = 3 * aw, 4 * aw, 4 * aw + IDX_DIM, 4 * aw + IDX_DIM + IDX_HEADS
            w_main = jnp.concatenate([w[:, :4 * aw], w[:, o_bq:]], axis=1).astype(BF16)
            w_idx = jnp.concatenate([w[:, o_ik:o_bq], jnp.zeros((d, 128 - IDX_DIM - IDX_HEADS), w.dtype)],
                                    axis=1).astype(BF16)
            p0, ikw = _norm_matmul(h, norm_mix[l], w_main, tm, _pick_cols(w_main.shape[1]), w_side=w_idx)
            a_out = _dsa_attention(p0, ikw, bias, nbatch, nb, n_keep)
            r_out = _retention(p0, even_gn_gain[j], nbatch, nb)
            w_out = even_w_out[j].astype(BF16)
            h = _proj_residual([(a_out, w_out, aw, 0, 0)]
                               + [(r_out, w_out, aw, t, t + 1) for t in range(vw // aw)], h, _pick_tile(rows, 512))
        else:
            p1 = _norm_matmul(h, norm_mix[l], odd_w_in[j].astype(BF16), tm, _pick_cols(odd_w_in.shape[2]))
            s_out = _stick_breaking(p1, nbatch, nb)
            h = _proj_residual([(s_out, odd_w_out[j].astype(BF16))], h, _pick_tile(rows, 512))
        act = _norm_gated_up(h, norm_ffn[l], ffn_w_up[l], ffn_w_gate[l], ffn_conv_w[l], ffn_conv_b[l], tm)
        w_down = ffn_w_down[l].astype(BF16)
        if l + 1 < depth:
            h = _proj_residual([(act, w_down)], h, _pick_tile(rows, 512))
        else:
            h = _proj_residual([(act, w_down)], h, BLK, final=(norm_final, nbatch, nb))
    return h
```

```python
import functools
import math

import jax
import jax.numpy as jnp
import numpy as np
from jax import lax
from jax.experimental import pallas as pl
from jax.experimental.pallas import tpu as pltpu

N_META = 16
BLK = 256
LEAD = BLK - N_META
A_HEADS, A_HEAD_DIM = 8, 64
IDX_HEADS, IDX_DIM = 8, 64
TOPK_MAX = 256
N_BUCKETS, MAX_DISTANCE = 32, 128
B_HEADS, B_QK_DIM, B_V_DIM = 4, 128, 256
ROPE_BASE = 10000.0
C_HEADS, C_HEAD_DIM = 16, 64
CONV_WIDTH = 3
EPS = 1e-6
BIG = 1e30
LOG2E = 1.4426950408889634
BISECT_ITERS = 40
PEEL_SLACK = 7.0
VMEM_LIMIT_BYTES = 56 * 1024 * 1024

F32 = jnp.float32
BF16 = jnp.bfloat16
NT_DIMS = (((1,), (1,)), ((), ()))
TN_DIMS = (((0,), (0,)), ((), ()))


def _params(*sem):
    return pltpu.CompilerParams(dimension_semantics=sem, vmem_limit_bytes=VMEM_LIMIT_BYTES)


def _norm_matmul_kernel(x_ref, g_ref, w_ref, *rest, side):
    if side:
        ws_ref, o_ref, os_ref, xn_ref = rest
    else:
        o_ref, xn_ref = rest

    @pl.when(pl.program_id(1) == 0)
    def _():
        x = x_ref[...]
        ms = jnp.mean(x * x, axis=-1, keepdims=True)
        xn_ref[...] = (x * lax.rsqrt(ms + EPS) * g_ref[...]).astype(BF16)
        if side:
            os_ref[...] = jnp.dot(xn_ref[...], ws_ref[...], preferred_element_type=F32).astype(os_ref.dtype)

    o_ref[...] = jnp.dot(xn_ref[...], w_ref[...], preferred_element_type=F32).astype(o_ref.dtype)


def _norm_matmul(x, gain, w, tm, tn, w_side=None):
    rows, d = x.shape
    n = w.shape[1]
    in_specs = [pl.BlockSpec((tm, d), lambda i, j: (i, 0)),
                pl.BlockSpec((1, d), lambda i, j: (0, 0)),
                pl.BlockSpec((d, tn), lambda i, j: (0, j))]
    out_shape = [jax.ShapeDtypeStruct((rows, n), BF16)]
    out_specs = [pl.BlockSpec((tm, tn), lambda i, j: (i, j))]
    args = [x, gain.reshape(1, d), w]
    if w_side is not None:
        ns = w_side.shape[1]
        in_specs.append(pl.BlockSpec((d, ns), lambda i, j: (0, 0)))
        out_shape.append(jax.ShapeDtypeStruct((rows, ns), BF16))
        out_specs.append(pl.BlockSpec((tm, ns), lambda i, j: (i, 0)))
        args.append(w_side)
    out = pl.pallas_call(
        functools.partial(_norm_matmul_kernel, side=w_side is not None),
        out_shape=out_shape,
        grid=(rows // tm, n // tn),
        in_specs=in_specs,
        out_specs=out_specs,
        scratch_shapes=[pltpu.VMEM((tm, d), BF16)],
        compiler_params=_params("parallel", "arbitrary"),
        name="norm_matmul",
    )(*args)
    return out if w_side is not None else out[0]


def _bucket_tiles():
    q = np.arange(BLK)[:, None]
    k = np.arange(BLK)[None, :]
    tiles = []
    for t in range(2):
        n = np.maximum(q - k + t * BLK, 0)
        max_exact = N_BUCKETS // 2
        large = max_exact + (np.log(np.maximum(n, 1).astype(np.float32) / max_exact)
                             / math.log(MAX_DISTANCE / max_exact)
                             * (N_BUCKETS - max_exact)).astype(np.int32)
        large = np.minimum(large, N_BUCKETS - 1)
        tiles.append(np.where(n < max_exact, n, large).astype(np.int32))
    return np.stack(tiles)


def _bias_tiles_kernel(rb_ref, idx_ref, o_ref):
    h = pl.program_id(0)
    far = rb_ref[N_BUCKETS - 1, h]
    for t in range(2):
        idx = idx_ref[t]
        acc = jnp.zeros((BLK, BLK), F32)
        for b in range(N_BUCKETS):
            acc = jnp.where(idx == b, rb_ref[b, h] - far, acc)
        o_ref[t] = acc


def _bias_tiles(rel_bias):
    assert BLK + 1 >= MAX_DISTANCE
    return pl.pallas_call(
        _bias_tiles_kernel,
        out_shape=jax.ShapeDtypeStruct((A_HEADS, 2, BLK, BLK), F32),
        grid=(A_HEADS,),
        in_specs=[pl.BlockSpec(memory_space=pltpu.SMEM),
                  pl.BlockSpec((2, BLK, BLK), lambda h: (0, 0, 0))],
        out_specs=pl.BlockSpec((None, 2, BLK, BLK), lambda h: (h, 0, 0, 0)),
        compiler_params=_params("parallel"),
        name="t5_bias_tiles",
    )(rel_bias.astype(F32), jnp.asarray(_bucket_tiles()))


def _dsa_kernel(aq_ref, iq_ref, ikwq_ref, ak_ref, av_ref, ikw_ref, bias_ref, o_ref,
                s_ref, acc_ref, av1_ref, *, n_keep):
    i = pl.program_id(1)
    row = lax.broadcasted_iota(jnp.int32, (BLK, BLK), 0)
    col = lax.broadcasted_iota(jnp.int32, (BLK, BLK), 1)
    qf = col + i * BLK
    heads = [slice(h * A_HEAD_DIM, (h + 1) * A_HEAD_DIM) for h in range(A_HEADS)]
    heads1 = [slice(h * 2 * A_HEAD_DIM, (h + 1) * 2 * A_HEAD_DIM) for h in range(A_HEADS)]

    @pl.when(i == 0)
    def _():
        def fill(j, _):
            ones = jnp.ones((BLK, A_HEAD_DIM), BF16)
            av1_ref[j] = jnp.concatenate([x for hs in heads for x in (av_ref[j, :, hs], ones)], axis=1)
            return 0
        lax.fori_loop(0, av_ref.shape[0], fill, 0)

    s_ref[0] = jnp.where((row >= LEAD) & (row <= qf), BIG, -BIG)
    eye = (lax.broadcasted_iota(jnp.int32, (128, 128), 0)
           == lax.broadcasted_iota(jnp.int32, (128, 128), 1)).astype(BF16)
    ikwq_t = lax.dot_general(eye, ikwq_ref[...], NT_DIMS, preferred_element_type=F32)
    w_scale = (IDX_DIM ** -0.5) * (IDX_HEADS ** -0.5)
    w_rows = [ikwq_t[IDX_DIM + h:IDX_DIM + h + 1, :] * w_scale for h in range(IDX_HEADS)]
    iq_heads = [iq_ref[:, h * IDX_DIM:(h + 1) * IDX_DIM] for h in range(IDX_HEADS)]

    def score_blocks(js, mabs):
        accs = [jnp.zeros((BLK, BLK), F32) for _ in js]
        for h in range(IDX_HEADS):
            for t, j in enumerate(js):
                st = lax.dot_general(ikw_ref[j, :, :IDX_DIM], iq_heads[h], NT_DIMS, preferred_element_type=F32)
                accs[t] = accs[t] + jnp.maximum(st, 0.0) * w_rows[h]
        for j, acc in zip(js, accs):
            causal = (row + j * BLK) <= qf
            s_ref[j] = jnp.where(causal, acc, -BIG)
            mabs = jnp.maximum(mabs, jnp.max(jnp.where(causal, jnp.abs(acc), 0.0), axis=0, keepdims=True))
        return mabs

    mabs = lax.fori_loop(0, jnp.right_shift(i, 2),
                         lambda t, m: score_blocks((4 * t + 1, 4 * t + 2, 4 * t + 3, 4 * t + 4), m),
                         jnp.zeros((1, BLK), F32))
    scored = i & ~3
    mabs = lax.cond((i & 2) == 2, lambda m: score_blocks((scored + 1, scored + 2), m), lambda m: m, mabs)
    mabs = lax.cond((i & 1) == 1, lambda m: score_blocks((i,), m), lambda m: m, mabs)

    k_eff = float(n_keep - N_META)
    bound = mabs * 1.000001 + 1e-30

    def count_ge(thr):
        def body(j, c):
            hit = (s_ref[j] >= thr).astype(F32).reshape(4, BLK // 32, 8, BLK)
            return c + jnp.sum(hit, axis=1)
        c = lax.fori_loop(1, i + 1, body, jnp.zeros((4, 8, BLK), F32))
        return jnp.sum(jnp.sum(c, axis=0), axis=0, keepdims=True)

    def unsettled(state):
        it, _, _, cnt_lo = state
        return (it < BISECT_ITERS) & (jnp.max(cnt_lo) > k_eff + PEEL_SLACK)

    def halve(lo, hi, cnt_lo):
        mid = lo + (hi - lo) * 0.5
        cnt = count_ge(mid)
        ge = cnt >= k_eff
        return jnp.where(ge, mid, lo), jnp.where(ge, hi, mid), jnp.where(ge, cnt, cnt_lo)

    def bisect(state):
        it, lo, hi, cnt_lo = state
        return (it + 2,) + halve(*halve(lo, hi, cnt_lo))

    _, lo, _, _ = lax.while_loop(unsettled, bisect, (jnp.int32(0), -bound, bound, count_ge(-bound)))

    def vmin_body(j, v):
        s = s_ref[j]
        return jnp.minimum(v, jnp.min(jnp.where(s >= lo, s, BIG), axis=0, keepdims=True))

    v = lax.fori_loop(1, i + 1, vmin_body, jnp.full((1, BLK), BIG, F32))
    v = jnp.where(v >= BIG, 0.0, v)

    def above(x):
        def body(j, carry):
            cnt, nxt = carry
            s = s_ref[j]
            gt = s > x
            cnt = cnt + jnp.sum(gt.astype(F32).reshape(4, BLK // 32, 8, BLK), axis=1)
            nxt = jnp.minimum(nxt, jnp.min(jnp.where(gt, s, BIG).reshape(4, BLK // 32, 8, BLK), axis=1))
            return cnt, nxt
        cnt, nxt = lax.fori_loop(1, i + 1, body, (jnp.zeros((4, 8, BLK), F32), jnp.full((4, 8, BLK), BIG, F32)))
        return (jnp.sum(jnp.sum(cnt, axis=0), axis=0, keepdims=True),
                jnp.min(jnp.min(nxt, axis=0), axis=0, keepdims=True))

    def peel(state):
        x, cnt_gt, nxt = state
        x = jnp.where(cnt_gt >= k_eff, nxt, x)
        return (x,) + above(x)

    v, cnt_gt, _ = lax.while_loop(lambda s: jnp.max(s[1]) >= k_eff, peel, (v,) + above(v))
    ties_wanted = k_eff - cnt_gt

    acc_ref[...] = jnp.zeros_like(acc_ref)
    tri_incl = (col <= row).astype(BF16)
    aq_heads = [aq_ref[:, hs] * (A_HEAD_DIM ** -0.5) for hs in heads]

    def attend(js, carry, near):
        ties_seen, ms = carry
        negs = []
        for j in js:
            s = s_ref[j]
            eq = s == v
            rank = ties_seen + jnp.dot(tri_incl, eq.astype(BF16), preferred_element_type=F32)
            sel = (s > v) | (eq & (rank <= ties_wanted))
            negs.append(jnp.where(sel, 0.0, -BIG).T)
            ties_seen = rank[BLK - 1:BLK, :]

        def logits(t, h):
            z = lax.dot_general(aq_heads[h], ak_ref[js[t], :, heads[h]], NT_DIMS, preferred_element_type=F32)
            return z + (bias_ref[h, i - js[t]] + negs[t] if near else negs[t])

        zs = {(t, h): logits(t, h) for h in range(A_HEADS) for t in range(len(js))}
        ms_new = []
        for h in range(A_HEADS):
            zh = [zs.pop((t, h)) for t in range(len(js))]
            m_new = jnp.maximum(ms[h], jnp.max(functools.reduce(jnp.maximum, zh), axis=1, keepdims=True))
            alpha = jnp.exp(ms[h] - m_new)
            ms_new.append(m_new)
            acc = alpha * acc_ref[h]
            for t, z in enumerate(zh):
                acc = acc + jnp.dot(jnp.exp(z - m_new).astype(BF16), av1_ref[js[t], :, heads1[h]],
                                    preferred_element_type=F32)
            acc_ref[h] = acc
        return ties_seen, tuple(ms_new)

    init = (jnp.zeros((1, BLK), F32), tuple(jnp.full((BLK, 1), -BIG, F32) for _ in heads))
    n_far = jnp.maximum(i - 1, 0)
    carry = lax.fori_loop(0, jnp.right_shift(n_far, 2),
                          lambda t, c: attend((4 * t, 4 * t + 1, 4 * t + 2, 4 * t + 3), c, False), init)
    done = n_far & ~3
    carry = lax.cond((n_far & 2) == 2, lambda c: attend((done, done + 1), c, False), lambda c: c, carry)
    carry = lax.cond((n_far & 1) == 1, lambda c: attend((n_far - 1,), c, False), lambda c: c, carry)

    @pl.when(i >= 1)
    def _():
        attend((i - 1, i), carry, True)

    @pl.when(i == 0)
    def _():
        attend((0,), carry, True)

    valid_q = (lax.broadcasted_iota(jnp.int32, (BLK, 1), 0) + i * BLK) >= LEAD
    outs = []
    for h in range(A_HEADS):
        acc = acc_ref[h]
        outs.append((acc / pltpu.roll(acc, A_HEAD_DIM, 1))[:, :A_HEAD_DIM])
    o_ref[...] = jnp.where(valid_q, jnp.concatenate(outs, axis=1), 0.0).astype(o_ref.dtype)


def _dsa_attention(p0, ikw, bias, nbatch, nb, n_keep):
    rows = p0.shape[0]
    aw = A_HEADS * A_HEAD_DIM
    p0b = p0.reshape(nbatch, nb, BLK, p0.shape[1])
    ikwb = ikw.reshape(nbatch, nb, BLK, ikw.shape[1])
    return pl.pallas_call(
        functools.partial(_dsa_kernel, n_keep=n_keep),
        out_shape=jax.ShapeDtypeStruct((rows, aw), BF16),
        grid=(nbatch, nb),
        in_specs=[pl.BlockSpec((BLK, aw), lambda b, i: (b * nb + i, 0)),
                  pl.BlockSpec((BLK, aw), lambda b, i: (b * nb + i, 3)),
                  pl.BlockSpec((BLK, 128), lambda b, i: (b * nb + i, 0)),
                  pl.BlockSpec((None, nb, BLK, aw), lambda b, i: (b, 0, 0, 1)),
                  pl.BlockSpec((None, nb, BLK, aw), lambda b, i: (b, 0, 0, 2)),
                  pl.BlockSpec((None, nb, BLK, 128), lambda b, i: (b, 0, 0, 0)),
                  pl.BlockSpec((A_HEADS, 2, BLK, BLK), lambda b, i: (0, 0, 0, 0))],
        out_specs=pl.BlockSpec((BLK, aw), lambda b, i: (b * nb + i, 0)),
        scratch_shapes=[pltpu.VMEM((nb, BLK, BLK), F32),
                        pltpu.VMEM((A_HEADS, BLK, 2 * A_HEAD_DIM), F32),
                        pltpu.VMEM((nb, BLK, 2 * aw), BF16)],
        compiler_params=_params("parallel", "arbitrary"),
        name="dsa_attention",
    )(p0, p0, ikw, p0b, p0b, ikwb, bias)


def _retention_kernel(bq_ref, bk_ref, bv_ref, bg_ref, cos_ref, sin_ref, dmat_ref, xi_ref, zeta_ref,
                      gain_ref, o_ref, r_ref, *, g_chunk):
    @pl.when(pl.program_id(1) == 0)
    def _():
        r_ref[...] = jnp.zeros_like(r_ref)

    cosf = cos_ref[...]
    sinf = sin_ref[...]

    def rot(x):
        return x * cosf + pltpu.roll(x, B_QK_DIM // 2, 1) * sinf

    for h in range(B_HEADS):
        ks = slice(h * B_QK_DIM, (h + 1) * B_QK_DIM)
        vs = slice(h * B_V_DIM, (h + 1) * B_V_DIM)
        q = rot(bq_ref[:, ks].astype(F32))
        k = rot(bk_ref[:, ks].astype(F32)) * (B_QK_DIM ** -0.5)
        qb = q.astype(BF16)
        kb = k.astype(BF16)
        v = bv_ref[:, vs]
        inner = lax.dot_general(qb, kb, NT_DIMS, preferred_element_type=F32) * dmat_ref[h]
        r_old = r_ref[h]
        o = (jnp.dot(inner.astype(BF16), v, preferred_element_type=F32)
             + jnp.dot(qb, r_old.astype(BF16), preferred_element_type=F32) * xi_ref[h])
        kz = (k * zeta_ref[h]).astype(BF16)
        r_ref[h] = r_old * g_chunk[h] + lax.dot_general(kz, v, TN_DIMS, preferred_element_type=F32)
        mu = jnp.mean(o, axis=-1, keepdims=True)
        oc = o - mu
        var = jnp.mean(oc * oc, axis=-1, keepdims=True)
        rn = oc * lax.rsqrt(var + EPS) * gain_ref[:, vs]
        gate = bg_ref[:, vs].astype(F32)
        o_ref[:, vs] = (rn * (gate / (1.0 + jnp.exp(-gate)))).astype(o_ref.dtype)


def _retention(p0, gn_gain, nbatch, nb):
    rows = p0.shape[0]
    qkw, vw = B_HEADS * B_QK_DIM, B_HEADS * B_V_DIM
    frame = nb * BLK
    half = B_QK_DIM // 2
    pos = (jnp.arange(frame) - LEAD).astype(F32)
    inv = 1.0 / (ROPE_BASE ** (jnp.arange(half, dtype=F32) / half))
    ang = pos[:, None] * inv[None, :]
    cosf = jnp.concatenate([jnp.cos(ang), jnp.cos(ang)], axis=-1)
    sinf = jnp.concatenate([-jnp.sin(ang), jnp.sin(ang)], axis=-1)
    lg = jnp.log(1.0 - 2.0 ** (-5.0 - jnp.arange(B_HEADS, dtype=F32)))
    n = jnp.arange(BLK, dtype=F32)
    diff = n[:, None] - n[None, :]
    dmat = jnp.where(diff[None] >= 0, jnp.exp(jnp.maximum(diff, 0.0)[None] * lg[:, None, None]), 0.0)
    xi = jnp.broadcast_to(jnp.exp((n[None, :] + 1.0) * lg[:, None])[..., None], (B_HEADS, BLK, B_V_DIM))
    zeta = jnp.broadcast_to(jnp.exp((BLK - 1.0 - n[None, :]) * lg[:, None])[..., None],
                            (B_HEADS, BLK, B_QK_DIM))
    g_chunk = tuple(float(math.exp(BLK * math.log(1.0 - 2.0 ** (-5.0 - h)))) for h in range(B_HEADS))
    const = lambda shape: pl.BlockSpec(shape, lambda b, i: (0,) * len(shape))
    return pl.pallas_call(
        functools.partial(_retention_kernel, g_chunk=g_chunk),
        out_shape=jax.ShapeDtypeStruct((rows, vw), BF16),
        grid=(nbatch, nb),
        in_specs=[pl.BlockSpec((BLK, qkw), lambda b, i: (b * nb + i, 4)),
                  pl.BlockSpec((BLK, qkw), lambda b, i: (b * nb + i, 5)),
                  pl.BlockSpec((BLK, vw), lambda b, i: (b * nb + i, 3)),
                  pl.BlockSpec((BLK, vw), lambda b, i: (b * nb + i, 4)),
                  pl.BlockSpec((BLK, B_QK_DIM), lambda b, i: (i, 0)),
                  pl.BlockSpec((BLK, B_QK_DIM), lambda b, i: (i, 0)),
                  const((B_HEADS, BLK, BLK)),
                  const((B_HEADS, BLK, B_V_DIM)),
                  const((B_HEADS, BLK, B_QK_DIM)),
                  const((1, vw))],
        out_specs=pl.BlockSpec((BLK, vw), lambda b, i: (b * nb + i, 0)),
        scratch_shapes=[pltpu.VMEM((B_HEADS, B_QK_DIM, B_V_DIM), F32)],
        compiler_params=_params("parallel", "arbitrary"),
        name="retention",
    )(p0, p0, p0, p0, cosf, sinf, dmat, xi, zeta, gn_gain.reshape(1, vw).astype(F32))


def _proj_residual_kernel(*refs, n_pairs, final):
    x = refs[2 * n_pairs][...]
    for t in range(n_pairs):
        x = x + jnp.dot(refs[2 * t][...], refs[2 * t + 1][...], preferred_element_type=F32)
    if final:
        ng_ref, o_ref = refs[2 * n_pairs + 1:]
        ms = jnp.mean(x * x, axis=-1, keepdims=True)
        o_ref[...] = x * lax.rsqrt(ms + EPS) * ng_ref[...]
    else:
        refs[2 * n_pairs + 1][...] = x


def _proj_residual(pairs, h, tm, final=None):
    rows, d = h.shape
    in_specs, args = [], []
    for a, w, *blk in pairs:
        kb, ac, wr = blk if blk else (a.shape[1], 0, 0)
        in_specs += [pl.BlockSpec((tm, kb), lambda i, ac=ac: (i, ac)),
                     pl.BlockSpec((kb, d), lambda i, wr=wr: (wr, 0))]
        args += [a, w]
    in_specs.append(pl.BlockSpec((tm, d), lambda i: (i, 0)))
    args.append(h)
    if final is None:
        out_shape = jax.ShapeDtypeStruct((rows, d), F32)
        out_spec = pl.BlockSpec((tm, d), lambda i: (i, 0))
        sem = "parallel"
    else:
        gain, nbatch, nb = final
        assert tm == BLK
        in_specs.append(pl.BlockSpec((1, d), lambda i: (0, 0)))
        args.append(gain.reshape(1, d).astype(F32))
        out_shape = jax.ShapeDtypeStruct((nbatch, (nb - 1) * BLK, d), F32)
        out_spec = pl.BlockSpec((None, BLK, d), lambda i: (i // nb, jnp.maximum(i % nb - 1, 0), 0))
        sem = "arbitrary"
    return pl.pallas_call(
        functools.partial(_proj_residual_kernel, n_pairs=len(pairs), final=final is not None),
        out_shape=out_shape,
        grid=(rows // tm,),
        in_specs=in_specs,
        out_specs=out_spec,
        compiler_params=_params(sem),
        name="proj_residual",
    )(*args)


GATE_COLS = 256


def _norm_gate_kernel(x_ref, gn_ref, wu_ref, wg_ref, cw_ref, cb_ref, o_ref, halo_ref):
    @pl.when(pl.program_id(0) == 0)
    def _():
        halo_ref[...] = jnp.zeros_like(halo_ref)

    x = x_ref[...]
    ms = jnp.mean(x * x, axis=-1, keepdims=True)
    xn = (x * lax.rsqrt(ms + EPS) * gn_ref[...]).astype(BF16)
    tm = x.shape[0]
    row8 = lax.broadcasted_iota(jnp.int32, (8, GATE_COLS), 0)
    for c in range(o_ref.shape[1] // GATE_COLS):
        cs = slice(c * GATE_COLS, (c + 1) * GATE_COLS)
        u = jnp.dot(xn, wu_ref[:, cs], preferred_element_type=F32)
        g = jnp.dot(xn, wg_ref[:, cs], preferred_element_type=F32)
        prev = halo_ref[:, cs]
        halo_ref[:, cs] = g[tm - 8:, :]
        g1 = pltpu.roll(g, 1, 0)
        g2 = pltpu.roll(g, 2, 0)
        g1 = jnp.concatenate([jnp.where(row8 >= 1, g1[:8], prev[7:8]), g1[8:]], axis=0)
        g2 = jnp.concatenate([jnp.where(row8 >= 2, g2[:8], jnp.where(row8 == 1, prev[7:8], prev[6:7])),
                              g2[8:]], axis=0)
        gc = g2 * cw_ref[0:1, cs] + g1 * cw_ref[1:2, cs] + g * cw_ref[2:3, cs] + cb_ref[:, cs]
        o_ref[:, cs] = ((gc / (1.0 + jnp.exp(-gc))) * u).astype(o_ref.dtype)


def _norm_gated_up(x, gain, w_up, w_gate, conv_w, conv_b, tm):
    rows, d = x.shape
    dff = w_up.shape[1]
    assert dff % GATE_COLS == 0
    const = lambda shape: pl.BlockSpec(shape, lambda i: (0, 0))
    return pl.pallas_call(
        _norm_gate_kernel,
        out_shape=jax.ShapeDtypeStruct((rows, dff), BF16),
        grid=(rows // tm,),
        in_specs=[pl.BlockSpec((tm, d), lambda i: (i, 0)),
                  const((1, d)), const((d, dff)), const((d, dff)), const((CONV_WIDTH, dff)), const((1, dff))],
        out_specs=pl.BlockSpec((tm, dff), lambda i: (i, 0)),
        scratch_shapes=[pltpu.VMEM((8, dff), F32)],
        compiler_params=_params("arbitrary"),
        name="norm_gated_up",
    )(x, gain.reshape(1, d), w_up.astype(BF16), w_gate.astype(BF16), conv_w.astype(F32),
      conv_b.reshape(1, dff).astype(F32))


SB_GROUP = 8
SB_DEAD = 110.0


def _stick_breaking_kernel(q_ref, k_ref, v_ref, o_ref, acc_ref):
    i = pl.program_id(2)
    row = lax.broadcasted_iota(jnp.int32, (BLK, BLK), 0)
    col = lax.broadcasted_iota(jnp.int32, (BLK, BLK), 1)
    tri_after = (row > col).astype(BF16)
    scale = C_HEAD_DIM ** -0.5

    heads = [slice(h * C_HEAD_DIM, (h + 1) * C_HEAD_DIM) for h in range(SB_GROUP)]
    q_heads = [q_ref[:, hs] * scale for hs in heads]
    acc_ref[...] = jnp.zeros_like(acc_ref)

    def tiles(blocks, carries):
        keys = [(b, h) for b in range(len(blocks)) for h in range(SB_GROUP)]
        zs = {(b, h): lax.dot_general(q_heads[h], k_ref[blocks[b][0], :, heads[h]], NT_DIMS,
                                      preferred_element_type=F32) for b, h in keys}
        log_beta, rests, sums = {}, {}, {}
        for b, h in keys:
            z, mask = zs[b, h], blocks[b][1]
            sp = jnp.maximum(z, 0.0) + jnp.log(1.0 + jnp.exp2(jnp.abs(z) * -LOG2E))
            if mask is not None:
                sp = jnp.where(mask, sp, 0.0)
            log_beta[b, h] = z - sp
            sums[b, h] = jnp.sum(sp, axis=1, keepdims=True)
            rests[b, h] = jnp.dot(sp.astype(BF16), tri_after, preferred_element_type=F32)
        run = list(carries)
        for b, h in keys:
            j, mask = blocks[b]
            a = jnp.exp2((log_beta[b, h] - (rests[b, h] + run[h])) * LOG2E)
            if mask is not None:
                a = jnp.where(mask, a, 0.0)
            acc_ref[h] += jnp.dot(a.astype(BF16), v_ref[j, :, heads[h]], preferred_element_type=F32)
            run[h] = run[h] + sums[b, h]
        return tuple(run)

    def alive(carries):
        return jnp.min(functools.reduce(jnp.minimum, carries)) < SB_DEAD

    qf = row + i * BLK
    kf = col + i * BLK
    diag = (kf < qf) & (kf >= LEAD)
    carries = lax.cond(
        i >= 2, lambda c: tiles([(i, diag), (i - 1, None)], c),
        lambda c: lax.cond(i == 1, lambda c1: tiles([(1, diag), (0, col >= LEAD)], c1),
                           lambda c1: tiles([(0, diag)], c1), c),
        tuple(jnp.zeros((BLK, 1), F32) for _ in heads))
    n_rest = jnp.maximum(i - 2, 0)
    n_pairs = jnp.right_shift(n_rest, 1)
    _, carries = lax.while_loop(
        lambda s: (s[0] < n_pairs) & alive(s[1]),
        lambda s: (s[0] + 1, tiles([(i - 2 - 2 * s[0], None), (i - 3 - 2 * s[0], None)], s[1])),
        (jnp.int32(0), carries))
    carries = lax.cond(((n_rest & 1) == 1) & alive(carries), lambda c: tiles([(1, None)], c), lambda c: c,
                       carries)

    @pl.when((i >= 2) & alive(carries))
    def _():
        tiles([(0, col >= LEAD)], carries)

    o_ref[...] = jnp.concatenate([acc_ref[h] for h in range(SB_GROUP)], axis=1).astype(o_ref.dtype)


def _stick_breaking(p1, nbatch, nb):
    rows = p1.shape[0]
    cw = C_HEADS * C_HEAD_DIM
    gw = SB_GROUP * C_HEAD_DIM
    ngroups = C_HEADS // SB_GROUP
    p1b = p1.reshape(nbatch, nb, BLK, p1.shape[1])
    return pl.pallas_call(
        _stick_breaking_kernel,
        out_shape=jax.ShapeDtypeStruct((rows, cw), BF16),
        grid=(nbatch, ngroups, nb),
        in_specs=[pl.BlockSpec((BLK, gw), lambda b, g, i: (b * nb + i, g)),
                  pl.BlockSpec((None, nb, BLK, gw), lambda b, g, i: (b, 0, 0, ngroups + g)),
                  pl.BlockSpec((None, nb, BLK, gw), lambda b, g, i: (b, 0, 0, 2 * ngroups + g))],
        out_specs=pl.BlockSpec((BLK, gw), lambda b, g, i: (b * nb + i, g)),
        scratch_shapes=[pltpu.VMEM((SB_GROUP, BLK, C_HEAD_DIM), F32)],
        compiler_params=_params("parallel", "parallel", "arbitrary"),
        name="stick_breaking",
    )(p1, p1b, p1b)


def _pick_tile(rows, pref):
    t = pref
    while rows % t:
        t //= 2
    return t


def _pick_cols(n, cap=2816):
    return max(t for t in range(128, min(n, cap) + 1, 128) if n % t == 0)


def kernel(x, meta_tokens, rel_bias, norm_mix, norm_ffn, norm_final, even_w_in, even_gn_gain, even_w_out, odd_w_in, odd_w_out, ffn_w_up, ffn_w_gate, ffn_conv_w, ffn_conv_b, ffn_w_down):
    nbatch, seq, d = x.shape
    assert seq % BLK == 0
    nb = seq // BLK + 1
    rows = nbatch * nb * BLK
    n_keep = min(TOPK_MAX, seq // 4)
    assert n_keep >= N_META
    depth = norm_mix.shape[0]
    tm = _pick_tile(rows, 1024)

    meta = jnp.broadcast_to(meta_tokens[None].astype(x.dtype), (nbatch, N_META, d))
    h = jnp.concatenate([jnp.zeros((nbatch, LEAD, d), x.dtype), meta, x], axis=1).reshape(rows, d)

    aw = A_HEADS * A_HEAD_DIM
    qkw, vw = B_HEADS * B_QK_DIM, B_HEADS * B_V_DIM
    cw = C_HEADS * C_HEAD_DIM
    bias = _bias_tiles(rel_bias)

    for l in range(depth):
        j = l // 2
        if l % 2 == 0:
            w = even_w_in[j]
            o_iq, o_ik, o_iw, o_bq = 3 * aw, 4 * aw, 4 * aw + IDX_DIM, 4 * aw + IDX_DIM + IDX_HEADS
            w_main = jnp.concatenate([w[:, :4 * aw], w[:, o_bq:]], axis=1).astype(BF16)
            w_idx = jnp.concatenate([w[:, o_ik:o_bq], jnp.zeros((d, 128 - IDX_DIM - IDX_HEADS), w.dtype)],
                                    axis=1).astype(BF16)
            p0, ikw = _norm_matmul(h, norm_mix[l], w_main, tm, _pick_cols(w_main.shape[1]), w_side=w_idx)
            a_out = _dsa_attention(p0, ikw, bias, nbatch, nb, n_keep)
            r_out = _retention(p0, even_gn_gain[j], nbatch, nb)
            w_out = even_w_out[j].astype(BF16)
            h = _proj_residual([(a_out, w_out, aw, 0, 0)]
                               + [(r_out, w_out, aw, t, t + 1) for t in range(vw // aw)], h, _pick_tile(rows, 512))
        else:
            p1 = _norm_matmul(h, norm_mix[l], odd_w_in[j].astype(BF16), tm, _pick_cols(odd_w_in.shape[2]))
            s_out = _stick_breaking(p1, nbatch, nb)
            h = _proj_residual([(s_out, odd_w_out[j].astype(BF16))], h, _pick_tile(rows, 512))
        act = _norm_gated_up(h, norm_ffn[l], ffn_w_up[l], ffn_w_gate[l], ffn_conv_w[l], ffn_conv_b[l], tm)
        w_down = ffn_w_down[l].astype(BF16)
        if l + 1 < depth:
            h = _proj_residual([(act, w_down)], h, _pick_tile(rows, 512))
        else:
            h = _proj_residual([(act, w_down)], h, BLK, final=(norm_final, nbatch, nb))
    return h
```
